```python
import jax, jax.numpy as jnp
from jax import lax
import numpy as np

D_MODEL = 1024
BATCH = 2
SEQ = 8192
DEPTH = 2

D_MIX = D_MODEL
RW_HEADS = 8
RW_HEAD_DIM = 64
RW_WIDTH = RW_HEADS * RW_HEAD_DIM
DECAY_LORA = 64
AAA_LORA = 64
GATE_LORA = 128
N_DIR = 2
MLA_HEADS = 8
QK_NOPE = 64
QK_ROPE = 32
V_HEAD = 64
Q_LORA = 256
KV_LORA = 128
MLA_WIDTH = MLA_HEADS * V_HEAD
ROPE_THETA = 10000.0
Q_BLOCK = 128
N_EXPERTS = 16
EC_FACTOR = 2
D_EXPERT = 1024
D_PLE = 256
NORM_EPS = 1e-6
GN_EPS = 64e-5

RW_COLS = 3 * RW_WIDTH + N_DIR * DECAY_LORA + N_DIR * AAA_LORA + GATE_LORA
MLA_COLS = Q_LORA + KV_LORA + QK_ROPE
IN_COLS = RW_COLS + MLA_COLS

kernel_name = 'hybrid_rwkv7_mla_expert_choice_encoder'


def rmsnorm(x, g, eps=NORM_EPS):
    xf = x.astype(jnp.float32)
    y = xf * lax.rsqrt(jnp.mean(xf * xf, axis=-1, keepdims=True) + eps)
    return (y * g.astype(jnp.float32)).astype(x.dtype)


def centred_shift_mix(z, mu):
    zp = jnp.pad(z, ((0, 0), (1, 1), (0, 0)))
    nb = 0.5 * (zp[:, :-2] + zp[:, 2:])
    return z + mu * (nb - z)


def rwkv7_scan(r, w, k, v, kk, a, reverse):
    b, _, h, n = r.shape
    xs = tuple(jnp.moveaxis(t, 1, 0) for t in (r, w, k, v, kk, a))
    s0 = jnp.zeros((b, h, n, n), jnp.float32)

    def step(s, inp):
        r_t, w_t, k_t, v_t, kk_t, a_t = inp
        sa = jnp.einsum('bhvk,bhk->bhv', s, -kk_t)
        s = (s * w_t[:, :, None, :] + sa[..., None] * (kk_t * a_t)[:, :, None, :]
             + v_t[..., None] * k_t[:, :, None, :])
        return s, jnp.einsum('bhvk,bhk->bhv', s, r_t)

    _, out = lax.scan(step, s0, xs, reverse=reverse)
    return jnp.moveaxis(out, 0, 1)


def rwkv7_group(z, mu, w0, w_up, a0, a_up, g_up, k_k, k_a, r_k, lnx_w, lnx_b):
    z = centred_shift_mix(z, mu).astype(jnp.float32)
    b, t, _ = z.shape
    o1 = RW_WIDTH
    o2 = 2 * RW_WIDTH
    o3 = 3 * RW_WIDTH
    o4 = o3 + N_DIR * DECAY_LORA
    o5 = o4 + N_DIR * AAA_LORA
    r = z[..., :o1]
    k = z[..., o1:o2]
    v = z[..., o2:o3]
    wd = z[..., o3:o4].reshape(b, t, N_DIR, DECAY_LORA)
    ad = z[..., o4:o5].reshape(b, t, N_DIR, AAA_LORA)
    gd = z[..., o5:]
    w_logit = w0 + jnp.einsum('btdl,dlc->btdc', jnp.tanh(wd), w_up)
    decay = jnp.exp(-jnp.exp(-jax.nn.softplus(-w_logit) - 0.5))
    a = jax.nn.sigmoid(a0 + jnp.einsum('btdl,dlc->btdc', ad, a_up))
    g = jax.nn.sigmoid(gd) @ g_up

    def hs(u):
        return u.reshape(u.shape[:-1] + (RW_HEADS, RW_HEAD_DIM))

    kk = hs(k * k_k)
    kk = kk / jnp.maximum(jnp.sqrt(jnp.sum(kk * kk, axis=-1, keepdims=True)), 1e-12)
    k_dir = k[:, :, None, :] * (1.0 + (a - 1.0) * k_a)
    r_h = hs(r)
    v_h = hs(v)
    o_fwd = rwkv7_scan(r_h, hs(decay[:, :, 0]), hs(k_dir[:, :, 0]), v_h, kk, hs(a[:, :, 0]), False)
    o_bwd = rwkv7_scan(r_h, hs(decay[:, :, 1]), hs(k_dir[:, :, 1]), v_h, kk, hs(a[:, :, 1]), True)
    o = o_fwd + o_bwd
    mean = jnp.mean(o, axis=-1, keepdims=True)
    var = jnp.mean(jnp.square(o - mean), axis=-1, keepdims=True)
    o = ((o - mean) * lax.rsqrt(var + GN_EPS)).reshape(b, t, RW_WIDTH) * lnx_w + lnx_b
    k_bonus = hs(0.5 * (k_dir[:, :, 0] + k_dir[:, :, 1]))
    bonus = (jnp.sum(r_h * k_bonus * r_k, axis=-1, keepdims=True) * v_h).reshape(b, t, RW_WIDTH)
    return (o + bonus) * g


def rope_angles(positions):
    inv = ROPE_THETA ** (-jnp.arange(0, QK_ROPE, 2, dtype=jnp.float32) / QK_ROPE)
    ang = positions.astype(jnp.float32)[..., None] * inv
    return jnp.cos(ang), jnp.sin(ang)


def apply_rope(u, cos, sin):
    half = QK_ROPE // 2
    u1 = u[..., :half]
    u2 = u[..., half:]
    return jnp.concatenate([u1 * cos - u2 * sin, u1 * sin + u2 * cos], axis=-1)


def mla_group(z, cos, sin, q_norm, q_up, kv_norm, kv_up, o_norm):
    b, t, _ = z.shape
    qd = z[..., :Q_LORA]
    kvd = z[..., Q_LORA:Q_LORA + KV_LORA]
    kr = z[..., Q_LORA + KV_LORA:]
    q = (rmsnorm(qd, q_norm) @ q_up).reshape(b, t, MLA_HEADS, QK_NOPE + QK_ROPE)
    q_nope = q[..., :QK_NOPE]
    q_rope = apply_rope(q[..., QK_NOPE:], cos[:, :, None, :], sin[:, :, None, :])
    kv = (rmsnorm(kvd, kv_norm) @ kv_up).reshape(b, t, MLA_HEADS, QK_NOPE + V_HEAD)
    k_nope = kv[..., :QK_NOPE]
    v = kv[..., QK_NOPE:]
    k_rope = apply_rope(kr, cos, sin)
    scale = (QK_NOPE + QK_ROPE) ** -0.5
    nblk = t // Q_BLOCK

    def to_blocks(u):
        return jnp.moveaxis(u.reshape((b, nblk, Q_BLOCK) + u.shape[2:]), 1, 0)

    def attend(qs):
        qn, qr = qs
        s = (jnp.einsum('bqhd,bkhd->bhqk', qn, k_nope).astype(jnp.float32)
             + jnp.einsum('bqhr,bkr->bhqk', qr, k_rope).astype(jnp.float32)) * scale
        pr = jax.nn.softmax(s, axis=-1).astype(v.dtype)
        return jnp.einsum('bhqk,bkhd->bqhd', pr, v)

    o = lax.map(attend, (to_blocks(q_nope), to_blocks(q_rope)))
    o = jnp.moveaxis(o, 0, 1).reshape(b, t, MLA_WIDTH)
    return rmsnorm(o, o_norm)


def expert_choice_ffn(h, router, w_gate, w_up, w_down):
    b, t, _ = h.shape
    cap = EC_FACTOR * t // N_EXPERTS
    aff = jax.nn.softmax((h @ router).astype(jnp.float32), axis=-1)
    gate, idx = lax.top_k(jnp.swapaxes(aff, 1, 2), cap)
    b_idx = jnp.arange(b)[:, None, None]
    xe = h[b_idx, idx]
    hid = (jax.nn.silu(jnp.einsum('becd,edf->becf', xe, w_gate))
           * jnp.einsum('becd,edf->becf', xe, w_up))
    ye = jnp.einsum('becf,efd->becd', hid, w_down) * gate[..., None].astype(h.dtype)
    return jnp.zeros_like(h).at[b_idx, idx].add(ye.astype(h.dtype))


def setup_inputs(seed: int = 0) -> dict:
    key = jax.random.key(seed)
    ks = jax.random.split(key, 40)
    f32 = jnp.float32

    def nrm(k, shape, scale):
        return jax.random.normal(k, shape, f32) * scale

    def gain(k, shape):
        return 1.0 + 0.02 * jax.random.normal(k, shape, f32)

    offsets = jax.random.randint(ks[2], (BATCH, 1), 0, 1024, dtype=jnp.int32)
    positions = jnp.arange(SEQ, dtype=jnp.int32)[None, :] + offsets
    return {
        'x': nrm(ks[0], (BATCH, SEQ, D_MODEL), 1.0),
        'p': nrm(ks[1], (DEPTH, BATCH, SEQ, D_PLE), 1.0),
        'positions': positions,
        'attn_norm': gain(ks[3], (DEPTH, D_MODEL)),
        'w_in': nrm(ks[4], (DEPTH, D_MODEL, IN_COLS), D_MODEL ** -0.5),
        'rw_mu': jax.random.uniform(ks[5], (DEPTH, RW_COLS), f32),
        'rw_w0': jax.random.uniform(ks[6], (DEPTH, N_DIR, RW_WIDTH), f32, -5.0, 1.0),
        'rw_w_up': nrm(ks[7], (DEPTH, N_DIR, DECAY_LORA, RW_WIDTH), 0.5 * DECAY_LORA ** -0.5),
        'rw_a0': nrm(ks[8], (DEPTH, N_DIR, RW_WIDTH), 0.5),
        'rw_a_up': nrm(ks[9], (DEPTH, N_DIR, AAA_LORA, RW_WIDTH), AAA_LORA ** -0.5),
        'rw_g_up': nrm(ks[10], (DEPTH, GATE_LORA, RW_WIDTH), GATE_LORA ** -0.5),
        'rw_k_k': 0.85 + 0.1 * jax.random.normal(ks[11], (DEPTH, RW_WIDTH), f32),
        'rw_k_a': 1.0 + 0.1 * jax.random.normal(ks[12], (DEPTH, RW_WIDTH), f32),
        'rw_r_k': nrm(ks[13], (DEPTH, RW_HEADS, RW_HEAD_DIM), 0.1),
        'rw_lnx_w': gain(ks[14], (DEPTH, RW_WIDTH)),
        'rw_lnx_b': nrm(ks[15], (DEPTH, RW_WIDTH), 0.02),
        'mla_q_norm': gain(ks[16], (DEPTH, Q_LORA)),
        'mla_q_up': nrm(ks[17], (DEPTH, Q_LORA, MLA_HEADS * (QK_NOPE + QK_ROPE)), Q_LORA ** -0.5),
        'mla_kv_norm': gain(ks[18], (DEPTH, KV_LORA)),
        'mla_kv_up': nrm(ks[19], (DEPTH, KV_LORA, MLA_HEADS * (QK_NOPE + V_HEAD)), KV_LORA ** -0.5),
        'mla_o_norm': gain(ks[20], (DEPTH, MLA_WIDTH)),
        'w_out': nrm(ks[21], (DEPTH, D_MIX, D_MODEL), D_MIX ** -0.5),
        'ffn_norm': gain(ks[22], (DEPTH, D_MODEL)),
        'router': nrm(ks[23], (DEPTH, D_MODEL, N_EXPERTS), D_MODEL ** -0.5),
        'exp_w_gate': nrm(ks[24], (DEPTH, N_EXPERTS, D_MODEL, D_EXPERT), D_MODEL ** -0.5),
        'exp_w_up': nrm(ks[25], (DEPTH, N_EXPERTS, D_MODEL, D_EXPERT), D_MODEL ** -0.5),
        'exp_w_down': nrm(ks[26], (DEPTH, N_EXPERTS, D_EXPERT, D_MODEL), D_EXPERT ** -0.5),
        'ple_norm': gain(ks[27], (DEPTH, D_MODEL)),
        'ple_proj': nrm(ks[28], (DEPTH, D_PLE, D_MODEL), D_PLE ** -0.5),
        'ple_gate': nrm(ks[29], (DEPTH, D_MODEL, D_MODEL), D_MODEL ** -0.5),
        'final_norm': gain(ks[30], (D_MODEL,)),
    }


def reference(x, p, positions, attn_norm, w_in, rw_mu, rw_w0, rw_w_up, rw_a0, rw_a_up,
              rw_g_up, rw_k_k, rw_k_a, rw_r_k, rw_lnx_w, rw_lnx_b, mla_q_norm, mla_q_up,
              mla_kv_norm, mla_kv_up, mla_o_norm, w_out, ffn_norm, router, exp_w_gate,
              exp_w_up, exp_w_down, ple_norm, ple_proj, ple_gate, final_norm):
    cos, sin = rope_angles(positions)
    for i in range(DEPTH):
        h = rmsnorm(x, attn_norm[i])
        z = h @ w_in[i]
        y_rw = rwkv7_group(z[..., :RW_COLS], rw_mu[i], rw_w0[i], rw_w_up[i], rw_a0[i],
                           rw_a_up[i], rw_g_up[i], rw_k_k[i], rw_k_a[i], rw_r_k[i],
                           rw_lnx_w[i], rw_lnx_b[i])
        y_mla = mla_group(z[..., RW_COLS:], cos, sin, mla_q_norm[i], mla_q_up[i],
                          mla_kv_norm[i], mla_kv_up[i], mla_o_norm[i])
        y = jnp.concatenate([y_rw.astype(x.dtype), y_mla.astype(x.dtype)], axis=-1)
        x = x + (y @ w_out[i]).astype(x.dtype)
        x = x + expert_choice_ffn(rmsnorm(x, ffn_norm[i]), router[i], exp_w_gate[i],
                                  exp_w_up[i], exp_w_down[i])
        ple = (p[i] @ ple_proj[i]) * jax.nn.sigmoid(rmsnorm(x, ple_norm[i]) @ ple_gate[i])
        x = x + ple.astype(x.dtype)
    return rmsnorm(x, final_norm)
```

```python
import functools

import jax
import jax.numpy as jnp
from jax import lax
from jax.experimental import pallas as pl
from jax.experimental.pallas import tpu as pltpu

F32 = jnp.float32
BF16 = jnp.bfloat16
I32 = jnp.int32
HIGHEST = lax.Precision.HIGHEST

RW_HEADS = 8
HEAD_DIM = 64
RW_WIDTH = RW_HEADS * HEAD_DIM
RW_COLS = 3 * RW_WIDTH + 2 * 64 + 2 * 64 + 128
MLA_HEADS = 8
QK_NOPE = 64
QK_ROPE = 32
V_HEAD = 64
Q_LORA = 256
KV_LORA = 128
MLA_WIDTH = MLA_HEADS * V_HEAD
MLA_IN = Q_LORA + KV_LORA + 2 * 128
ROPE_THETA = 10000.0
N_EXPERTS = 16
EC_FACTOR = 2
NORM_EPS = 1e-6
GN_EPS = 64e-5

LANES = 128
CHUNK = 64
PAIR = 2 * HEAD_DIM
N_PAIRS = RW_WIDTH // PAIR
VMEM_LIMIT = 56 * 1024 * 1024


def _cparams(sem):
    return pltpu.CompilerParams(dimension_semantics=sem, vmem_limit_bytes=VMEM_LIMIT)


def _rms(x, g):
    return x * lax.rsqrt(jnp.mean(x * x, axis=-1, keepdims=True) + NORM_EPS) * g


def _sigmoid(x):
    return 1.0 / (1.0 + jnp.exp(-x))


def _dot(a, b):
    return jnp.dot(a.astype(BF16), b.astype(BF16), preferred_element_type=F32)


def _dot32(a, b):
    return jnp.dot(a, b, preferred_element_type=F32, precision=HIGHEST)


def _dot32_nt(a, b):
    return lax.dot_general(a, b, (((1,), (1,)), ((), ())), preferred_element_type=F32,
                           precision=HIGHEST)


def _dot32_tn(a, b):
    return lax.dot_general(a, b, (((0,), (0,)), ((), ())), preferred_element_type=F32,
                           precision=HIGHEST)


def _dot_nt(a, b):
    return lax.dot_general(a.astype(BF16), b.astype(BF16), (((1,), (1,)), ((), ())),
                           preferred_element_type=F32)


def _rope_kernel(pos_ref, inv_ref, c_ref, s_ref):
    ang = pos_ref[...] * inv_ref[...]
    c_ref[...] = jnp.cos(ang)
    s_ref[...] = jnp.sin(ang)


def _rope_tables(positions):
    b, t = positions.shape
    m = b * t
    inv = ROPE_THETA ** (-jnp.arange(0, QK_ROPE, 2, dtype=F32) / QK_ROPE)
    inv_row = jnp.concatenate([jnp.zeros((QK_NOPE,), F32), inv, inv,
                               jnp.zeros((LANES - QK_NOPE - QK_ROPE,), F32)])[None, :]
    posf = jnp.broadcast_to(positions.astype(F32).reshape(m, 1), (m, LANES))
    tm = min(1024, m)
    return pl.pallas_call(
        _rope_kernel,
        grid=(m // tm,),
        in_specs=[pl.BlockSpec((tm, LANES), lambda i: (i, 0)),
                  pl.BlockSpec((1, LANES), lambda i: (0, 0))],
        out_specs=[pl.BlockSpec((tm, LANES), lambda i: (i, 0))] * 2,
        out_shape=[jax.ShapeDtypeStruct((m, LANES), F32)] * 2,
        compiler_params=_cparams(("parallel",)),
        name="rope_tables",
    )(posf, inv_row)


def _in_kernel(x_ref, g_ref, w_ref, o_ref):
    h = _rms(x_ref[...], g_ref[...])
    o_ref[...] = jnp.dot(h.astype(BF16), w_ref[...], preferred_element_type=F32)


def _in_proj(x2d, g, w_ext):
    m, d = x2d.shape
    n = w_ext.shape[1]
    tm = min(512, m)
    return pl.pallas_call(
        _in_kernel,
        grid=(m // tm,),
        in_specs=[pl.BlockSpec((tm, d), lambda i: (i, 0)),
                  pl.BlockSpec((1, d), lambda i: (0, 0)),
                  pl.BlockSpec((d, n), lambda i: (0, 0))],
        out_specs=pl.BlockSpec((tm, n), lambda i: (i, 0)),
        out_shape=jax.ShapeDtypeStruct((m, n), F32),
        compiler_params=_cparams(("parallel",)),
        name="in_proj",
    )(x2d, g, w_ext)


def _pair_masks():
    i = lax.broadcasted_iota(I32, (PAIR, PAIR), 0)
    j = lax.broadcasted_iota(I32, (PAIR, PAIR), 1)
    same = (i // CHUNK) == (j // CHUNK)
    li = i % CHUNK
    lj = j % CHUNK
    return same, li, lj, i == j


def _chunk_pair(r, kd, v, kk, a, lw, reverse, consts):
    tri, strict, incl, level_masks, eye, m0, m1 = consts
    cum = _dot32(tri, lw)
    last = cum[0:1] if reverse else cum[CHUNK - 1:CHUNK]
    g_inv = jnp.exp(-cum)
    b = kk * a
    a_t = kk * jnp.exp(cum - lw)
    b_t = b * g_inv
    k_t = kd * g_inv
    r_t = r * jnp.exp(cum)
    g_end = jnp.exp(last - cum)
    b_h = b * g_end
    k_h = kd * g_end

    def stack(x):
        return jnp.concatenate([jnp.where(m0, x, 0.0), jnp.where(m1, x, 0.0)], axis=0)

    a2, b2, k2, r2, v2, bh2, kh2 = (stack(x) for x in (a_t, b_t, k_t, r_t, v, b_h, k_h))
    prod = _dot32_nt(jnp.concatenate([a2, r2], axis=0), jnp.concatenate([b2, k2], axis=0))
    n_mat = jnp.where(strict, prod[:PAIR, :PAIR], 0.0)
    m_ak = jnp.where(strict, prod[:PAIR, PAIR:], 0.0)
    m_rb = jnp.where(incl, prod[PAIR:, :PAIR], 0.0)
    m_rk = jnp.where(incl, prod[PAIR:, PAIR:], 0.0)

    x = jnp.where(eye, 1.0, 0.0) - jnp.where(level_masks[0], n_mat, 0.0)
    for lm in level_masks[1:]:
        x = x - _dot32(x, _dot32(jnp.where(lm, n_mat, 0.0), x))

    mv = _dot32(jnp.concatenate([m_ak, m_rk], axis=0), v2)
    tw = _dot32(x, jnp.concatenate([a2, mv[:PAIR]], axis=1))
    qo = jnp.concatenate([r2, mv[PAIR:]], axis=1) - _dot32(m_rb, tw)
    bt = _dot32_tn(bh2, tw)
    kv = _dot32_tn(kh2, v2)
    g_mat = jnp.where(eye, jnp.exp(last), 0.0) - bt[:, :PAIR]
    h_mat = kv - bt[:, PAIR:]
    qo = qo[:CHUNK] + qo[CHUNK:]
    return qo[:, :PAIR], qo[:, PAIR:], g_mat, h_mat


def _rwkv_a_kernel(z_ref, zp_ref, zn_ref, mu_ref, w0_ref, wup_ref, a0_ref, aup_ref, gup_ref,
                   kk_ref, ka_ref, rk_ref, hsum_ref,
                   q_out, ol_out, g_out, h_out, bonus_out, gate_out,
                   r_s, v_s, kkn_s, kd_s, a_s, lw_s, *, tm, seq):
    i = pl.program_id(0)
    z = z_ref[...]
    row = lax.broadcasted_iota(I32, (tm, 1), 0)
    has_prev = (i * tm) % seq != 0
    has_next = ((i + 1) * tm) % seq != 0
    prev_row = jnp.where(has_prev, zp_ref[7:8, :], 0.0)
    next_row = jnp.where(has_next, zn_ref[0:1, :], 0.0)
    z_dn = jnp.where(row == 0, prev_row, pltpu.roll(z, 1, axis=0))
    z_up = jnp.where(row == tm - 1, next_row, pltpu.roll(z, tm - 1, axis=0))
    zs = z + mu_ref[...] * (0.5 * (z_dn + z_up) - z)

    w = RW_WIDTH
    r = zs[:, :w]
    k = zs[:, w:2 * w]
    v = zs[:, 2 * w:3 * w]
    wd = zs[:, 3 * w:3 * w + 128]
    ad = zs[:, 3 * w + 128:3 * w + 256]
    gd = zs[:, 3 * w + 256:3 * w + 384]

    hsum = hsum_ref[...]
    w_logit = w0_ref[...] + _dot32(jnp.tanh(wd), wup_ref[...])
    lw = -_sigmoid(w_logit) * jnp.exp(jnp.float32(-0.5))
    a = _sigmoid(a0_ref[...] + _dot32(ad, aup_ref[...]))
    gate_out[...] = _dot32(_sigmoid(gd), gup_ref[...])
    kkr = k * kk_ref[...]
    kkn = kkr / jnp.maximum(jnp.sqrt(_dot32(kkr * kkr, hsum)), 1e-12)
    kd0 = k * (1.0 + (a[:, :w] - 1.0) * ka_ref[...])
    kd1 = k * (1.0 + (a[:, w:] - 1.0) * ka_ref[...])
    bonus_out[...] = _dot32(r * (0.5 * (kd0 + kd1)) * rk_ref[...], hsum) * v

    r_s[...] = r
    v_s[...] = v
    kkn_s[...] = kkn
    kd_s[0] = kd0
    kd_s[1] = kd1
    a_s[0] = a[:, :w]
    a_s[1] = a[:, w:]
    lw_s[0] = lw[:, :w]
    lw_s[1] = lw[:, w:]

    same, li, lj, eye = _pair_masks()
    lane = lax.broadcasted_iota(I32, (1, PAIR), 1)
    m0 = lane < HEAD_DIM
    m1 = lane >= HEAD_DIM
    ti = lax.broadcasted_iota(I32, (CHUNK, CHUNK), 0)
    tj = lax.broadcasted_iota(I32, (CHUNK, CHUNK), 1)
    consts = []
    for reverse in (False, True):
        before = (lj > li) if reverse else (lj < li)
        tri = jnp.where((tj >= ti) if reverse else (tj <= ti), 1.0, 0.0)
        strict = same & before
        incl = same & (before | (li == lj))
        levels = []
        s = 1
        while s < CHUNK:
            blk = same & ((li // (2 * s)) == (lj // (2 * s)))
            hi_row = (li // s) % 2 == 1
            hi_col = (lj // s) % 2 == 1
            levels.append(blk & ((~hi_row & hi_col) if reverse else (hi_row & ~hi_col)))
            s *= 2
        consts.append((tri, strict, incl, levels, eye, m0, m1))

    def chunk_body(c, carry):
        rows = pl.ds(pl.multiple_of(c * CHUNK, CHUNK), CHUNK)
        for d, reverse in enumerate((False, True)):
            for p in range(N_PAIRS):
                lanes = slice(p * PAIR, (p + 1) * PAIR)
                qe, ol, g_mat, h_mat = _chunk_pair(
                    r_s[rows, lanes], kd_s[d, rows, lanes], v_s[rows, lanes], kkn_s[rows, lanes],
                    a_s[d, rows, lanes], lw_s[d, rows, lanes], reverse, consts[d])
                q_out[d, rows, lanes] = qe
                ol_out[d, rows, lanes] = ol
                g_out[d, c, :, lanes] = g_mat
                h_out[d, c, :, lanes] = h_mat
        return carry

    lax.fori_loop(0, tm // CHUNK, chunk_body, 0)


def _rwkv_a(z2d, seq, mu, w0, wup, a0, aup, gup, k_k, k_a, r_k, hsum):
    m = z2d.shape[0]
    tm = min(256, seq)
    nc = tm // CHUNK
    w = RW_WIDTH
    full = lambda shape: pl.BlockSpec(shape, lambda i: (0,) * len(shape))
    last8 = m // 8 - 1
    kern = functools.partial(_rwkv_a_kernel, tm=tm, seq=seq)
    return pl.pallas_call(
        kern,
        grid=(m // tm,),
        in_specs=[
            pl.BlockSpec((tm, RW_COLS), lambda i: (i, 0)),
            pl.BlockSpec((8, RW_COLS), lambda i: (jnp.maximum(i * (tm // 8) - 1, 0), 0)),
            pl.BlockSpec((8, RW_COLS), lambda i: (jnp.minimum((i + 1) * (tm // 8), last8), 0)),
            full((1, RW_COLS)), full((1, 2 * w)), full((128, 2 * w)), full((1, 2 * w)),
            full((128, 2 * w)), full((128, w)), full((1, w)), full((1, w)), full((1, w)),
            full((w, w)),
        ],
        out_specs=[
            pl.BlockSpec((2, tm, w), lambda i: (0, i, 0)),
            pl.BlockSpec((2, tm, w), lambda i: (0, i, 0)),
            pl.BlockSpec((2, nc, PAIR, w), lambda i: (0, i, 0, 0)),
            pl.BlockSpec((2, nc, PAIR, w), lambda i: (0, i, 0, 0)),
            pl.BlockSpec((tm, w), lambda i: (i, 0)),
            pl.BlockSpec((tm, w), lambda i: (i, 0)),
        ],
        out_shape=[
            jax.ShapeDtypeStruct((2, m, w), F32),
            jax.ShapeDtypeStruct((2, m, w), F32),
            jax.ShapeDtypeStruct((2, m // CHUNK, PAIR, w), F32),
            jax.ShapeDtypeStruct((2, m // CHUNK, PAIR, w), F32),
            jax.ShapeDtypeStruct((m, w), F32),
            jax.ShapeDtypeStruct((m, w), F32),
        ],
        scratch_shapes=[pltpu.VMEM((tm, w), F32)] * 3 + [pltpu.VMEM((2, tm, w), F32)] * 3,
        compiler_params=_cparams(("parallel",)),
        name="rwkv_chunk_local",
    )(z2d, z2d, z2d, mu, w0, wup, a0, aup, gup, k_k, k_a, r_k, hsum)


def _rwkv_b_kernel(q_ref, ol_ref, g_ref, h_ref, o_ref, s_ref, *, reverse, cb):
    @pl.when(pl.program_id(1) == 0)
    def _():
        s_ref[...] = jnp.zeros_like(s_ref)

    for step in range(cb):
        c = cb - 1 - step if reverse else step
        rows = slice(c * CHUNK, (c + 1) * CHUNK)
        for p in range(N_PAIRS):
            lanes = slice(p * PAIR, (p + 1) * PAIR)
            s = s_ref[:, lanes]
            o_ref[rows, lanes] = _dot32(q_ref[rows, lanes], s) + ol_ref[rows, lanes]
            s_ref[:, lanes] = _dot32(g_ref[c, :, lanes], s) + h_ref[c, :, lanes]


def _rwkv_b(qeff, oloc, g_all, h_all, batch, seq, d):
    m = batch * seq
    w = RW_WIDTH
    cb = min(4, seq // CHUNK)
    tm = cb * CHUNK
    nb = seq // tm
    reverse = d == 1

    def blk(b, j):
        return b * nb + (nb - 1 - j if reverse else j)

    kern = functools.partial(_rwkv_b_kernel, reverse=reverse, cb=cb)
    return pl.pallas_call(
        kern,
        grid=(batch, nb),
        in_specs=[
            pl.BlockSpec((None, tm, w), lambda b, j: (d, blk(b, j), 0)),
            pl.BlockSpec((None, tm, w), lambda b, j: (d, blk(b, j), 0)),
            pl.BlockSpec((None, cb, PAIR, w), lambda b, j: (d, blk(b, j), 0, 0)),
            pl.BlockSpec((None, cb, PAIR, w), lambda b, j: (d, blk(b, j), 0, 0)),
        ],
        out_specs=pl.BlockSpec((tm, w), lambda b, j: (blk(b, j), 0)),
        out_shape=jax.ShapeDtypeStruct((m, w), F32),
        scratch_shapes=[pltpu.VMEM((PAIR, w), F32)],
        compiler_params=_cparams(("parallel", "arbitrary")),
        name="rwkv_recurrence_bwd" if reverse else "rwkv_recurrence_fwd",
    )(qeff, oloc, g_all, h_all)


def _mla_prep_kernel(z_ref, c_ref, s_ref, qn_ref, kvn_ref, qa_ref, qb_ref, kk_ref, kvv_ref,
                     q_out, k_out, v_out, *, scale):
    z = z_ref[...]
    cos = c_ref[...]
    sin = s_ref[...]
    qd = _rms(z[:, :Q_LORA], qn_ref[...]).astype(BF16)
    kvd = _rms(z[:, Q_LORA:Q_LORA + KV_LORA], kvn_ref[...]).astype(BF16)
    o = Q_LORA + KV_LORA
    kr = z[:, o:o + LANES] * cos + z[:, o + LANES:o + 2 * LANES] * sin
    qa = jnp.dot(qd, qa_ref[...], preferred_element_type=F32)
    qb = jnp.dot(qd, qb_ref[...], preferred_element_type=F32)
    kn = jnp.dot(kvd, kk_ref[...], preferred_element_type=F32)
    v_out[...] = jnp.dot(kvd, kvv_ref[...], preferred_element_type=F32).astype(BF16)
    for h in range(MLA_HEADS):
        lanes = slice(h * LANES, (h + 1) * LANES)
        q_out[h] = ((qa[:, lanes] * cos + qb[:, lanes] * sin) * scale).astype(BF16)
        k_out[h] = (kn[:, lanes] + kr).astype(BF16)


def _mla_prep(z2d, cos, sin, batch, seq, q_norm, kv_norm, q_a, q_b, kv_k, kv_v):
    tm = min(512, seq)
    nt = seq // tm
    hw = MLA_HEADS * LANES
    scale = float((QK_NOPE + QK_ROPE) ** -0.5)
    full = lambda shape: pl.BlockSpec(shape, lambda b, i: (0,) * len(shape))
    col_blk = RW_COLS // MLA_IN
    assert col_blk * MLA_IN == RW_COLS
    return pl.pallas_call(
        functools.partial(_mla_prep_kernel, scale=scale),
        grid=(batch, nt),
        in_specs=[
            pl.BlockSpec((tm, MLA_IN), lambda b, i: (b * nt + i, col_blk)),
            pl.BlockSpec((tm, LANES), lambda b, i: (b * nt + i, 0)),
            pl.BlockSpec((tm, LANES), lambda b, i: (b * nt + i, 0)),
            full((1, Q_LORA)), full((1, KV_LORA)), full((Q_LORA, hw)), full((Q_LORA, hw)),
            full((KV_LORA, hw)), full((KV_LORA, MLA_WIDTH)),
        ],
        out_specs=[
            pl.BlockSpec((None, MLA_HEADS, tm, LANES), lambda b, i: (b, 0, i, 0)),
            pl.BlockSpec((None, MLA_HEADS, tm, LANES), lambda b, i: (b, 0, i, 0)),
            pl.BlockSpec((None, tm, MLA_WIDTH), lambda b, i: (b, i, 0)),
        ],
        out_shape=[
            jax.ShapeDtypeStruct((batch, MLA_HEADS, seq, LANES), BF16),
            jax.ShapeDtypeStruct((batch, MLA_HEADS, seq, LANES), BF16),
            jax.ShapeDtypeStruct((batch, seq, MLA_WIDTH), BF16),
        ],
        compiler_params=_cparams(("parallel", "parallel")),
        name="mla_prep",
    )(z2d, cos, sin, q_norm, kv_norm, q_a, q_b, kv_k, kv_v)


def _flash_kernel(q_ref, k_ref, v_ref, o_ref, m_ref, l_ref, acc_ref):
    j = pl.program_id(3)

    @pl.when(j == 0)
    def _():
        m_ref[...] = jnp.full_like(m_ref, -jnp.inf)
        l_ref[...] = jnp.zeros_like(l_ref)
        acc_ref[...] = jnp.zeros_like(acc_ref)

    v = v_ref[...]
    lane = lax.broadcasted_iota(I32, (1, LANES), 1)
    for h in range(2):
        s = lax.dot_general(q_ref[h], k_ref[h], (((1,), (1,)), ((), ())),
                            preferred_element_type=F32)
        m_prev = m_ref[h]
        m_new = jnp.maximum(m_prev, jnp.max(s, axis=-1, keepdims=True))
        alpha = jnp.exp(m_prev - m_new)
        p = jnp.exp(s - m_new)
        l_ref[h] = alpha * l_ref[h] + jnp.sum(p, axis=-1, keepdims=True)
        head_lanes = (lane < V_HEAD) if h == 0 else (lane >= V_HEAD)
        vh = jnp.where(head_lanes, v, jnp.zeros_like(v))
        acc_ref[h] = alpha * acc_ref[h] + jnp.dot(p.astype(BF16), vh, preferred_element_type=F32)
        m_ref[h] = m_new

    @pl.when(j == pl.num_programs(3) - 1)
    def _():
        o_ref[...] = acc_ref[0] / l_ref[0] + acc_ref[1] / l_ref[1]


def _flash(q, k, v):
    batch, heads, seq, _ = q.shape
    tq = min(512, seq)
    tk = min(512, seq)
    return pl.pallas_call(
        _flash_kernel,
        grid=(batch, heads // 2, seq // tq, seq // tk),
        in_specs=[
            pl.BlockSpec((None, 2, tq, LANES), lambda b, p, i, j: (b, p, i, 0)),
            pl.BlockSpec((None, 2, tk, LANES), lambda b, p, i, j: (b, p, j, 0)),
            pl.BlockSpec((None, tk, LANES), lambda b, p, i, j: (b, j, p)),
        ],
        out_specs=pl.BlockSpec((None, tq, LANES), lambda b, p, i, j: (b, i, p)),
        out_shape=jax.ShapeDtypeStruct((batch, seq, MLA_WIDTH), F32),
        scratch_shapes=[pltpu.VMEM((2, tq, 1), F32), pltpu.VMEM((2, tq, 1), F32),
                        pltpu.VMEM((2, tq, LANES), F32)],
        compiler_params=_cparams(("parallel", "parallel", "parallel", "arbitrary")),
        name="mla_flash",
    )(q, k, v)


def _out_kernel(of_ref, ob_ref, bonus_ref, gate_ref, ym_ref, x_ref, hsum_ref, lw_ref, lb_ref,
                on_ref, w_ref, o_ref):
    o = of_ref[...] + ob_ref[...]
    hsum = hsum_ref[...]
    inv_n = 1.0 / HEAD_DIM
    mean = _dot32(o, hsum) * inv_n
    d = o - mean
    var = _dot32(d * d, hsum) * inv_n
    y_rw = (d * lax.rsqrt(var + GN_EPS) * lw_ref[...] + lb_ref[...] + bonus_ref[...]) * gate_ref[...]
    y_mla = _rms(ym_ref[...], on_ref[...])
    w = RW_WIDTH
    o_ref[...] = (x_ref[...] + _dot(y_rw, w_ref[:w, :]) + _dot(y_mla, w_ref[w:, :]))


def _out_proj(o_f, o_b, bonus, gate, y_mla, x2d, hsum, lnx_w, lnx_b, o_norm, w_out):
    m, d = x2d.shape
    w = RW_WIDTH
    tm = min(256, m)
    row = lambda n: pl.BlockSpec((tm, n), lambda i: (i, 0))
    full = lambda shape: pl.BlockSpec(shape, lambda i: (0,) * len(shape))
    return pl.pallas_call(
        _out_kernel,
        grid=(m // tm,),
        in_specs=[row(w), row(w), row(w), row(w), row(MLA_WIDTH), row(d), full((w, w)),
                  full((1, w)), full((1, w)), full((1, MLA_WIDTH)), full((w + MLA_WIDTH, d))],
        out_specs=row(d),
        out_shape=jax.ShapeDtypeStruct((m, d), F32),
        compiler_params=_cparams(("parallel",)),
        name="out_proj",
    )(o_f, o_b, bonus, gate, y_mla, x2d, hsum, lnx_w, lnx_b, o_norm, w_out)


def _router_kernel(x_ref, g_ref, rt_ref, a_ref):
    xn = _rms(x_ref[...], g_ref[...])
    logits = _dot32_nt(rt_ref[...], xn)
    mx = jnp.max(logits, axis=0, keepdims=True)
    e = jnp.exp(logits - mx)
    a_ref[...] = e / jnp.sum(e, axis=0, keepdims=True)


def _router(x3d, g, router_t):
    batch, seq, d = x3d.shape
    e = router_t.shape[0]
    tm = min(512, seq)
    return pl.pallas_call(
        _router_kernel,
        grid=(batch, seq // tm),
        in_specs=[pl.BlockSpec((None, tm, d), lambda b, i: (b, i, 0)),
                  pl.BlockSpec((1, d), lambda b, i: (0, 0)),
                  pl.BlockSpec((e, d), lambda b, i: (0, 0))],
        out_specs=pl.BlockSpec((None, e, tm), lambda b, i: (b, 0, i)),
        out_shape=jax.ShapeDtypeStruct((batch, e, seq), F32),
        compiler_params=_cparams(("parallel", "parallel")),
        name="moe_router",
    )(x3d, g, router_t)


def _cumsum_lanes(src_ref, dst_ref, upper):
    rows, t = src_ref.shape

    def body(j, carry):
        cols = pl.ds(pl.multiple_of(j * LANES, LANES), LANES)
        cs = jnp.dot(src_ref[:, cols].astype(BF16), upper, preferred_element_type=F32) + carry
        dst_ref[:, cols] = cs
        return cs[:, LANES - 1:LANES]

    lax.fori_loop(0, t // LANES, body, jnp.zeros((rows, 1), F32))


def _select_kernel(a_ref, res_ref, m_ref, c_ref, rank_ref, *, cap, ct, tt):
    aff = a_ref[...]
    n_e, t = aff.shape
    bits = pltpu.bitcast(aff, I32)

    def search(i, cur):
        cand = cur | jnp.left_shift(jnp.int32(1), 30 - i)
        cnt = jnp.sum(jnp.where(bits >= cand, 1, 0), axis=1, keepdims=True)
        return jnp.where(cnt >= cap, cand, cur)

    thr = lax.fori_loop(0, 31, search, jnp.zeros((n_e, 1), I32))
    gt = bits > thr
    eq = bits == thr
    need = cap - jnp.sum(jnp.where(gt, 1, 0), axis=1, keepdims=True)

    ri = lax.broadcasted_iota(I32, (LANES, LANES), 0)
    ci = lax.broadcasted_iota(I32, (LANES, LANES), 1)
    upper = jnp.where(ri <= ci, 1.0, 0.0).astype(BF16)

    m_ref[...] = jnp.where(eq, 1.0, 0.0)
    _cumsum_lanes(m_ref, c_ref, upper)
    sel = gt | (eq & (c_ref[...] <= need.astype(F32)))
    m_ref[...] = jnp.where(sel, 1.0, 0.0)
    _cumsum_lanes(m_ref, c_ref, upper)
    rank_ref[...] = jnp.where(sel, c_ref[...] - 1.0, -1.0)

    c_iota = lax.broadcasted_iota(I32, (ct, 1), 0).astype(F32)
    vrow = lax.broadcasted_iota(I32, (8, 1), 0)
    t_iota = lax.broadcasted_iota(I32, (1, tt), 1)

    def per_expert(e, _):
        def per_ctile(ci_, _):
            c0 = pl.multiple_of(ci_ * ct, ct)

            def per_ttile(ti_, acc):
                t0 = pl.multiple_of(ti_ * tt, tt)
                rk = rank_ref[pl.ds(e, 1), pl.ds(t0, tt)]
                av = a_ref[pl.ds(e, 1), pl.ds(t0, tt)]
                onehot = jnp.where(rk - c0.astype(F32) == c_iota, 1.0, 0.0).astype(BF16)
                tpos = t_iota + t0
                t_hi = (tpos // LANES).astype(F32)
                t_lo = (tpos % LANES).astype(F32)
                a_hi = av.astype(BF16).astype(F32)
                a_mid = (av - a_hi).astype(BF16).astype(F32)
                a_lo = av - a_hi - a_mid
                vals = jnp.where(vrow == 0, t_hi, jnp.where(vrow == 1, t_lo, jnp.where(
                    vrow == 2, a_hi, jnp.where(vrow == 3, a_mid, jnp.where(vrow == 4, a_lo, 0.0)))))
                return acc + lax.dot_general(vals.astype(BF16), onehot, (((1,), (1,)), ((), ())),
                                             preferred_element_type=F32)

            acc = lax.fori_loop(0, t // tt, per_ttile, jnp.zeros((8, ct), F32))
            res_ref[e, :, pl.ds(c0, ct)] = acc
            return 0

        lax.fori_loop(0, cap // ct, per_ctile, 0)
        return 0

    lax.fori_loop(0, n_e, per_expert, 0)


def _select(aff_t, cap):
    batch, e, seq = aff_t.shape
    ct = min(256, cap)
    tt = min(512, seq)
    kern = functools.partial(_select_kernel, cap=cap, ct=ct, tt=tt)
    res = pl.pallas_call(
        kern,
        grid=(batch,),
        in_specs=[pl.BlockSpec((None, e, seq), lambda b: (b, 0, 0))],
        out_specs=pl.BlockSpec((None, e, 8, cap), lambda b: (b, 0, 0, 0)),
        out_shape=jax.ShapeDtypeStruct((batch, e, 8, cap), F32),
        scratch_shapes=[pltpu.VMEM((e, seq), F32)] * 3,
        compiler_params=_cparams(("parallel",)),
        name="moe_select",
    )(aff_t)
    idx = (res[:, :, 0, :] * LANES + res[:, :, 1, :]).astype(I32)
    gate = res[:, :, 2, :] + res[:, :, 3, :] + res[:, :, 4, :]
    return idx, gate


def _gather_kernel(idx_ref, x_ref, o_ref, *, cap, n_e):
    base = (pl.program_id(0) * n_e + pl.program_id(2)) * cap

    def body(c, carry):
        row = idx_ref[base + c]
        o_ref[pl.ds(c, 1), :] = x_ref[pl.ds(row, 1), :]
        return carry

    lax.fori_loop(0, cap, body, 0, unroll=8)


def _gather(idx_flat, x3d, n_e, cap):
    batch, seq, d = x3d.shape
    dt = 512
    return pl.pallas_call(
        functools.partial(_gather_kernel, cap=cap, n_e=n_e),
        grid_spec=pltpu.PrefetchScalarGridSpec(
            num_scalar_prefetch=1,
            grid=(batch, d // dt, n_e),
            in_specs=[pl.BlockSpec((None, seq, dt), lambda b, j, e, idx: (b, 0, j))],
            out_specs=pl.BlockSpec((None, None, cap, dt), lambda b, j, e, idx: (b, e, 0, j)),
        ),
        out_shape=jax.ShapeDtypeStruct((batch, n_e, cap, d), F32),
        compiler_params=_cparams(("parallel", "parallel", "arbitrary")),
        name="moe_gather",
    )(idx_flat, x3d)


def _ffn_kernel(x_ref, gate_ref, g_ref, wg_ref, wu_ref, wd_ref, o_ref):
    xn = _rms(x_ref[...], g_ref[...]).astype(BF16)
    h1 = jnp.dot(xn, wg_ref[...], preferred_element_type=F32)
    h2 = jnp.dot(xn, wu_ref[...], preferred_element_type=F32)
    hid = (h1 * _sigmoid(h1) * h2).astype(BF16)
    o_ref[...] = jnp.dot(hid, wd_ref[...], preferred_element_type=F32) * gate_ref[...]


def _expert_ffn(xe, gate_col, g, w_gate, w_up, w_down):
    batch, n_e, cap, d = xe.shape
    f = w_gate.shape[2]
    tc = min(256, cap)
    return pl.pallas_call(
        _ffn_kernel,
        grid=(n_e, batch, cap // tc),
        in_specs=[
            pl.BlockSpec((None, None, tc, d), lambda e, b, c: (b, e, c, 0)),
            pl.BlockSpec((None, None, tc, 1), lambda e, b, c: (b, e, c, 0)),
            pl.BlockSpec((1, d), lambda e, b, c: (0, 0)),
            pl.BlockSpec((None, d, f), lambda e, b, c: (e, 0, 0)),
            pl.BlockSpec((None, d, f), lambda e, b, c: (e, 0, 0)),
            pl.BlockSpec((None, f, d), lambda e, b, c: (e, 0, 0)),
        ],
        out_specs=pl.BlockSpec((None, None, tc, d), lambda e, b, c: (b, e, c, 0)),
        out_shape=jax.ShapeDtypeStruct((batch, n_e, cap, d), F32),
        compiler_params=_cparams(("parallel", "parallel", "parallel")),
        name="moe_ffn",
    )(xe, gate_col, g, w_gate, w_up, w_down)


def _scatter_kernel(idx_ref, x_ref, y_ref, o_ref, *, cap, n_e):
    e = pl.program_id(2)
    base = (pl.program_id(0) * n_e + e) * cap

    @pl.when(e == 0)
    def _():
        o_ref[...] = x_ref[...]

    def body(c, carry):
        row = idx_ref[base + c]
        o_ref[pl.ds(row, 1), :] = o_ref[pl.ds(row, 1), :] + y_ref[pl.ds(c, 1), :]
        return carry

    lax.fori_loop(0, cap, body, 0, unroll=8)


def _scatter_add(idx_flat, x3d, ye):
    batch, seq, d = x3d.shape
    _, n_e, cap, _ = ye.shape
    dt = 256
    return pl.pallas_call(
        functools.partial(_scatter_kernel, cap=cap, n_e=n_e),
        grid_spec=pltpu.PrefetchScalarGridSpec(
            num_scalar_prefetch=1,
            grid=(batch, d // dt, n_e),
            in_specs=[pl.BlockSpec((None, seq, dt), lambda b, j, e, idx: (b, 0, j)),
                      pl.BlockSpec((None, None, cap, dt), lambda b, j, e, idx: (b, e, 0, j))],
            out_specs=pl.BlockSpec((None, seq, dt), lambda b, j, e, idx: (b, 0, j)),
        ),
        out_shape=jax.ShapeDtypeStruct((batch, seq, d), F32),
        compiler_params=_cparams(("parallel", "parallel", "arbitrary")),
        name="moe_scatter_add",
    )(idx_flat, x3d, ye)


def _ple_kernel(x_ref, p_ref, g_ref, wp_ref, wg_ref, fg_ref, o_ref, *, final):
    x = x_ref[...]
    gate = _sigmoid(_dot(_rms(x, g_ref[...]), wg_ref[...]))
    out = x + _dot(p_ref[...], wp_ref[...]) * gate
    if final:
        out = _rms(out, fg_ref[...])
    o_ref[...] = out


def _ple(x2d, p2d, g, w_proj, w_gate, final_g, final):
    m, d = x2d.shape
    dp = p2d.shape[1]
    tm = min(512, m)
    full = lambda shape: pl.BlockSpec(shape, lambda i: (0,) * len(shape))
    return pl.pallas_call(
        functools.partial(_ple_kernel, final=final),
        grid=(m // tm,),
        in_specs=[pl.BlockSpec((tm, d), lambda i: (i, 0)), pl.BlockSpec((tm, dp), lambda i: (i, 0)),
                  full((1, d)), full((dp, d)), full((d, d)), full((1, d))],
        out_specs=pl.BlockSpec((tm, d), lambda i: (i, 0)),
        out_shape=jax.ShapeDtypeStruct((m, d), F32),
        compiler_params=_cparams(("parallel",)),
        name="ple_final" if final else "ple",
    )(x2d, p2d, g, w_proj, w_gate, final_g)


def _rot_cols(w):
    half = QK_ROPE // 2
    return jnp.concatenate([-w[..., half:], w[..., :half]], axis=-1)


def _pad_head(nope, rope):
    lead = (nope if nope is not None else rope).shape[:-1]
    n = nope if nope is not None else jnp.zeros(lead + (QK_NOPE,), F32)
    r = rope if rope is not None else jnp.zeros(lead + (QK_ROPE,), F32)
    return jnp.concatenate([n, r, jnp.zeros(lead + (LANES - QK_NOPE - QK_ROPE,), F32)], axis=-1)


def _block_rows(w_pair):
    z = jnp.zeros_like(w_pair[0])
    return jnp.concatenate([jnp.concatenate([w_pair[0], z], axis=1),
                            jnp.concatenate([z, w_pair[1]], axis=1)], axis=0)


def kernel(x, p, positions, attn_norm, w_in, rw_mu, rw_w0, rw_w_up, rw_a0, rw_a_up, rw_g_up, rw_k_k,
           rw_k_a, rw_r_k, rw_lnx_w, rw_lnx_b, mla_q_norm, mla_q_up, mla_kv_norm, mla_kv_up,
           mla_o_norm, w_out, ffn_norm, router, exp_w_gate, exp_w_up, exp_w_down, ple_norm,
           ple_proj, ple_gate, final_norm):
    batch, seq, d = x.shape
    depth = w_in.shape[0]
    m = batch * seq
    cap = EC_FACTOR * seq // N_EXPERTS
    w = RW_WIDTH

    cos, sin = _rope_tables(positions)
    hsum = (jnp.arange(w)[:, None] // HEAD_DIM == jnp.arange(w)[None, :] // HEAD_DIM).astype(F32)
    x2d = x.reshape(m, d)

    for i in range(depth):
        w_mla = w_in[i][:, RW_COLS:]
        w_kr = w_mla[:, Q_LORA + KV_LORA:]
        w_ext = jnp.concatenate([w_in[i][:, :RW_COLS], w_mla[:, :Q_LORA + KV_LORA],
                                 _pad_head(None, w_kr), _pad_head(None, _rot_cols(w_kr))],
                                axis=1).astype(BF16)
        q_up = mla_q_up[i].reshape(Q_LORA, MLA_HEADS, QK_NOPE + QK_ROPE)
        q_a = _pad_head(q_up[..., :QK_NOPE], q_up[..., QK_NOPE:]).reshape(Q_LORA, -1).astype(BF16)
        q_b = _pad_head(None, _rot_cols(q_up[..., QK_NOPE:])).reshape(Q_LORA, -1).astype(BF16)
        kv_up = mla_kv_up[i].reshape(KV_LORA, MLA_HEADS, QK_NOPE + V_HEAD)
        kv_k = _pad_head(kv_up[..., :QK_NOPE], None).reshape(KV_LORA, -1).astype(BF16)
        kv_v = kv_up[..., QK_NOPE:].reshape(KV_LORA, MLA_WIDTH).astype(BF16)

        z2d = _in_proj(x2d, attn_norm[i][None, :], w_ext)
        qeff, oloc, g_all, h_all, bonus, gate = _rwkv_a(
            z2d, seq, rw_mu[i][None, :], rw_w0[i].reshape(1, 2 * w), _block_rows(rw_w_up[i]),
            rw_a0[i].reshape(1, 2 * w), _block_rows(rw_a_up[i]), rw_g_up[i], rw_k_k[i][None, :],
            rw_k_a[i][None, :], rw_r_k[i].reshape(1, w), hsum)
        o_f = _rwkv_b(qeff, oloc, g_all, h_all, batch, seq, 0)
        o_b = _rwkv_b(qeff, oloc, g_all, h_all, batch, seq, 1)
        q, k, v = _mla_prep(z2d, cos, sin, batch, seq, mla_q_norm[i][None, :],
                            mla_kv_norm[i][None, :], q_a, q_b, kv_k, kv_v)
        y_mla = _flash(q, k, v).reshape(m, MLA_WIDTH)
        x2d = _out_proj(o_f, o_b, bonus, gate, y_mla, x2d, hsum, rw_lnx_w[i][None, :],
                        rw_lnx_b[i][None, :], mla_o_norm[i][None, :], w_out[i].astype(BF16))

        x3d = x2d.reshape(batch, seq, d)
        aff_t = _router(x3d, ffn_norm[i][None, :], router[i].T)
        idx, gates = _select(aff_t, cap)
        idx_flat = idx.reshape(-1)
        xe = _gather(idx_flat, x3d, N_EXPERTS, cap)
        ye = _expert_ffn(xe, gates.reshape(batch, N_EXPERTS, cap, 1), ffn_norm[i][None, :],
                         exp_w_gate[i].astype(BF16), exp_w_up[i].astype(BF16),
                         exp_w_down[i].astype(BF16))
        x2d = _scatter_add(idx_flat, x3d, ye).reshape(m, d)

        x2d = _ple(x2d, p[i].reshape(m, -1), ple_norm[i][None, :], ple_proj[i].astype(BF16),
                   ple_gate[i].astype(BF16), final_norm[None, :], final=(i == depth - 1))

    return x2d.reshape(batch, seq, d)
```

```python
import functools

import jax
import jax.numpy as jnp
from jax import lax
from jax.experimental import pallas as pl
from jax.experimental.pallas import tpu as pltpu

F32 = jnp.float32
BF16 = jnp.bfloat16
I32 = jnp.int32
HIGHEST = lax.Precision.HIGHEST

RW_HEADS = 8
HEAD_DIM = 64
RW_WIDTH = RW_HEADS * HEAD_DIM
RW_COLS = 3 * RW_WIDTH + 2 * 64 + 2 * 64 + 128
MLA_HEADS = 8
QK_NOPE = 64
QK_ROPE = 32
V_HEAD = 64
Q_LORA = 256
KV_LORA = 128
MLA_WIDTH = MLA_HEADS * V_HEAD
MLA_IN = Q_LORA + KV_LORA + 2 * 128
ROPE_THETA = 10000.0
N_EXPERTS = 16
EC_FACTOR = 2
NORM_EPS = 1e-6
GN_EPS = 64e-5

LANES = 128
CHUNK = 64
PAIR = 2 * HEAD_DIM
N_PAIRS = RW_WIDTH // PAIR
VMEM_LIMIT = 56 * 1024 * 1024
FLASH_TQ = 1024
FLASH_TK = 512


def _cparams(sem):
    return pltpu.CompilerParams(dimension_semantics=sem, vmem_limit_bytes=VMEM_LIMIT)


def _rms(x, g):
    return x * lax.rsqrt(jnp.mean(x * x, axis=-1, keepdims=True) + NORM_EPS) * g


def _sigmoid(x):
    return 1.0 / (1.0 + jnp.exp(-x))


def _dot(a, b):
    return jnp.dot(a.astype(BF16), b.astype(BF16), preferred_element_type=F32)


def _dot32(a, b):
    return jnp.dot(a, b, preferred_element_type=F32, precision=HIGHEST)


def _dot32_nt(a, b):
    return lax.dot_general(a, b, (((1,), (1,)), ((), ())), preferred_element_type=F32,
                           precision=HIGHEST)


def _dot32_tn(a, b):
    return lax.dot_general(a, b, (((0,), (0,)), ((), ())), preferred_element_type=F32,
                           precision=HIGHEST)


def _dot_tn(a, b):
    return lax.dot_general(a.astype(BF16), b.astype(BF16), (((0,), (0,)), ((), ())),
                           preferred_element_type=F32)


def _dot_nt(a, b):
    return lax.dot_general(a.astype(BF16), b.astype(BF16), (((1,), (1,)), ((), ())),
                           preferred_element_type=F32)


def _rope_kernel(pos_ref, inv_ref, c_ref, s_ref):
    ang = pos_ref[...] * inv_ref[...]
    c_ref[...] = jnp.cos(ang)
    s_ref[...] = jnp.sin(ang)


def _rope_tables(positions):
    b, t = positions.shape
    m = b * t
    inv = ROPE_THETA ** (-jnp.arange(0, QK_ROPE, 2, dtype=F32) / QK_ROPE)
    inv_row = jnp.concatenate([jnp.zeros((QK_NOPE,), F32), inv, inv,
                               jnp.zeros((LANES - QK_NOPE - QK_ROPE,), F32)])[None, :]
    posf = jnp.broadcast_to(positions.astype(F32).reshape(m, 1), (m, LANES))
    tm = min(1024, m)
    return pl.pallas_call(
        _rope_kernel,
        grid=(m // tm,),
        in_specs=[pl.BlockSpec((tm, LANES), lambda i: (i, 0)),
                  pl.BlockSpec((1, LANES), lambda i: (0, 0))],
        out_specs=[pl.BlockSpec((tm, LANES), lambda i: (i, 0))] * 2,
        out_shape=[jax.ShapeDtypeStruct((m, LANES), F32)] * 2,
        compiler_params=_cparams(("parallel",)),
        name="rope_tables",
    )(posf, inv_row)


def _in_kernel(x_ref, g_ref, w_ref, o_ref):
    h = _rms(x_ref[...], g_ref[...])
    o_ref[...] = jnp.dot(h.astype(BF16), w_ref[...], preferred_element_type=F32)


def _in_proj(x2d, g, w_ext):
    m, d = x2d.shape
    n = w_ext.shape[1]
    tm = min(512, m)
    return pl.pallas_call(
        _in_kernel,
        grid=(m // tm,),
        in_specs=[pl.BlockSpec((tm, d), lambda i: (i, 0)),
                  pl.BlockSpec((1, d), lambda i: (0, 0)),
                  pl.BlockSpec((d, n), lambda i: (0, 0))],
        out_specs=pl.BlockSpec((tm, n), lambda i: (i, 0)),
        out_shape=jax.ShapeDtypeStruct((m, n), F32),
        compiler_params=_cparams(("parallel",)),
        name="in_proj",
    )(x2d, g, w_ext)


def _pair_masks():
    i = lax.broadcasted_iota(I32, (PAIR, PAIR), 0)
    j = lax.broadcasted_iota(I32, (PAIR, PAIR), 1)
    same = (i // CHUNK) == (j // CHUNK)
    li = i % CHUNK
    lj = j % CHUNK
    return same, li, lj, i == j


def _chunk_pair(a_t, b_t, k_t, r_t, v, b_h, k_h, g_last, consts):
    n = len(a_t)
    idx = range(n)
    strict = [c[0] for c in consts]
    incl = [c[1] for c in consts]
    levels = [c[2] for c in consts]
    eye, m0, m1 = consts[0][3:]

    def stack(x):
        return jnp.concatenate([jnp.where(m0, x, jnp.zeros_like(x)),
                                jnp.where(m1, x, jnp.zeros_like(x))], axis=0)

    a2, b2, k2, v2, bh2, kh2, r2 = ([stack(x) for x in xs] for xs in (a_t, b_t, k_t, v, b_h, k_h, r_t))
    prod = [_dot_nt(jnp.concatenate([a2[i], r2[i].astype(BF16)], axis=0),
                    jnp.concatenate([b2[i], k2[i]], axis=0)) for i in idx]
    n_mat = [jnp.where(strict[i], prod[i][:PAIR, :PAIR], 0.0) for i in idx]
    m_ak = [jnp.where(strict[i], prod[i][:PAIR, PAIR:], 0.0).astype(BF16) for i in idx]
    m_rb = [jnp.where(incl[i], prod[i][PAIR:, :PAIR], 0.0).astype(BF16) for i in idx]
    m_rk = [jnp.where(incl[i], prod[i][PAIR:, PAIR:], 0.0).astype(BF16) for i in idx]
    mv = [_dot(jnp.concatenate([m_ak[i], m_rk[i]], axis=0), v2[i]) for i in idx]

    x = [jnp.where(eye, 1.0, 0.0) - jnp.where(levels[i][0], n_mat[i], 0.0) for i in idx]
    n_bf = [n_mat[i].astype(BF16) for i in idx]
    zero = jnp.zeros((PAIR, PAIR), BF16)
    for lv in range(1, len(levels[0])):
        cx = [_dot(jnp.where(levels[i][lv], n_bf[i], zero), x[i]) for i in idx]
        x = [x[i] - _dot(x[i], cx[i]) for i in idx]

    tw = [_dot(x[i], jnp.concatenate([a2[i], mv[i][:PAIR].astype(BF16)], axis=1))
          for i in idx]
    tw_bf = [t.astype(BF16) for t in tw]
    qo = [jnp.concatenate([r2[i], mv[i][PAIR:]], axis=1) - _dot(m_rb[i], tw_bf[i])
          for i in idx]
    bt = [_dot_tn(bh2[i], tw_bf[i]) for i in idx]
    kv = [_dot_tn(kh2[i], v2[i]) for i in idx]
    out = []
    for i in idx:
        g_mat = jnp.where(eye, g_last[i], 0.0) - bt[i][:, :PAIR]
        h_mat = kv[i] - bt[i][:, PAIR:]
        q = qo[i][:CHUNK] + qo[i][CHUNK:]
        out.append((q[:, :PAIR], q[:, PAIR:], g_mat, h_mat))
    return out


def _rwkv_a_kernel(z_ref, zp_ref, zn_ref, mu_ref, w0_ref, wup_ref, a0_ref, aup_ref, gup_ref,
                   kk_ref, ka_ref, rk_ref, hsum_ref, trif_ref, trib_ref, ones_ref,
                   q_out, ol_out, g_out, h_out, bonus_out, gate_out,
                   at_s, bt_s, kt_s, rt_s, bh_s, kh_s, v_s, gl_s, *, tm, seq):
    i = pl.program_id(0)
    z = z_ref[...]
    row = lax.broadcasted_iota(I32, (tm, 1), 0)
    has_prev = (i * tm) % seq != 0
    has_next = ((i + 1) * tm) % seq != 0
    prev_row = jnp.where(has_prev, zp_ref[7:8, :], 0.0)
    next_row = jnp.where(has_next, zn_ref[0:1, :], 0.0)
    z_dn = jnp.where(row == 0, prev_row, pltpu.roll(z, 1, axis=0))
    z_up = jnp.where(row == tm - 1, next_row, pltpu.roll(z, tm - 1, axis=0))
    zs = z + mu_ref[...] * (0.5 * (z_dn + z_up) - z)

    w = RW_WIDTH
    r = zs[:, :w]
    k = zs[:, w:2 * w]
    v = zs[:, 2 * w:3 * w]
    wd = zs[:, 3 * w:3 * w + 128]
    ad = zs[:, 3 * w + 128:3 * w + 256]
    gd = zs[:, 3 * w + 256:3 * w + 384]

    hsum = hsum_ref[...]
    w_logit = w0_ref[...] + _dot32(jnp.tanh(wd), wup_ref[...])
    lw = -_sigmoid(w_logit) * jnp.exp(jnp.float32(-0.5))
    a = _sigmoid(a0_ref[...] + _dot32(ad, aup_ref[...]))
    gate_out[...] = _dot32(_sigmoid(gd), gup_ref[...])
    kkr = k * kk_ref[...]
    kkn = kkr / jnp.maximum(jnp.sqrt(_dot32(kkr * kkr, hsum)), 1e-12)
    kd = [k * (1.0 + (a[:, d * w:(d + 1) * w] - 1.0) * ka_ref[...]) for d in range(2)]
    bonus_out[...] = _dot32(r * (0.5 * (kd[0] + kd[1])) * rk_ref[...], hsum) * v
    v_s[...] = v.astype(BF16)

    tot = _dot32(ones_ref[...], lw)
    for d, tri_ref in enumerate((trif_ref, trib_ref)):
        cols = slice(d * w, (d + 1) * w)
        lw_d = lw[:, cols]
        cum = _dot32(tri_ref[...], lw_d)
        g_inv = jnp.exp(-cum)
        g_end = jnp.exp(tot[:, cols] - cum)
        b = kkn * a[:, cols]
        at_s[d] = (kkn * jnp.exp(cum - lw_d)).astype(BF16)
        bt_s[d] = (b * g_inv).astype(BF16)
        kt_s[d] = (kd[d] * g_inv).astype(BF16)
        rt_s[d] = r * jnp.exp(cum)
        bh_s[d] = (b * g_end).astype(BF16)
        kh_s[d] = (kd[d] * g_end).astype(BF16)
        gl_s[d] = jnp.exp(tot[:, cols])

    same, li, lj, eye = _pair_masks()
    lane = lax.broadcasted_iota(I32, (1, PAIR), 1)
    m0 = lane < HEAD_DIM
    m1 = lane >= HEAD_DIM
    consts = []
    for reverse in (False, True):
        before = (lj > li) if reverse else (lj < li)
        strict = same & before
        incl = same & (before | (li == lj))
        levels = []
        s = 1
        while s < CHUNK:
            blk = same & ((li // (2 * s)) == (lj // (2 * s)))
            hi_row = (li // s) % 2 == 1
            hi_col = (lj // s) % 2 == 1
            levels.append(blk & ((~hi_row & hi_col) if reverse else (hi_row & ~hi_col)))
            s *= 2
        consts.append((strict, incl, levels, eye, m0, m1))

    def chunk_body(c, carry):
        r0 = pl.multiple_of(c * CHUNK, CHUNK)
        rows = pl.ds(r0, CHUNK)
        inst = [(d, slice(p * PAIR, (p + 1) * PAIR)) for d in range(2) for p in range(N_PAIRS)]
        outs = _chunk_pair(
            [at_s[d, rows, ln] for d, ln in inst], [bt_s[d, rows, ln] for d, ln in inst],
            [kt_s[d, rows, ln] for d, ln in inst], [rt_s[d, rows, ln] for d, ln in inst],
            [v_s[rows, ln] for d, ln in inst], [bh_s[d, rows, ln] for d, ln in inst],
            [kh_s[d, rows, ln] for d, ln in inst], [gl_s[d, pl.ds(r0, 1), ln] for d, ln in inst],
            [consts[d] for d, ln in inst])
        for (d, ln), (qe, ol, g_mat, h_mat) in zip(inst, outs):
            q_out[d, rows, ln] = qe
            ol_out[d, rows, ln] = ol
            g_out[d, c, :, ln] = g_mat
            h_out[d, c, :, ln] = h_mat
        return carry

    lax.fori_loop(0, tm // CHUNK, chunk_body, 0)


def _rwkv_a(z2d, seq, mu, w0, wup, a0, aup, gup, k_k, k_a, r_k, hsum):
    m = z2d.shape[0]
    tm = min(256, seq)
    nc = tm // CHUNK
    w = RW_WIDTH
    full = lambda shape: pl.BlockSpec(shape, lambda i: (0,) * len(shape))
    last8 = m // 8 - 1
    ti = jnp.arange(tm)[:, None]
    tj = jnp.arange(tm)[None, :]
    same_chunk = (ti // CHUNK) == (tj // CHUNK)
    tri_f = (same_chunk & (tj <= ti)).astype(F32)
    tri_b = (same_chunk & (tj >= ti)).astype(F32)
    ones_bd = same_chunk.astype(F32)
    kern = functools.partial(_rwkv_a_kernel, tm=tm, seq=seq)
    return pl.pallas_call(
        kern,
        grid=(m // tm,),
        in_specs=[
            pl.BlockSpec((tm, RW_COLS), lambda i: (i, 0)),
            pl.BlockSpec((8, RW_COLS), lambda i: (jnp.maximum(i * (tm // 8) - 1, 0), 0)),
            pl.BlockSpec((8, RW_COLS), lambda i: (jnp.minimum((i + 1) * (tm // 8), last8), 0)),
            full((1, RW_COLS)), full((1, 2 * w)), full((128, 2 * w)), full((1, 2 * w)),
            full((128, 2 * w)), full((128, w)), full((1, w)), full((1, w)), full((1, w)),
            full((w, w)), full((tm, tm)), full((tm, tm)), full((tm, tm)),
        ],
        out_specs=[
            pl.BlockSpec((2, tm, w), lambda i: (0, i, 0)),
            pl.BlockSpec((2, tm, w), lambda i: (0, i, 0)),
            pl.BlockSpec((2, nc, PAIR, w), lambda i: (0, i, 0, 0)),
            pl.BlockSpec((2, nc, PAIR, w), lambda i: (0, i, 0, 0)),
            pl.BlockSpec((tm, w), lambda i: (i, 0)),
            pl.BlockSpec((tm, w), lambda i: (i, 0)),
        ],
        out_shape=[
            jax.ShapeDtypeStruct((2, m, w), F32),
            jax.ShapeDtypeStruct((2, m, w), F32),
            jax.ShapeDtypeStruct((2, m // CHUNK, PAIR, w), F32),
            jax.ShapeDtypeStruct((2, m // CHUNK, PAIR, w), F32),
            jax.ShapeDtypeStruct((m, w), F32),
            jax.ShapeDtypeStruct((m, w), F32),
        ],
        scratch_shapes=[pltpu.VMEM((2, tm, w), BF16)] * 3 + [pltpu.VMEM((2, tm, w), F32)]
        + [pltpu.VMEM((2, tm, w), BF16)] * 2 + [pltpu.VMEM((tm, w), BF16), pltpu.VMEM((2, tm, w), F32)],
        compiler_params=_cparams(("parallel",)),
        name="rwkv_chunk_local",
    )(z2d, z2d, z2d, mu, w0, wup, a0, aup, gup, k_k, k_a, r_k, hsum, tri_f, tri_b, ones_bd)


def _rwkv_b_kernel(q_ref, ol_ref, g_ref, h_ref, o_ref, s_ref, *, reverse, cb):
    @pl.when(pl.program_id(1) == 0)
    def _():
        s_ref[...] = jnp.zeros_like(s_ref)

    for step in range(cb):
        c = cb - 1 - step if reverse else step
        rows = slice(c * CHUNK, (c + 1) * CHUNK)
        for p in range(N_PAIRS):
            lanes = slice(p * PAIR, (p + 1) * PAIR)
            s = s_ref[:, lanes]
            o_ref[rows, lanes] = _dot32(q_ref[rows, lanes], s) + ol_ref[rows, lanes]
            s_ref[:, lanes] = _dot32(g_ref[c, :, lanes], s) + h_ref[c, :, lanes]


def _rwkv_b(qeff, oloc, g_all, h_all, batch, seq, d):
    m = batch * seq
    w = RW_WIDTH
    cb = min(4, seq // CHUNK)
    tm = cb * CHUNK
    nb = seq // tm
    reverse = d == 1

    def blk(b, j):
        return b * nb + (nb - 1 - j if reverse else j)

    kern = functools.partial(_rwkv_b_kernel, reverse=reverse, cb=cb)
    return pl.pallas_call(
        kern,
        grid=(batch, nb),
        in_specs=[
            pl.BlockSpec((None, tm, w), lambda b, j: (d, blk(b, j), 0)),
            pl.BlockSpec((None, tm, w), lambda b, j: (d, blk(b, j), 0)),
            pl.BlockSpec((None, cb, PAIR, w), lambda b, j: (d, blk(b, j), 0, 0)),
            pl.BlockSpec((None, cb, PAIR, w), lambda b, j: (d, blk(b, j), 0, 0)),
        ],
        out_specs=pl.BlockSpec((tm, w), lambda b, j: (blk(b, j), 0)),
        out_shape=jax.ShapeDtypeStruct((m, w), F32),
        scratch_shapes=[pltpu.VMEM((PAIR, w), F32)],
        compiler_params=_cparams(("parallel", "arbitrary")),
        name="rwkv_recurrence_bwd" if reverse else "rwkv_recurrence_fwd",
    )(qeff, oloc, g_all, h_all)


def _mla_prep_kernel(z_ref, c_ref, s_ref, qn_ref, kvn_ref, qa_ref, qb_ref, kk_ref, kvv_ref,
                     q_out, k_out, v_out, *, scale):
    z = z_ref[...]
    cos = c_ref[...]
    sin = s_ref[...]
    qd = _rms(z[:, :Q_LORA], qn_ref[...]).astype(BF16)
    kvd = _rms(z[:, Q_LORA:Q_LORA + KV_LORA], kvn_ref[...]).astype(BF16)
    o = Q_LORA + KV_LORA
    kr = z[:, o:o + LANES] * cos + z[:, o + LANES:o + 2 * LANES] * sin
    qa = jnp.dot(qd, qa_ref[...], preferred_element_type=F32)
    qb = jnp.dot(qd, qb_ref[...], preferred_element_type=F32)
    kn = jnp.dot(kvd, kk_ref[...], preferred_element_type=F32)
    v_out[...] = lax.dot_general(kvv_ref[...], kvd, (((1,), (1,)), ((), ())),
                                 preferred_element_type=F32).astype(BF16)
    for h in range(MLA_HEADS):
        lanes = slice(h * LANES, (h + 1) * LANES)
        q_out[h] = ((qa[:, lanes] * cos + qb[:, lanes] * sin) * scale).astype(BF16)
        k_out[h] = (kn[:, lanes] + kr).astype(BF16)


def _mla_prep(z2d, cos, sin, batch, seq, q_norm, kv_norm, q_a, q_b, kv_k, kv_v):
    tm = min(512, seq)
    nt = seq // tm
    hw = MLA_HEADS * LANES
    scale = float((QK_NOPE + QK_ROPE) ** -0.5)
    full = lambda shape: pl.BlockSpec(shape, lambda b, i: (0,) * len(shape))
    col_blk = RW_COLS // MLA_IN
    assert col_blk * MLA_IN == RW_COLS
    return pl.pallas_call(
        functools.partial(_mla_prep_kernel, scale=scale),
        grid=(batch, nt),
        in_specs=[
            pl.BlockSpec((tm, MLA_IN), lambda b, i: (b * nt + i, col_blk)),
            pl.BlockSpec((tm, LANES), lambda b, i: (b * nt + i, 0)),
            pl.BlockSpec((tm, LANES), lambda b, i: (b * nt + i, 0)),
            full((1, Q_LORA)), full((1, KV_LORA)), full((Q_LORA, hw)), full((Q_LORA, hw)),
            full((KV_LORA, hw)), full((MLA_WIDTH, KV_LORA)),
        ],
        out_specs=[
            pl.BlockSpec((None, MLA_HEADS, tm, LANES), lambda b, i: (b, 0, i, 0)),
            pl.BlockSpec((None, MLA_HEADS, tm, LANES), lambda b, i: (b, 0, i, 0)),
            pl.BlockSpec((None, MLA_WIDTH, tm), lambda b, i: (b, 0, i)),
        ],
        out_shape=[
            jax.ShapeDtypeStruct((batch, MLA_HEADS, seq, LANES), BF16),
            jax.ShapeDtypeStruct((batch, MLA_HEADS, seq, LANES), BF16),
            jax.ShapeDtypeStruct((batch, MLA_WIDTH, seq), BF16),
        ],
        compiler_params=_cparams(("parallel", "parallel")),
        name="mla_prep",
    )(z2d, cos, sin, q_norm, kv_norm, q_a, q_b, kv_k, kv_v)


def _flash_kernel(q_ref, k_ref, vt_ref, o_ref, m_ref, l_ref, acc_ref):
    j = pl.program_id(3)

    @pl.when(j == 0)
    def _():
        m_ref[...] = jnp.full_like(m_ref, -jnp.inf)
        l_ref[...] = jnp.zeros_like(l_ref)
        acc_ref[...] = jnp.zeros_like(acc_ref)

    for h in range(2):
        rows = slice(h * V_HEAD, (h + 1) * V_HEAD)
        s = lax.dot_general(k_ref[h], q_ref[h], (((1,), (1,)), ((), ())),
                            preferred_element_type=F32)
        m_prev = m_ref[h:h + 1, :]
        m_new = jnp.maximum(m_prev, jnp.max(s, axis=0, keepdims=True))
        alpha = jnp.exp(m_prev - m_new)
        p = jnp.exp(s - m_new)
        l_ref[h:h + 1, :] = alpha * l_ref[h:h + 1, :] + jnp.sum(p, axis=0, keepdims=True)
        acc_ref[rows, :] = alpha * acc_ref[rows, :] + jnp.dot(
            vt_ref[rows, :], p.astype(BF16), preferred_element_type=F32)
        m_ref[h:h + 1, :] = m_new

    @pl.when(j == pl.num_programs(3) - 1)
    def _():
        inv = 1.0 / l_ref[...]
        o_t = jnp.concatenate([acc_ref[:V_HEAD, :] * inv[0:1, :], acc_ref[V_HEAD:, :] * inv[1:2, :]],
                              axis=0)
        o_ref[...] = o_t.T


def _flash(q, k, v_t):
    batch, heads, seq, _ = q.shape
    tq = min(FLASH_TQ, seq)
    tk = min(FLASH_TK, seq)
    return pl.pallas_call(
        _flash_kernel,
        grid=(batch, heads // 2, seq // tq, seq // tk),
        in_specs=[
            pl.BlockSpec((None, 2, tq, LANES), lambda b, p, i, j: (b, p, i, 0)),
            pl.BlockSpec((None, 2, tk, LANES), lambda b, p, i, j: (b, p, j, 0)),
            pl.BlockSpec((None, 2 * V_HEAD, tk), lambda b, p, i, j: (b, p, j)),
        ],
        out_specs=pl.BlockSpec((None, tq, LANES), lambda b, p, i, j: (b, i, p)),
        out_shape=jax.ShapeDtypeStruct((batch, seq, MLA_WIDTH), F32),
        scratch_shapes=[pltpu.VMEM((2, tq), F32), pltpu.VMEM((2, tq), F32),
                        pltpu.VMEM((2 * V_HEAD, tq), F32)],
        compiler_params=_cparams(("parallel", "parallel", "parallel", "arbitrary")),
        name="mla_flash",
    )(q, k, v_t)


def _out_kernel(of_ref, ob_ref, bonus_ref, gate_ref, ym_ref, x_ref, hsum_ref, lw_ref, lb_ref,
                on_ref, w_ref, o_ref):
    o = of_ref[...] + ob_ref[...]
    hsum = hsum_ref[...]
    inv_n = 1.0 / HEAD_DIM
    mean = _dot32(o, hsum) * inv_n
    d = o - mean
    var = _dot32(d * d, hsum) * inv_n
    y_rw = (d * lax.rsqrt(var + GN_EPS) * lw_ref[...] + lb_ref[...] + bonus_ref[...]) * gate_ref[...]
    y_mla = _rms(ym_ref[...], on_ref[...])
    w = RW_WIDTH
    o_ref[...] = (x_ref[...] + _dot(y_rw, w_ref[:w, :]) + _dot(y_mla, w_ref[w:, :]))


def _out_proj(o_f, o_b, bonus, gate, y_mla, x2d, hsum, lnx_w, lnx_b, o_norm, w_out):
    m, d = x2d.shape
    w = RW_WIDTH
    tm = min(256, m)
    row = lambda n: pl.BlockSpec((tm, n), lambda i: (i, 0))
    full = lambda shape: pl.BlockSpec(shape, lambda i: (0,) * len(shape))
    return pl.pallas_call(
        _out_kernel,
        grid=(m // tm,),
        in_specs=[row(w), row(w), row(w), row(w), row(MLA_WIDTH), row(d), full((w, w)),
                  full((1, w)), full((1, w)), full((1, MLA_WIDTH)), full((w + MLA_WIDTH, d))],
        out_specs=row(d),
        out_shape=jax.ShapeDtypeStruct((m, d), F32),
        compiler_params=_cparams(("parallel",)),
        name="out_proj",
    )(o_f, o_b, bonus, gate, y_mla, x2d, hsum, lnx_w, lnx_b, o_norm, w_out)


def _router_kernel(x_ref, g_ref, rt_ref, a_ref):
    xn = _rms(x_ref[...], g_ref[...])
    logits = _dot32_nt(rt_ref[...], xn)
    mx = jnp.max(logits, axis=0, keepdims=True)
    e = jnp.exp(logits - mx)
    a_ref[...] = e / jnp.sum(e, axis=0, keepdims=True)


def _router(x3d, g, router_t):
    batch, seq, d = x3d.shape
    e = router_t.shape[0]
    tm = min(512, seq)
    return pl.pallas_call(
        _router_kernel,
        grid=(batch, seq // tm),
        in_specs=[pl.BlockSpec((None, tm, d), lambda b, i: (b, i, 0)),
                  pl.BlockSpec((1, d), lambda b, i: (0, 0)),
                  pl.BlockSpec((e, d), lambda b, i: (0, 0))],
        out_specs=pl.BlockSpec((None, e, tm), lambda b, i: (b, 0, i)),
        out_shape=jax.ShapeDtypeStruct((batch, e, seq), F32),
        compiler_params=_cparams(("parallel", "parallel")),
        name="moe_router",
    )(x3d, g, router_t)


def _cumsum_lanes(src_ref, dst_ref, upper):
    rows, t = src_ref.shape

    def body(j, carry):
        cols = pl.ds(pl.multiple_of(j * LANES, LANES), LANES)
        cs = jnp.dot(src_ref[:, cols].astype(BF16), upper, preferred_element_type=F32) + carry
        dst_ref[:, cols] = cs
        return cs[:, LANES - 1:LANES]

    lax.fori_loop(0, t // LANES, body, jnp.zeros((rows, 1), F32))


def _select_kernel(a_ref, res_ref, m_ref, c_ref, rank_ref, *, cap, ct, tt):
    aff = a_ref[...]
    n_e, t = aff.shape
    bits = pltpu.bitcast(aff, I32)

    def search(i, cur):
        cand = cur | jnp.left_shift(jnp.int32(1), 30 - i)
        cnt = jnp.sum(jnp.where(bits >= cand, 1, 0), axis=1, keepdims=True)
        return jnp.where(cnt >= cap, cand, cur)

    thr = lax.fori_loop(0, 31, search, jnp.zeros((n_e, 1), I32))
    gt = bits > thr
    eq = bits == thr
    need = cap - jnp.sum(jnp.where(gt, 1, 0), axis=1, keepdims=True)

    ri = lax.broadcasted_iota(I32, (LANES, LANES), 0)
    ci = lax.broadcasted_iota(I32, (LANES, LANES), 1)
    upper = jnp.where(ri <= ci, 1.0, 0.0).astype(BF16)

    m_ref[...] = jnp.where(eq, 1.0, 0.0)
    _cumsum_lanes(m_ref, c_ref, upper)
    sel = gt | (eq & (c_ref[...] <= need.astype(F32)))
    m_ref[...] = jnp.where(sel, 1.0, 0.0)
    _cumsum_lanes(m_ref, c_ref, upper)
    rank_ref[...] = jnp.where(sel, c_ref[...] - 1.0, -1.0)

    c_iota = lax.broadcasted_iota(I32, (ct, 1), 0).astype(F32)
    vrow = lax.broadcasted_iota(I32, (8, 1), 0)
    t_iota = lax.broadcasted_iota(I32, (1, tt), 1)

    def per_expert(e, _):
        def per_ctile(ci_, _):
            c0 = pl.multiple_of(ci_ * ct, ct)

            def per_ttile(ti_, acc):
                t0 = pl.multiple_of(ti_ * tt, tt)
                rk = rank_ref[pl.ds(e, 1), pl.ds(t0, tt)]
                av = a_ref[pl.ds(e, 1), pl.ds(t0, tt)]
                onehot = jnp.where(rk - c0.astype(F32) == c_iota, 1.0, 0.0).astype(BF16)
                tpos = t_iota + t0
                t_hi = (tpos // LANES).astype(F32)
                t_lo = (tpos % LANES).astype(F32)
                a_hi = av.astype(BF16).astype(F32)
                a_mid = (av - a_hi).astype(BF16).astype(F32)
                a_lo = av - a_hi - a_mid
                vals = jnp.where(vrow == 0, t_hi, jnp.where(vrow == 1, t_lo, jnp.where(
                    vrow == 2, a_hi, jnp.where(vrow == 3, a_mid, jnp.where(vrow == 4, a_lo, 0.0)))))
                return acc + lax.dot_general(vals.astype(BF16), onehot, (((1,), (1,)), ((), ())),
                                             preferred_element_type=F32)

            acc = lax.fori_loop(0, t // tt, per_ttile, jnp.zeros((8, ct), F32))
            res_ref[e, :, pl.ds(c0, ct)] = acc
            return 0

        lax.fori_loop(0, cap // ct, per_ctile, 0)
        return 0

    lax.fori_loop(0, n_e, per_expert, 0)


def _select(aff_t, cap):
    batch, e, seq = aff_t.shape
    ct = min(256, cap)
    tt = min(512, seq)
    kern = functools.partial(_select_kernel, cap=cap, ct=ct, tt=tt)
    res = pl.pallas_call(
        kern,
        grid=(batch,),
        in_specs=[pl.BlockSpec((None, e, seq), lambda b: (b, 0, 0))],
        out_specs=pl.BlockSpec((None, e, 8, cap), lambda b: (b, 0, 0, 0)),
        out_shape=jax.ShapeDtypeStruct((batch, e, 8, cap), F32),
        scratch_shapes=[pltpu.VMEM((e, seq), F32)] * 3,
        compiler_params=_cparams(("parallel",)),
        name="moe_select",
    )(aff_t)
    idx = (res[:, :, 0, :] * LANES + res[:, :, 1, :]).astype(I32)
    gate = res[:, :, 2, :] + res[:, :, 3, :] + res[:, :, 4, :]
    return idx, gate


def _gather_kernel(idx_ref, x_ref, o_ref, *, cap, n_e):
    base = (pl.program_id(0) * n_e + pl.program_id(2)) * cap

    def body(c, carry):
        row = idx_ref[base + c]
        o_ref[pl.ds(c, 1), :] = x_ref[pl.ds(row, 1), :]
        return carry

    lax.fori_loop(0, cap, body, 0, unroll=8)


def _gather(idx_flat, x3d, n_e, cap):
    batch, seq, d = x3d.shape
    dt = 512
    return pl.pallas_call(
        functools.partial(_gather_kernel, cap=cap, n_e=n_e),
        grid_spec=pltpu.PrefetchScalarGridSpec(
            num_scalar_prefetch=1,
            grid=(batch, d // dt, n_e),
            in_specs=[pl.BlockSpec((None, seq, dt), lambda b, j, e, idx: (b, 0, j))],
            out_specs=pl.BlockSpec((None, None, cap, dt), lambda b, j, e, idx: (b, e, 0, j)),
        ),
        out_shape=jax.ShapeDtypeStruct((batch, n_e, cap, d), F32),
        compiler_params=_cparams(("parallel", "parallel", "arbitrary")),
        name="moe_gather",
    )(idx_flat, x3d)


def _ffn_kernel(x_ref, gate_ref, g_ref, wg_ref, wu_ref, wd_ref, o_ref):
    xn = _rms(x_ref[...], g_ref[...]).astype(BF16)
    h1 = jnp.dot(xn, wg_ref[...], preferred_element_type=F32)
    h2 = jnp.dot(xn, wu_ref[...], preferred_element_type=F32)
    hid = (h1 * _sigmoid(h1) * h2).astype(BF16)
    o_ref[...] = jnp.dot(hid, wd_ref[...], preferred_element_type=F32) * gate_ref[...]


def _expert_ffn(xe, gate_col, g, w_gate, w_up, w_down):
    batch, n_e, cap, d = xe.shape
    f = w_gate.shape[2]
    tc = min(256, cap)
    return pl.pallas_call(
        _ffn_kernel,
        grid=(n_e, batch, cap // tc),
        in_specs=[
            pl.BlockSpec((None, None, tc, d), lambda e, b, c: (b, e, c, 0)),
            pl.BlockSpec((None, None, tc, 1), lambda e, b, c: (b, e, c, 0)),
            pl.BlockSpec((1, d), lambda e, b, c: (0, 0)),
            pl.BlockSpec((None, d, f), lambda e, b, c: (e, 0, 0)),
            pl.BlockSpec((None, d, f), lambda e, b, c: (e, 0, 0)),
            pl.BlockSpec((None, f, d), lambda e, b, c: (e, 0, 0)),
        ],
        out_specs=pl.BlockSpec((None, None, tc, d), lambda e, b, c: (b, e, c, 0)),
        out_shape=jax.ShapeDtypeStruct((batch, n_e, cap, d), F32),
        compiler_params=_cparams(("parallel", "parallel", "parallel")),
        name="moe_ffn",
    )(xe, gate_col, g, w_gate, w_up, w_down)


def _scatter_kernel(idx_ref, x_ref, y_ref, o_ref, *, cap, n_e):
    e = pl.program_id(2)
    base = (pl.program_id(0) * n_e + e) * cap

    @pl.when(e == 0)
    def _():
        o_ref[...] = x_ref[...]

    def body(c, carry):
        row = idx_ref[base + c]
        o_ref[pl.ds(row, 1), :] = o_ref[pl.ds(row, 1), :] + y_ref[pl.ds(c, 1), :]
        return carry

    lax.fori_loop(0, cap, body, 0, unroll=8)


def _scatter_add(idx_flat, x3d, ye):
    batch, seq, d = x3d.shape
    _, n_e, cap, _ = ye.shape
    dt = 256
    return pl.pallas_call(
        functools.partial(_scatter_kernel, cap=cap, n_e=n_e),
        grid_spec=pltpu.PrefetchScalarGridSpec(
            num_scalar_prefetch=1,
            grid=(batch, d // dt, n_e),
            in_specs=[pl.BlockSpec((None, seq, dt), lambda b, j, e, idx: (b, 0, j)),
                      pl.BlockSpec((None, None, cap, dt), lambda b, j, e, idx: (b, e, 0, j))],
            out_specs=pl.BlockSpec((None, seq, dt), lambda b, j, e, idx: (b, 0, j)),
        ),
        out_shape=jax.ShapeDtypeStruct((batch, seq, d), F32),
        compiler_params=_cparams(("parallel", "parallel", "arbitrary")),
        name="moe_scatter_add",
    )(idx_flat, x3d, ye)


def _ple_kernel(x_ref, p_ref, g_ref, wp_ref, wg_ref, fg_ref, o_ref, *, final):
    x = x_ref[...]
    gate = _sigmoid(_dot(_rms(x, g_ref[...]), wg_ref[...]))
    out = x + _dot(p_ref[...], wp_ref[...]) * gate
    if final:
        out = _rms(out, fg_ref[...])
    o_ref[...] = out


def _ple(x2d, p2d, g, w_proj, w_gate, final_g, final):
    m, d = x2d.shape
    dp = p2d.shape[1]
    tm = min(512, m)
    full = lambda shape: pl.BlockSpec(shape, lambda i: (0,) * len(shape))
    return pl.pallas_call(
        functools.partial(_ple_kernel, final=final),
        grid=(m // tm,),
        in_specs=[pl.BlockSpec((tm, d), lambda i: (i, 0)), pl.BlockSpec((tm, dp), lambda i: (i, 0)),
                  full((1, d)), full((dp, d)), full((d, d)), full((1, d))],
        out_specs=pl.BlockSpec((tm, d), lambda i: (i, 0)),
        out_shape=jax.ShapeDtypeStruct((m, d), F32),
        compiler_params=_cparams(("parallel",)),
        name="ple_final" if final else "ple",
    )(x2d, p2d, g, w_proj, w_gate, final_g)


def _rot_cols(w):
    half = QK_ROPE // 2
    return jnp.concatenate([-w[..., half:], w[..., :half]], axis=-1)


def _pad_head(nope, rope):
    lead = (nope if nope is not None else rope).shape[:-1]
    n = nope if nope is not None else jnp.zeros(lead + (QK_NOPE,), F32)
    r = rope if rope is not None else jnp.zeros(lead + (QK_ROPE,), F32)
    return jnp.concatenate([n, r, jnp.zeros(lead + (LANES - QK_NOPE - QK_ROPE,), F32)], axis=-1)


def _block_rows(w_pair):
    z = jnp.zeros_like(w_pair[0])
    return jnp.concatenate([jnp.concatenate([w_pair[0], z], axis=1),
                            jnp.concatenate([z, w_pair[1]], axis=1)], axis=0)


def kernel(x, p, positions, attn_norm, w_in, rw_mu, rw_w0, rw_w_up, rw_a0, rw_a_up, rw_g_up, rw_k_k,
           rw_k_a, rw_r_k, rw_lnx_w, rw_lnx_b, mla_q_norm, mla_q_up, mla_kv_norm, mla_kv_up,
           mla_o_norm, w_out, ffn_norm, router, exp_w_gate, exp_w_up, exp_w_down, ple_norm,
           ple_proj, ple_gate, final_norm):
    batch, seq, d = x.shape
    depth = w_in.shape[0]
    m = batch * seq
    cap = EC_FACTOR * seq // N_EXPERTS
    w = RW_WIDTH

    cos, sin = _rope_tables(positions)
    hsum = (jnp.arange(w)[:, None] // HEAD_DIM == jnp.arange(w)[None, :] // HEAD_DIM).astype(F32)
    x2d = x.reshape(m, d)

    for i in range(depth):
        w_mla = w_in[i][:, RW_COLS:]
        w_kr = w_mla[:, Q_LORA + KV_LORA:]
        w_ext = jnp.concatenate([w_in[i][:, :RW_COLS], w_mla[:, :Q_LORA + KV_LORA],
                                 _pad_head(None, w_kr), _pad_head(None, _rot_cols(w_kr))],
                                axis=1).astype(BF16)
        q_up = mla_q_up[i].reshape(Q_LORA, MLA_HEADS, QK_NOPE + QK_ROPE)
        q_a = _pad_head(q_up[..., :QK_NOPE], q_up[..., QK_NOPE:]).reshape(Q_LORA, -1).astype(BF16)
        q_b = _pad_head(None, _rot_cols(q_up[..., QK_NOPE:])).reshape(Q_LORA, -1).astype(BF16)
        kv_up = mla_kv_up[i].reshape(KV_LORA, MLA_HEADS, QK_NOPE + V_HEAD)
        kv_k = _pad_head(kv_up[..., :QK_NOPE], None).reshape(KV_LORA, -1).astype(BF16)
        kv_v = kv_up[..., QK_NOPE:].reshape(KV_LORA, MLA_WIDTH).T.astype(BF16)

        z2d = _in_proj(x2d, attn_norm[i][None, :], w_ext)
        qeff, oloc, g_all, h_all, bonus, gate = _rwkv_a(
            z2d, seq, rw_mu[i][None, :], rw_w0[i].reshape(1, 2 * w), _block_rows(rw_w_up[i]),
            rw_a0[i].reshape(1, 2 * w), _block_rows(rw_a_up[i]), rw_g_up[i], rw_k_k[i][None, :],
            rw_k_a[i][None, :], rw_r_k[i].reshape(1, w), hsum)
        o_f = _rwkv_b(qeff, oloc, g_all, h_all, batch, seq, 0)
        o_b = _rwkv_b(qeff, oloc, g_all, h_all, batch, seq, 1)
        q, k, v = _mla_prep(z2d, cos, sin, batch, seq, mla_q_norm[i][None, :],
                            mla_kv_norm[i][None, :], q_a, q_b, kv_k, kv_v)
        y_mla = _flash(q, k, v).reshape(m, MLA_WIDTH)
        x2d = _out_proj(o_f, o_b, bonus, gate, y_mla, x2d, hsum, rw_lnx_w[i][None, :],
                        rw_lnx_b[i][None, :], mla_o_norm[i][None, :], w_out[i].astype(BF16))

        x3d = x2d.reshape(batch, seq, d)
        aff_t = _router(x3d, ffn_norm[i][None, :], router[i].T)
        idx, gates = _select(aff_t, cap)
        idx_flat = idx.reshape(-1)
        xe = _gather(idx_flat, x3d, N_EXPERTS, cap)
        ye = _expert_ffn(xe, gates.reshape(batch, N_EXPERTS, cap, 1), ffn_norm[i][None, :],
                         exp_w_gate[i].astype(BF16), exp_w_up[i].astype(BF16),
                         exp_w_down[i].astype(BF16))
        x2d = _scatter_add(idx_flat, x3d, ye).reshape(m, d)

        x2d = _ple(x2d, p[i].reshape(m, -1), ple_norm[i][None, :], ple_proj[i].astype(BF16),
                   ple_gate[i].astype(BF16), final_norm[None, :], final=(i == depth - 1))

    return x2d.reshape(batch, seq, d)
```

```python
import functools
import math

import jax
import jax.numpy as jnp
from jax import lax
from jax.experimental import pallas as pl
from jax.experimental.pallas import tpu as pltpu

F32 = jnp.float32
BF16 = jnp.bfloat16
I32 = jnp.int32
HIGHEST = lax.Precision.HIGHEST

RW_HEADS = 8
HEAD_DIM = 64
RW_WIDTH = RW_HEADS * HEAD_DIM
RW_COLS = 3 * RW_WIDTH + 2 * 64 + 2 * 64 + 128
MLA_HEADS = 8
QK_NOPE = 64
QK_ROPE = 32
V_HEAD = 64
Q_LORA = 256
KV_LORA = 128
MLA_WIDTH = MLA_HEADS * V_HEAD
MLA_IN = Q_LORA + KV_LORA + 2 * 128
ROPE_THETA = 10000.0
N_EXPERTS = 16
EC_FACTOR = 2
NORM_EPS = 1e-6
GN_EPS = 64e-5

LANES = 128
SUBLANES = 8
CHUNK = 64
PAIR = 2 * HEAD_DIM
N_PAIRS = RW_WIDTH // PAIR
VMEM_LIMIT = 56 * 1024 * 1024
FLASH_TQ = 1024
FLASH_TK = 2048
FLASH_SUB = 256
ONES_ROWS = 16


def _cparams(sem):
    return pltpu.CompilerParams(dimension_semantics=sem, vmem_limit_bytes=VMEM_LIMIT)


def _rms(x, g):
    return x * lax.rsqrt(jnp.mean(x * x, axis=-1, keepdims=True) + NORM_EPS) * g


def _sigmoid(x):
    return 1.0 / (1.0 + jnp.exp(-x))


def _dot(a, b):
    return jnp.dot(a.astype(BF16), b.astype(BF16), preferred_element_type=F32)


def _dot32(a, b):
    return jnp.dot(a, b, preferred_element_type=F32, precision=HIGHEST)


def _split_bf16(x, parts):
    out = []
    rest = x
    for _ in range(parts):
        hi = rest.astype(BF16)
        out.append(hi)
        rest = rest - hi.astype(F32)
    return out


def _dot_lsplit(a, b_exact, parts):
    acc = None
    for term in _split_bf16(a, parts):
        d = jnp.dot(term, b_exact, preferred_element_type=F32)
        acc = d if acc is None else acc + d
    return acc


def _dot_rsplit(a_exact, b, parts):
    acc = None
    for term in _split_bf16(b, parts):
        d = jnp.dot(a_exact, term, preferred_element_type=F32)
        acc = d if acc is None else acc + d
    return acc


def _dot_3pass(a, b_hi, b_lo):
    a_hi, a_lo = _split_bf16(a, 2)
    return (jnp.dot(a_hi, b_hi, preferred_element_type=F32)
            + jnp.dot(a_lo, b_hi, preferred_element_type=F32)
            + jnp.dot(a_hi, b_lo, preferred_element_type=F32))


def _dot32_nt(a, b):
    return lax.dot_general(a, b, (((1,), (1,)), ((), ())), preferred_element_type=F32,
                           precision=HIGHEST)


def _dot32_tn(a, b):
    return lax.dot_general(a, b, (((0,), (0,)), ((), ())), preferred_element_type=F32,
                           precision=HIGHEST)


def _dot_tn(a, b):
    return lax.dot_general(a.astype(BF16), b.astype(BF16), (((0,), (0,)), ((), ())),
                           preferred_element_type=F32)


def _dot_nt(a, b):
    return lax.dot_general(a.astype(BF16), b.astype(BF16), (((1,), (1,)), ((), ())),
                           preferred_element_type=F32)


def _rope_kernel(pos_ref, inv_ref, c_ref, s_ref):
    ang = pos_ref[...] * inv_ref[...]
    c_ref[...] = jnp.cos(ang)
    s_ref[...] = jnp.sin(ang)


def _rope_tables(positions):
    b, t = positions.shape
    m = b * t
    inv = ROPE_THETA ** (-jnp.arange(0, QK_ROPE, 2, dtype=F32) / QK_ROPE)
    inv_row = jnp.concatenate([jnp.zeros((QK_NOPE,), F32), inv, inv,
                               jnp.zeros((LANES - QK_NOPE - QK_ROPE,), F32)])[None, :]
    posf = jnp.broadcast_to(positions.astype(F32).reshape(m, 1), (m, LANES))
    tm = min(1024, m)
    return pl.pallas_call(
        _rope_kernel,
        grid=(m // tm,),
        in_specs=[pl.BlockSpec((tm, LANES), lambda i: (i, 0)),
                  pl.BlockSpec((1, LANES), lambda i: (0, 0))],
        out_specs=[pl.BlockSpec((tm, LANES), lambda i: (i, 0))] * 2,
        out_shape=[jax.ShapeDtypeStruct((m, LANES), F32)] * 2,
        compiler_params=_cparams(("parallel",)),
        name="rope_tables",
    )(posf, inv_row)


def _in_kernel(x_ref, g_ref, w_ref, o_ref):
    h = _rms(x_ref[...], g_ref[...])
    o_ref[...] = jnp.dot(h.astype(BF16), w_ref[...], preferred_element_type=F32)


def _in_proj(x2d, g, w_ext):
    m, d = x2d.shape
    n = w_ext.shape[1]
    tm = min(512, m)
    return pl.pallas_call(
        _in_kernel,
        grid=(m // tm,),
        in_specs=[pl.BlockSpec((tm, d), lambda i: (i, 0)),
                  pl.BlockSpec((1, d), lambda i: (0, 0)),
                  pl.BlockSpec((d, n), lambda i: (0, 0))],
        out_specs=pl.BlockSpec((tm, n), lambda i: (i, 0)),
        out_shape=jax.ShapeDtypeStruct((m, n), F32),
        compiler_params=_cparams(("parallel",)),
        name="in_proj",
    )(x2d, g, w_ext)


def _pair_masks():
    i = lax.broadcasted_iota(I32, (PAIR, PAIR), 0)
    j = lax.broadcasted_iota(I32, (PAIR, PAIR), 1)
    same = (i // CHUNK) == (j // CHUNK)
    li = i % CHUNK
    lj = j % CHUNK
    return same, li, lj, i == j


def _chunk_pair(a_t, b_t, k_t, r_t, v, b_h, k_h, g_last, consts):
    n = len(a_t)
    idx = range(n)
    strict = [c[0] for c in consts]
    incl = [c[1] for c in consts]
    levels = [c[2] for c in consts]
    eye, m0, m1 = consts[0][3:]

    def stack(x):
        return jnp.concatenate([jnp.where(m0, x, jnp.zeros_like(x)),
                                jnp.where(m1, x, jnp.zeros_like(x))], axis=0)

    a2, b2, k2, v2, bh2, kh2, r2 = ([stack(x) for x in xs] for xs in (a_t, b_t, k_t, v, b_h, k_h, r_t))
    prod = [_dot_nt(jnp.concatenate([a2[i], r2[i].astype(BF16)], axis=0),
                    jnp.concatenate([b2[i], k2[i]], axis=0)) for i in idx]
    n_mat = [jnp.where(strict[i], prod[i][:PAIR, :PAIR], 0.0) for i in idx]
    m_ak = [jnp.where(strict[i], prod[i][:PAIR, PAIR:], 0.0).astype(BF16) for i in idx]
    m_rb = [jnp.where(incl[i], prod[i][PAIR:, :PAIR], 0.0).astype(BF16) for i in idx]
    m_rk = [jnp.where(incl[i], prod[i][PAIR:, PAIR:], 0.0).astype(BF16) for i in idx]
    mv = [_dot(jnp.concatenate([m_ak[i], m_rk[i]], axis=0), v2[i]) for i in idx]

    x = [jnp.where(eye, 1.0, 0.0) - jnp.where(levels[i][0], n_mat[i], 0.0) for i in idx]
    n_bf = [n_mat[i].astype(BF16) for i in idx]
    zero = jnp.zeros((PAIR, PAIR), BF16)
    for lv in range(1, len(levels[0])):
        cx = [_dot(jnp.where(levels[i][lv], n_bf[i], zero), x[i]) for i in idx]
        x = [x[i] - _dot(x[i], cx[i]) for i in idx]

    tw = [_dot(x[i], jnp.concatenate([a2[i], mv[i][:PAIR].astype(BF16)], axis=1))
          for i in idx]
    tw_bf = [t.astype(BF16) for t in tw]
    qo = [jnp.concatenate([r2[i], mv[i][PAIR:]], axis=1) - _dot(m_rb[i], tw_bf[i])
          for i in idx]
    bt = [_dot_tn(bh2[i], tw_bf[i]) for i in idx]
    kv = [_dot_tn(kh2[i], v2[i]) for i in idx]
    out = []
    for i in idx:
        g_mat = jnp.where(eye, g_last[i], 0.0) - bt[i][:, :PAIR]
        h_mat = kv[i] - bt[i][:, PAIR:]
        q = qo[i][:CHUNK] + qo[i][CHUNK:]
        out.append((q[:, :PAIR], q[:, PAIR:], g_mat, h_mat))
    return out


def _rwkv_a_kernel(z_ref, zp_ref, zn_ref, mu_ref, w0_ref, wuph_ref, wupl_ref, a0_ref, auph_ref,
                   aupl_ref, gup_ref, kk_ref, ka_ref, rk_ref, hsum_ref, trif_ref, trib_ref,
                   q_out, ol_out, g_out, h_out, bonus_out, gate_out,
                   at_s, bt_s, kt_s, rt_s, bh_s, kh_s, v_s, gl_s, *, tm, seq):
    i = pl.program_id(0)
    z = z_ref[...]
    row = lax.broadcasted_iota(I32, (tm, 1), 0)
    has_prev = (i * tm) % seq != 0
    has_next = ((i + 1) * tm) % seq != 0
    prev_row = jnp.where(has_prev, zp_ref[7:8, :], 0.0)
    next_row = jnp.where(has_next, zn_ref[0:1, :], 0.0)
    z_dn = jnp.where(row == 0, prev_row, pltpu.roll(z, 1, axis=0))
    z_up = jnp.where(row == tm - 1, next_row, pltpu.roll(z, tm - 1, axis=0))
    zs = z + mu_ref[...] * (0.5 * (z_dn + z_up) - z)

    w = RW_WIDTH
    r = zs[:, :w]
    k = zs[:, w:2 * w]
    v = zs[:, 2 * w:3 * w]
    wd = zs[:, 3 * w:3 * w + 128]
    ad = zs[:, 3 * w + 128:3 * w + 256]
    gd = zs[:, 3 * w + 256:3 * w + 384]

    hsum = hsum_ref[...]
    w_logit = w0_ref[...] + _dot_3pass(jnp.tanh(wd), wuph_ref[...], wupl_ref[...])
    lw = -_sigmoid(w_logit) * jnp.exp(jnp.float32(-0.5))
    a = _sigmoid(a0_ref[...] + _dot_3pass(ad, auph_ref[...], aupl_ref[...]))
    gate_out[...] = _dot(_sigmoid(gd), gup_ref[...])
    kkr = k * kk_ref[...]
    kkn = kkr / jnp.maximum(jnp.sqrt(_dot_lsplit(kkr * kkr, hsum, 2)), 1e-12)
    kd = [k * (1.0 + (a[:, d * w:(d + 1) * w] - 1.0) * ka_ref[...]) for d in range(2)]
    bonus_out[...] = _dot_lsplit(r * (0.5 * (kd[0] + kd[1])) * rk_ref[...], hsum, 2) * v
    v_s[...] = v.astype(BF16)

    for d, tri_ref in enumerate((trif_ref, trib_ref)):
        cols = slice(d * w, (d + 1) * w)
        lw_d = lw[:, cols]
        cum = _dot_rsplit(tri_ref[...], lw_d, 3)
        ends = [c * CHUNK if d == 1 else (c + 1) * CHUNK - 1 for c in range(tm // CHUNK)]
        tot = jnp.concatenate([jnp.broadcast_to(cum[e:e + 1, :], (CHUNK, w)) for e in ends], axis=0)
        g_inv = jnp.exp(-cum)
        g_end = jnp.exp(tot - cum)
        b = kkn * a[:, cols]
        at_s[d] = (kkn * jnp.exp(cum - lw_d)).astype(BF16)
        bt_s[d] = (b * g_inv).astype(BF16)
        kt_s[d] = (kd[d] * g_inv).astype(BF16)
        rt_s[d] = r * jnp.exp(cum)
        bh_s[d] = (b * g_end).astype(BF16)
        kh_s[d] = (kd[d] * g_end).astype(BF16)
        gl_s[d] = jnp.exp(tot)

    same, li, lj, eye = _pair_masks()
    lane = lax.broadcasted_iota(I32, (1, PAIR), 1)
    m0 = lane < HEAD_DIM
    m1 = lane >= HEAD_DIM
    consts = []
    for reverse in (False, True):
        before = (lj > li) if reverse else (lj < li)
        strict = same & before
        incl = same & (before | (li == lj))
        levels = []
        s = 1
        while s < CHUNK:
            blk = same & ((li // (2 * s)) == (lj // (2 * s)))
            hi_row = (li // s) % 2 == 1
            hi_col = (lj // s) % 2 == 1
            levels.append(blk & ((~hi_row & hi_col) if reverse else (hi_row & ~hi_col)))
            s *= 2
        consts.append((strict, incl, levels, eye, m0, m1))

    def chunk_body(c, carry):
        r0 = pl.multiple_of(c * CHUNK, CHUNK)
        rows = pl.ds(r0, CHUNK)
        inst = [(d, slice(p * PAIR, (p + 1) * PAIR)) for d in range(2) for p in range(N_PAIRS)]
        outs = _chunk_pair(
            [at_s[d, rows, ln] for d, ln in inst], [bt_s[d, rows, ln] for d, ln in inst],
            [kt_s[d, rows, ln] for d, ln in inst], [rt_s[d, rows, ln] for d, ln in inst],
            [v_s[rows, ln] for d, ln in inst], [bh_s[d, rows, ln] for d, ln in inst],
            [kh_s[d, rows, ln] for d, ln in inst], [gl_s[d, pl.ds(r0, 1), ln] for d, ln in inst],
            [consts[d] for d, ln in inst])
        for (d, ln), (qe, ol, g_mat, h_mat) in zip(inst, outs):
            q_out[d, rows, ln] = qe.astype(BF16)
            ol_out[d, rows, ln] = ol
            g_out[d, c, :, ln] = g_mat.astype(BF16)
            h_out[d, c, :, ln] = h_mat
        return carry

    lax.fori_loop(0, tm // CHUNK, chunk_body, 0)


def _rwkv_a(z2d, seq, mu, w0, wup, a0, aup, gup, k_k, k_a, r_k, hsum):
    m = z2d.shape[0]
    tm = min(256, seq)
    nc = tm // CHUNK
    w = RW_WIDTH
    full = lambda shape: pl.BlockSpec(shape, lambda i: (0,) * len(shape))
    last8 = m // 8 - 1
    ti = jnp.arange(tm)[:, None]
    tj = jnp.arange(tm)[None, :]
    same_chunk = (ti // CHUNK) == (tj // CHUNK)
    tri_f = (same_chunk & (tj <= ti)).astype(BF16)
    tri_b = (same_chunk & (tj >= ti)).astype(BF16)
    wup_hi, wup_lo = _hi_lo(wup)
    aup_hi, aup_lo = _hi_lo(aup)
    kern = functools.partial(_rwkv_a_kernel, tm=tm, seq=seq)
    return pl.pallas_call(
        kern,
        grid=(m // tm,),
        in_specs=[
            pl.BlockSpec((tm, RW_COLS), lambda i: (i, 0)),
            pl.BlockSpec((8, RW_COLS), lambda i: (jnp.maximum(i * (tm // 8) - 1, 0), 0)),
            pl.BlockSpec((8, RW_COLS), lambda i: (jnp.minimum((i + 1) * (tm // 8), last8), 0)),
            full((1, RW_COLS)), full((1, 2 * w)), full((128, 2 * w)), full((128, 2 * w)),
            full((1, 2 * w)), full((128, 2 * w)), full((128, 2 * w)), full((128, w)),
            full((1, w)), full((1, w)), full((1, w)), full((w, w)), full((tm, tm)), full((tm, tm)),
        ],
        out_specs=[
            pl.BlockSpec((2, tm, w), lambda i: (0, i, 0)),
            pl.BlockSpec((2, tm, w), lambda i: (0, i, 0)),
            pl.BlockSpec((2, nc, PAIR, w), lambda i: (0, i, 0, 0)),
            pl.BlockSpec((2, nc, PAIR, w), lambda i: (0, i, 0, 0)),
            pl.BlockSpec((tm, w), lambda i: (i, 0)),
            pl.BlockSpec((tm, w), lambda i: (i, 0)),
        ],
        out_shape=[
            jax.ShapeDtypeStruct((2, m, w), BF16),
            jax.ShapeDtypeStruct((2, m, w), F32),
            jax.ShapeDtypeStruct((2, m // CHUNK, PAIR, w), BF16),
            jax.ShapeDtypeStruct((2, m // CHUNK, PAIR, w), F32),
            jax.ShapeDtypeStruct((m, w), F32),
            jax.ShapeDtypeStruct((m, w), F32),
        ],
        scratch_shapes=[pltpu.VMEM((2, tm, w), BF16)] * 3 + [pltpu.VMEM((2, tm, w), F32)]
        + [pltpu.VMEM((2, tm, w), BF16)] * 2 + [pltpu.VMEM((tm, w), BF16), pltpu.VMEM((2, tm, w), F32)],
        compiler_params=_cparams(("parallel",)),
        name="rwkv_chunk_local",
    )(z2d, z2d, z2d, mu, w0, wup_hi, wup_lo, a0, aup_hi, aup_lo, gup.astype(BF16), k_k, k_a, r_k,
      hsum, tri_f, tri_b)


def _rwkv_b_kernel(qf_ref, olf_ref, gf_ref, hf_ref, qb_ref, olb_ref, gb_ref, hb_ref,
                   of_ref, ob_ref, s_ref, *, cb):
    @pl.when(pl.program_id(1) == 0)
    def _():
        s_ref[...] = jnp.zeros_like(s_ref)

    for step in range(cb):
        inst = []
        for d, refs in enumerate(((qf_ref, olf_ref, gf_ref, hf_ref, of_ref),
                                  (qb_ref, olb_ref, gb_ref, hb_ref, ob_ref))):
            c = cb - 1 - step if d == 1 else step
            for p in range(N_PAIRS):
                inst.append((d, c, slice(c * CHUNK, (c + 1) * CHUNK),
                             slice(p * PAIR, (p + 1) * PAIR)) + refs)
        s_bf = [s_ref[d, :, ln].astype(BF16) for d, c, rows, ln, *_ in inst]
        s_new = [jnp.dot(g_ref[c, :, ln], sb, preferred_element_type=F32) + h_ref[c, :, ln]
                 for (d, c, rows, ln, q_ref, ol_ref, g_ref, h_ref, o_ref), sb in zip(inst, s_bf)]
        o_val = [jnp.dot(q_ref[rows, ln], sb, preferred_element_type=F32) + ol_ref[rows, ln]
                 for (d, c, rows, ln, q_ref, ol_ref, g_ref, h_ref, o_ref), sb in zip(inst, s_bf)]
        for (d, c, rows, ln, q_ref, ol_ref, g_ref, h_ref, o_ref), sn, ov in zip(inst, s_new, o_val):
            s_ref[d, :, ln] = sn
            o_ref[rows, ln] = ov


def _rwkv_b(qeff, oloc, g_all, h_all, batch, seq):
    m = batch * seq
    w = RW_WIDTH
    cb = min(4, seq // CHUNK)
    tm = cb * CHUNK
    nb = seq // tm

    def fwd(b, j):
        return b * nb + j

    def bwd(b, j):
        return b * nb + nb - 1 - j

    def specs(d, blk):
        return [
            pl.BlockSpec((None, tm, w), lambda b, j: (d, blk(b, j), 0)),
            pl.BlockSpec((None, tm, w), lambda b, j: (d, blk(b, j), 0)),
            pl.BlockSpec((None, cb, PAIR, w), lambda b, j: (d, blk(b, j), 0, 0)),
            pl.BlockSpec((None, cb, PAIR, w), lambda b, j: (d, blk(b, j), 0, 0)),
        ]

    return pl.pallas_call(
        functools.partial(_rwkv_b_kernel, cb=cb),
        grid=(batch, nb),
        in_specs=specs(0, fwd) + specs(1, bwd),
        out_specs=[pl.BlockSpec((tm, w), lambda b, j: (fwd(b, j), 0)),
                   pl.BlockSpec((tm, w), lambda b, j: (bwd(b, j), 0))],
        out_shape=[jax.ShapeDtypeStruct((m, w), F32)] * 2,
        scratch_shapes=[pltpu.VMEM((2, PAIR, w), F32)],
        compiler_params=_cparams(("parallel", "arbitrary")),
        name="rwkv_recurrence",
    )(qeff, oloc, g_all, h_all, qeff, oloc, g_all, h_all)


def _mla_prep_kernel(z_ref, c_ref, s_ref, qn_ref, kvn_ref, qa_ref, qb_ref, kk_ref, kvv_ref,
                     q_out, k_out, v_out, *, scale):
    z = z_ref[...]
    cos = c_ref[...]
    sin = s_ref[...]
    qd = _rms(z[:, :Q_LORA], qn_ref[...]).astype(BF16)
    kvd = _rms(z[:, Q_LORA:Q_LORA + KV_LORA], kvn_ref[...]).astype(BF16)
    o = Q_LORA + KV_LORA
    kr = z[:, o:o + LANES] * cos + z[:, o + LANES:o + 2 * LANES] * sin
    qa = jnp.dot(qd, qa_ref[...], preferred_element_type=F32)
    qb = jnp.dot(qd, qb_ref[...], preferred_element_type=F32)
    kn = jnp.dot(kvd, kk_ref[...], preferred_element_type=F32)
    v_out[...] = lax.dot_general(kvv_ref[...], kvd, (((1,), (1,)), ((), ())),
                                 preferred_element_type=F32).astype(BF16)
    for h in range(MLA_HEADS):
        lanes = slice(h * LANES, (h + 1) * LANES)
        q_out[h] = ((qa[:, lanes] * cos + qb[:, lanes] * sin) * scale).astype(BF16)
        k_out[h] = (kn[:, lanes] + kr).astype(BF16)


def _mla_prep(z2d, cos, sin, batch, seq, q_norm, kv_norm, q_a, q_b, kv_k, kv_v):
    tm = min(512, seq)
    nt = seq // tm
    hw = MLA_HEADS * LANES
    scale = float((QK_NOPE + QK_ROPE) ** -0.5 * math.log2(math.e))
    full = lambda shape: pl.BlockSpec(shape, lambda b, i: (0,) * len(shape))
    col_blk = RW_COLS // MLA_IN
    assert col_blk * MLA_IN == RW_COLS
    return pl.pallas_call(
        functools.partial(_mla_prep_kernel, scale=scale),
        grid=(batch, nt),
        in_specs=[
            pl.BlockSpec((tm, MLA_IN), lambda b, i: (b * nt + i, col_blk)),
            pl.BlockSpec((tm, LANES), lambda b, i: (b * nt + i, 0)),
            pl.BlockSpec((tm, LANES), lambda b, i: (b * nt + i, 0)),
            full((1, Q_LORA)), full((1, KV_LORA)), full((Q_LORA, hw)), full((Q_LORA, hw)),
            full((KV_LORA, hw)), full((MLA_WIDTH, KV_LORA)),
        ],
        out_specs=[
            pl.BlockSpec((None, MLA_HEADS, tm, LANES), lambda b, i: (b, 0, i, 0)),
            pl.BlockSpec((None, MLA_HEADS, tm, LANES), lambda b, i: (b, 0, i, 0)),
            pl.BlockSpec((None, MLA_WIDTH, tm), lambda b, i: (b, 0, i)),
        ],
        out_shape=[
            jax.ShapeDtypeStruct((batch, MLA_HEADS, seq, LANES), BF16),
            jax.ShapeDtypeStruct((batch, MLA_HEADS, seq, LANES), BF16),
            jax.ShapeDtypeStruct((batch, MLA_WIDTH, seq), BF16),
        ],
        compiler_params=_cparams(("parallel", "parallel")),
        name="mla_prep",
    )(z2d, cos, sin, q_norm, kv_norm, q_a, q_b, kv_k, kv_v)


def _flash_kernel(q_ref, k_ref, vt_ref, o_ref, m_ref, l_ref, acc_ref):
    j = pl.program_id(3)

    @pl.when(j == 0)
    def _():
        m_ref[...] = jnp.full_like(m_ref, -jnp.inf)
        l_ref[...] = jnp.zeros_like(l_ref)
        acc_ref[...] = jnp.zeros_like(acc_ref)

    tk = k_ref.shape[1]
    sub = min(FLASH_SUB, tk)
    inst = [(h, slice(b * sub, (b + 1) * sub)) for b in range(tk // sub) for h in range(2)]
    rows = [slice(h * V_HEAD, (h + 1) * V_HEAD) for h in range(2)]
    s = [lax.dot_general(k_ref[h, kb, :], q_ref[h], (((1,), (1,)), ((), ())),
                         preferred_element_type=F32) for h, kb in inst]
    m_loc = [jnp.max(x, axis=0, keepdims=True) for x in s]
    p = [jnp.exp2(x - m).astype(BF16) for x, m in zip(s, m_loc)]
    ones = jnp.ones((ONES_ROWS, sub), BF16)
    pv = [jnp.dot(jnp.concatenate([vt_ref[rows[h], kb], ones], axis=0), x,
                  preferred_element_type=F32) for (h, kb), x in zip(inst, p)]
    l_loc = [x[V_HEAD:V_HEAD + 1, :] for x in pv]
    pv = [x[:V_HEAD, :] for x in pv]
    for h in range(2):
        mine = [i for i, (hh, _) in enumerate(inst) if hh == h]
        m_prev = m_ref[h:h + 1, :]
        m_new = m_prev
        for i in mine:
            m_new = jnp.maximum(m_new, m_loc[i])
        alpha = jnp.exp2(m_prev - m_new)
        l_new = alpha * l_ref[h:h + 1, :]
        acc = alpha * acc_ref[rows[h], :]
        for i in mine:
            w = jnp.exp2(m_loc[i] - m_new)
            l_new = l_new + w * l_loc[i]
            acc = acc + w * pv[i]
        m_ref[h:h + 1, :] = m_new
        l_ref[h:h + 1, :] = l_new
        acc_ref[rows[h], :] = acc

    @pl.when(j == pl.num_programs(3) - 1)
    def _():
        inv = 1.0 / l_ref[...]
        o_t = jnp.concatenate([acc_ref[:V_HEAD, :] * inv[0:1, :], acc_ref[V_HEAD:, :] * inv[1:2, :]],
                              axis=0)
        o_ref[...] = o_t.T


def _flash(q, k, v_t):
    batch, heads, seq, _ = q.shape
    tq = min(FLASH_TQ, seq)
    tk = min(FLASH_TK, seq)
    return pl.pallas_call(
        _flash_kernel,
        grid=(batch, heads // 2, seq // tq, seq // tk),
        in_specs=[
            pl.BlockSpec((None, 2, tq, LANES), lambda b, p, i, j: (b, p, i, 0)),
            pl.BlockSpec((None, 2, tk, LANES), lambda b, p, i, j: (b, p, j, 0)),
            pl.BlockSpec((None, 2 * V_HEAD, tk), lambda b, p, i, j: (b, p, j)),
        ],
        out_specs=pl.BlockSpec((None, tq, LANES), lambda b, p, i, j: (b, i, p)),
        out_shape=jax.ShapeDtypeStruct((batch, seq, MLA_WIDTH), F32),
        scratch_shapes=[pltpu.VMEM((2, tq), F32), pltpu.VMEM((2, tq), F32),
                        pltpu.VMEM((2 * V_HEAD, tq), F32)],
        compiler_params=_cparams(("parallel", "parallel", "parallel", "arbitrary")),
        name="mla_flash",
    )(q, k, v_t)


def _out_kernel(of_ref, ob_ref, bonus_ref, gate_ref, ym_ref, x_ref, hsum_ref, lw_ref, lb_ref,
                on_ref, w_ref, o_ref, ot_ref):
    o = of_ref[...] + ob_ref[...]
    hsum = hsum_ref[...]
    inv_n = 1.0 / HEAD_DIM
    mean = _dot_lsplit(o, hsum, 2) * inv_n
    d = o - mean
    var = _dot_lsplit(d * d, hsum, 2) * inv_n
    y_rw = (d * lax.rsqrt(var + GN_EPS) * lw_ref[...] + lb_ref[...] + bonus_ref[...]) * gate_ref[...]
    y_mla = _rms(ym_ref[...], on_ref[...])
    w = RW_WIDTH
    out = x_ref[...] + _dot(y_rw, w_ref[:w, :]) + _dot(y_mla, w_ref[w:, :])
    o_ref[...] = out
    _rows_to_tiles(ot_ref, out)


def _out_proj(o_f, o_b, bonus, gate, y_mla, x2d, hsum, lnx_w, lnx_b, o_norm, w_out):
    m, d = x2d.shape
    w = RW_WIDTH
    tm = min(256, m)
    row = lambda n: pl.BlockSpec((tm, n), lambda i: (i, 0))
    full = lambda shape: pl.BlockSpec(shape, lambda i: (0,) * len(shape))
    return pl.pallas_call(
        _out_kernel,
        grid=(m // tm,),
        in_specs=[row(w), row(w), row(w), row(w), row(MLA_WIDTH), row(d), full((w, w)),
                  full((1, w)), full((1, w)), full((1, MLA_WIDTH)), full((w + MLA_WIDTH, d))],
        out_specs=[row(d), pl.BlockSpec((tm * SUBLANES, LANES), lambda i: (i, 0))],
        out_shape=[jax.ShapeDtypeStruct((m, d), F32),
                   jax.ShapeDtypeStruct((m * SUBLANES, LANES), F32)],
        compiler_params=_cparams(("parallel",)),
        name="out_proj",
    )(o_f, o_b, bonus, gate, y_mla, x2d, hsum, lnx_w, lnx_b, o_norm, w_out)


def _router_kernel(x_ref, g_ref, rt_ref, a_ref):
    xn = _rms(x_ref[...], g_ref[...])
    logits = _dot32_nt(rt_ref[...], xn)
    mx = jnp.max(logits, axis=0, keepdims=True)
    e = jnp.exp(logits - mx)
    a_ref[...] = e / jnp.sum(e, axis=0, keepdims=True)


def _router(x3d, g, router_t):
    batch, seq, d = x3d.shape
    e = router_t.shape[0]
    tm = min(512, seq)
    return pl.pallas_call(
        _router_kernel,
        grid=(batch, seq // tm),
        in_specs=[pl.BlockSpec((None, tm, d), lambda b, i: (b, i, 0)),
                  pl.BlockSpec((1, d), lambda b, i: (0, 0)),
                  pl.BlockSpec((e, d), lambda b, i: (0, 0))],
        out_specs=pl.BlockSpec((None, e, tm), lambda b, i: (b, 0, i)),
        out_shape=jax.ShapeDtypeStruct((batch, e, seq), F32),
        compiler_params=_cparams(("parallel", "parallel")),
        name="moe_router",
    )(x3d, g, router_t)


def _cumsum_lanes(src_ref, dst_ref, upper):
    rows, t = src_ref.shape

    def body(j, carry):
        cols = pl.ds(pl.multiple_of(j * LANES, LANES), LANES)
        cs = jnp.dot(src_ref[:, cols].astype(BF16), upper, preferred_element_type=F32) + carry
        dst_ref[:, cols] = cs
        return cs[:, LANES - 1:LANES]

    lax.fori_loop(0, t // LANES, body, jnp.zeros((rows, 1), F32))


def _select_kernel(a_ref, res_ref, m_ref, c_ref, rank_ref, *, cap, ct, tt):
    aff = a_ref[...]
    n_e, t = aff.shape
    bits = pltpu.bitcast(aff, I32)

    def search(i, cur):
        cand = cur | jnp.left_shift(jnp.int32(1), 30 - i)
        cnt = jnp.sum(jnp.where(bits >= cand, 1, 0), axis=1, keepdims=True)
        return jnp.where(cnt >= cap, cand, cur)

    thr = lax.fori_loop(0, 31, search, jnp.zeros((n_e, 1), I32))
    gt = bits > thr
    eq = bits == thr
    need = cap - jnp.sum(jnp.where(gt, 1, 0), axis=1, keepdims=True)

    ri = lax.broadcasted_iota(I32, (LANES, LANES), 0)
    ci = lax.broadcasted_iota(I32, (LANES, LANES), 1)
    upper = jnp.where(ri <= ci, 1.0, 0.0).astype(BF16)

    m_ref[...] = jnp.where(eq, 1.0, 0.0)
    _cumsum_lanes(m_ref, c_ref, upper)
    sel = gt | (eq & (c_ref[...] <= need.astype(F32)))
    m_ref[...] = jnp.where(sel, 1.0, 0.0)
    _cumsum_lanes(m_ref, c_ref, upper)
    rank_ref[...] = jnp.where(sel, c_ref[...] - 1.0, -1.0)

    c_iota = lax.broadcasted_iota(I32, (ct, 1), 0).astype(F32)
    vrow = lax.broadcasted_iota(I32, (8, 1), 0)
    t_iota = lax.broadcasted_iota(I32, (1, tt), 1)

    def per_expert(e, _):
        def per_ctile(ci_, _):
            c0 = pl.multiple_of(ci_ * ct, ct)

            def per_ttile(ti_, acc):
                t0 = pl.multiple_of(ti_ * tt, tt)
                rk = rank_ref[pl.ds(e, 1), pl.ds(t0, tt)]
                av = a_ref[pl.ds(e, 1), pl.ds(t0, tt)]
                onehot = jnp.where(rk - c0.astype(F32) == c_iota, 1.0, 0.0).astype(BF16)
                tpos = t_iota + t0
                t_hi = (tpos // LANES).astype(F32)
                t_lo = (tpos % LANES).astype(F32)
                a_hi = av.astype(BF16).astype(F32)
                a_mid = (av - a_hi).astype(BF16).astype(F32)
                a_lo = av - a_hi - a_mid
                vals = jnp.where(vrow == 0, t_hi, jnp.where(vrow == 1, t_lo, jnp.where(
                    vrow == 2, a_hi, jnp.where(vrow == 3, a_mid, jnp.where(vrow == 4, a_lo, 0.0)))))
                return acc + lax.dot_general(vals.astype(BF16), onehot, (((1,), (1,)), ((), ())),
                                             preferred_element_type=F32)

            acc = lax.fori_loop(0, t // tt, per_ttile, jnp.zeros((8, ct), F32))
            res_ref[e, :, pl.ds(c0, ct)] = acc
            return 0

        lax.fori_loop(0, cap // ct, per_ctile, 0)
        return 0

    lax.fori_loop(0, n_e, per_expert, 0)


def _select(aff_t, cap):
    batch, e, seq = aff_t.shape
    ct = min(256, cap)
    tt = min(512, seq)
    kern = functools.partial(_select_kernel, cap=cap, ct=ct, tt=tt)
    res = pl.pallas_call(
        kern,
        grid=(batch,),
        in_specs=[pl.BlockSpec((None, e, seq), lambda b: (b, 0, 0))],
        out_specs=pl.BlockSpec((None, e, 8, cap), lambda b: (b, 0, 0, 0)),
        out_shape=jax.ShapeDtypeStruct((batch, e, 8, cap), F32),
        scratch_shapes=[pltpu.VMEM((e, seq), F32)] * 3,
        compiler_params=_cparams(("parallel",)),
        name="moe_select",
    )(aff_t)
    idx = (res[:, :, 0, :] * LANES + res[:, :, 1, :]).astype(I32)
    gate = res[:, :, 2, :] + res[:, :, 3, :] + res[:, :, 4, :]
    return idx, gate


def _rows_from_tiles(ref):
    n = ref.shape[0] // SUBLANES
    return jnp.concatenate([ref[pl.ds(s, n, stride=SUBLANES), :] for s in range(SUBLANES)], axis=-1)


def _rows_to_tiles(ref, val):
    n = val.shape[0]
    for s in range(SUBLANES):
        ref[pl.ds(s, n, stride=SUBLANES), :] = val[:, s * LANES:(s + 1) * LANES]


def _tile(r):
    return pl.ds(pl.multiple_of(r * SUBLANES, SUBLANES), SUBLANES)


def _gather_kernel(idx_ref, x_ref, o_ref, *, cap, n_e):
    base = (pl.program_id(0) * n_e + pl.program_id(1)) * cap

    def body(c, carry):
        o_ref[_tile(c), :] = x_ref[_tile(idx_ref[base + c]), :]
        return carry

    lax.fori_loop(0, cap, body, 0, unroll=8)


def _gather(idx_flat, x_tiles, n_e, cap):
    batch, rows, _ = x_tiles.shape
    return pl.pallas_call(
        functools.partial(_gather_kernel, cap=cap, n_e=n_e),
        grid_spec=pltpu.PrefetchScalarGridSpec(
            num_scalar_prefetch=1,
            grid=(batch, n_e),
            in_specs=[pl.BlockSpec((None, rows, LANES), lambda b, e, idx: (b, 0, 0),
                                   pipeline_mode=pl.Buffered(1))],
            out_specs=pl.BlockSpec((None, None, cap * SUBLANES, LANES),
                                   lambda b, e, idx: (b, e, 0, 0)),
        ),
        out_shape=jax.ShapeDtypeStruct((batch, n_e, cap * SUBLANES, LANES), F32),
        compiler_params=_cparams(("parallel", "arbitrary")),
        name="moe_gather",
    )(idx_flat, x_tiles)


def _ffn_kernel(x_ref, gate_ref, g_ref, wg_ref, wu_ref, wd_ref, o_ref):
    xn = _rms(_rows_from_tiles(x_ref), g_ref[...]).astype(BF16)
    h1 = jnp.dot(xn, wg_ref[...], preferred_element_type=F32)
    h2 = jnp.dot(xn, wu_ref[...], preferred_element_type=F32)
    hid = (h1 * _sigmoid(h1) * h2).astype(BF16)
    _rows_to_tiles(o_ref, jnp.dot(hid, wd_ref[...], preferred_element_type=F32) * gate_ref[...])


def _expert_ffn(xe, gate_col, g, w_gate, w_up, w_down):
    batch, n_e, rows, _ = xe.shape
    cap = rows // SUBLANES
    d, f = w_gate.shape[1:]
    tc = min(256, cap)
    return pl.pallas_call(
        _ffn_kernel,
        grid=(n_e, batch, cap // tc),
        in_specs=[
            pl.BlockSpec((None, None, tc * SUBLANES, LANES), lambda e, b, c: (b, e, c, 0)),
            pl.BlockSpec((None, None, tc, 1), lambda e, b, c: (b, e, c, 0)),
            pl.BlockSpec((1, d), lambda e, b, c: (0, 0)),
            pl.BlockSpec((None, d, f), lambda e, b, c: (e, 0, 0)),
            pl.BlockSpec((None, d, f), lambda e, b, c: (e, 0, 0)),
            pl.BlockSpec((None, f, d), lambda e, b, c: (e, 0, 0)),
        ],
        out_specs=pl.BlockSpec((None, None, tc * SUBLANES, LANES), lambda e, b, c: (b, e, c, 0)),
        out_shape=jax.ShapeDtypeStruct(xe.shape, F32),
        compiler_params=_cparams(("parallel", "parallel", "parallel")),
        name="moe_ffn",
    )(xe, gate_col, g, w_gate, w_up, w_down)


SCATTER_GROUP = 8


def _scatter_kernel(idx_ref, y_ref, o_ref, *, cap, n_e):
    e = pl.program_id(1)
    base = (pl.program_id(0) * n_e + e) * cap

    @pl.when(e == 0)
    def _():
        o_ref[...] = jnp.zeros_like(o_ref)

    def body(g, carry):
        c0 = g * SCATTER_GROUP
        rows = [idx_ref[base + c0 + i] for i in range(SCATTER_GROUP)]
        new = [o_ref[_tile(r), :] + y_ref[_tile(c0 + i), :] for i, r in enumerate(rows)]
        for r, v in zip(rows, new):
            o_ref[_tile(r), :] = v
        return carry

    lax.fori_loop(0, cap // SCATTER_GROUP, body, 0)


def _scatter_add(idx_flat, ye, seq):
    batch, n_e, rows, _ = ye.shape
    cap = rows // SUBLANES
    return pl.pallas_call(
        functools.partial(_scatter_kernel, cap=cap, n_e=n_e),
        grid_spec=pltpu.PrefetchScalarGridSpec(
            num_scalar_prefetch=1,
            grid=(batch, n_e),
            in_specs=[pl.BlockSpec((None, None, rows, LANES), lambda b, e, idx: (b, e, 0, 0))],
            out_specs=pl.BlockSpec((None, seq * SUBLANES, LANES), lambda b, e, idx: (b, 0, 0),
                                   pipeline_mode=pl.Buffered(1)),
        ),
        out_shape=jax.ShapeDtypeStruct((batch, seq * SUBLANES, LANES), F32),
        compiler_params=_cparams(("parallel", "arbitrary")),
        name="moe_scatter_add",
    )(idx_flat, ye)


def _ple_kernel(x_ref, d_ref, p_ref, g_ref, wp_ref, wg_ref, fg_ref, o_ref, *, final):
    x = x_ref[...] + _rows_from_tiles(d_ref)
    gate = _sigmoid(_dot(_rms(x, g_ref[...]), wg_ref[...]))
    out = x + _dot(p_ref[...], wp_ref[...]) * gate
    if final:
        out = _rms(out, fg_ref[...])
    o_ref[...] = out


def _ple(x2d, delta_tiles, p2d, g, w_proj, w_gate, final_g, final):
    m, d = x2d.shape
    dp = p2d.shape[1]
    tm = min(512, m)
    full = lambda shape: pl.BlockSpec(shape, lambda i: (0,) * len(shape))
    return pl.pallas_call(
        functools.partial(_ple_kernel, final=final),
        grid=(m // tm,),
        in_specs=[pl.BlockSpec((tm, d), lambda i: (i, 0)),
                  pl.BlockSpec((tm * SUBLANES, LANES), lambda i: (i, 0)),
                  pl.BlockSpec((tm, dp), lambda i: (i, 0)),
                  full((1, d)), full((dp, d)), full((d, d)), full((1, d))],
        out_specs=pl.BlockSpec((tm, d), lambda i: (i, 0)),
        out_shape=jax.ShapeDtypeStruct((m, d), F32),
        compiler_params=_cparams(("parallel",)),
        name="ple_final" if final else "ple",
    )(x2d, delta_tiles, p2d, g, w_proj, w_gate, final_g)


def _hi_lo(w):
    hi = w.astype(BF16)
    return hi, (w - hi.astype(F32)).astype(BF16)


def _rot_cols(w):
    half = QK_ROPE // 2
    return jnp.concatenate([-w[..., half:], w[..., :half]], axis=-1)


def _pad_head(nope, rope):
    lead = (nope if nope is not None else rope).shape[:-1]
    n = nope if nope is not None else jnp.zeros(lead + (QK_NOPE,), F32)
    r = rope if rope is not None else jnp.zeros(lead + (QK_ROPE,), F32)
    return jnp.concatenate([n, r, jnp.zeros(lead + (LANES - QK_NOPE - QK_ROPE,), F32)], axis=-1)


def _block_rows(w_pair):
    z = jnp.zeros_like(w_pair[0])
    return jnp.concatenate([jnp.concatenate([w_pair[0], z], axis=1),
                            jnp.concatenate([z, w_pair[1]], axis=1)], axis=0)


def kernel(x, p, positions, attn_norm, w_in, rw_mu, rw_w0, rw_w_up, rw_a0, rw_a_up, rw_g_up, rw_k_k,
           rw_k_a, rw_r_k, rw_lnx_w, rw_lnx_b, mla_q_norm, mla_q_up, mla_kv_norm, mla_kv_up,
           mla_o_norm, w_out, ffn_norm, router, exp_w_gate, exp_w_up, exp_w_down, ple_norm,
           ple_proj, ple_gate, final_norm):
    batch, seq, d = x.shape
    depth = w_in.shape[0]
    m = batch * seq
    cap = EC_FACTOR * seq // N_EXPERTS
    w = RW_WIDTH

    cos, sin = _rope_tables(positions)
    hsum = (jnp.arange(w)[:, None] // HEAD_DIM == jnp.arange(w)[None, :] // HEAD_DIM).astype(BF16)
    x2d = x.reshape(m, d)

    for i in range(depth):
        w_mla = w_in[i][:, RW_COLS:]
        w_kr = w_mla[:, Q_LORA + KV_LORA:]
        w_ext = jnp.concatenate([w_in[i][:, :RW_COLS], w_mla[:, :Q_LORA + KV_LORA],
                                 _pad_head(None, w_kr), _pad_head(None, _rot_cols(w_kr))],
                                axis=1).astype(BF16)
        q_up = mla_q_up[i].reshape(Q_LORA, MLA_HEADS, QK_NOPE + QK_ROPE)
        q_a = _pad_head(q_up[..., :QK_NOPE], q_up[..., QK_NOPE:]).reshape(Q_LORA, -1).astype(BF16)
        q_b = _pad_head(None, _rot_cols(q_up[..., QK_NOPE:])).reshape(Q_LORA, -1).astype(BF16)
        kv_up = mla_kv_up[i].reshape(KV_LORA, MLA_HEADS, QK_NOPE + V_HEAD)
        kv_k = _pad_head(kv_up[..., :QK_NOPE], None).reshape(KV_LORA, -1).astype(BF16)
        kv_v = kv_up[..., QK_NOPE:].reshape(KV_LORA, MLA_WIDTH).T.astype(BF16)

        z2d = _in_proj(x2d, attn_norm[i][None, :], w_ext)
        qeff, oloc, g_all, h_all, bonus, gate = _rwkv_a(
            z2d, seq, rw_mu[i][None, :], rw_w0[i].reshape(1, 2 * w), _block_rows(rw_w_up[i]),
            rw_a0[i].reshape(1, 2 * w), _block_rows(rw_a_up[i]), rw_g_up[i], rw_k_k[i][None, :],
            rw_k_a[i][None, :], rw_r_k[i].reshape(1, w), hsum)
        o_f, o_b = _rwkv_b(qeff, oloc, g_all, h_all, batch, seq)
        q, k, v = _mla_prep(z2d, cos, sin, batch, seq, mla_q_norm[i][None, :],
                            mla_kv_norm[i][None, :], q_a, q_b, kv_k, kv_v)
        y_mla = _flash(q, k, v).reshape(m, MLA_WIDTH)
        x2d, x_tiles = _out_proj(o_f, o_b, bonus, gate, y_mla, x2d, hsum, rw_lnx_w[i][None, :],
                                 rw_lnx_b[i][None, :], mla_o_norm[i][None, :],
                                 w_out[i].astype(BF16))

        aff_t = _router(x2d.reshape(batch, seq, d), ffn_norm[i][None, :], router[i].T)
        idx, gates = _select(aff_t, cap)
        idx_flat = idx.reshape(-1)
        xe = _gather(idx_flat, x_tiles.reshape(batch, seq * SUBLANES, LANES), N_EXPERTS, cap)
        ye = _expert_ffn(xe, gates.reshape(batch, N_EXPERTS, cap, 1), ffn_norm[i][None, :],
                         exp_w_gate[i].astype(BF16), exp_w_up[i].astype(BF16),
                         exp_w_down[i].astype(BF16))
        delta = _scatter_add(idx_flat, ye, seq).reshape(m * SUBLANES, LANES)

        x2d = _ple(x2d, delta, p[i].reshape(m, -1), ple_norm[i][None, :], ple_proj[i].astype(BF16),
                   ple_gate[i].astype(BF16), final_norm[None, :], final=(i == depth - 1))

    return x2d.reshape(batch, seq, d)
```

```python
import functools
import math

import jax
import jax.numpy as jnp
from jax import lax
from jax.experimental import pallas as pl
from jax.experimental.pallas import tpu as pltpu

F32 = jnp.float32
BF16 = jnp.bfloat16
I32 = jnp.int32
HIGHEST = lax.Precision.HIGHEST

RW_HEADS = 8
HEAD_DIM = 64
RW_WIDTH = RW_HEADS * HEAD_DIM
RW_COLS = 3 * RW_WIDTH + 2 * 64 + 2 * 64 + 128
MLA_HEADS = 8
QK_NOPE = 64
QK_ROPE = 32
V_HEAD = 64
Q_LORA = 256
KV_LORA = 128
MLA_WIDTH = MLA_HEADS * V_HEAD
MLA_IN = Q_LORA + KV_LORA + 2 * 128
ROPE_THETA = 10000.0
N_EXPERTS = 16
EC_FACTOR = 2
NORM_EPS = 1e-6
GN_EPS = 64e-5

LANES = 128
SUBLANES = 8
CHUNK = 64
PAIR = 2 * HEAD_DIM
N_PAIRS = RW_WIDTH // PAIR
VMEM_LIMIT = 56 * 1024 * 1024
FLASH_TQ = 1024
FLASH_TK = 2048
FLASH_SUB = 256
ONES_ROWS = 16


def _cparams(sem):
    return pltpu.CompilerParams(dimension_semantics=sem, vmem_limit_bytes=VMEM_LIMIT)


def _rms(x, g):
    return x * lax.rsqrt(jnp.mean(x * x, axis=-1, keepdims=True) + NORM_EPS) * g


def _sigmoid(x):
    return 1.0 / (1.0 + jnp.exp(-x))


def _dot(a, b):
    return jnp.dot(a.astype(BF16), b.astype(BF16), preferred_element_type=F32)


def _dot32(a, b):
    return jnp.dot(a, b, preferred_element_type=F32, precision=HIGHEST)


def _split_bf16(x, parts):
    out = []
    rest = x
    for _ in range(parts):
        hi = rest.astype(BF16)
        out.append(hi)
        rest = rest - hi.astype(F32)
    return out


def _dot_lsplit(a, b_exact, parts):
    acc = None
    for term in _split_bf16(a, parts):
        d = jnp.dot(term, b_exact, preferred_element_type=F32)
        acc = d if acc is None else acc + d
    return acc


def _dot_rsplit(a_exact, b, parts):
    acc = None
    for term in _split_bf16(b, parts):
        d = jnp.dot(a_exact, term, preferred_element_type=F32)
        acc = d if acc is None else acc + d
    return acc


def _dot_3pass(a, b_hi, b_lo):
    a_hi, a_lo = _split_bf16(a, 2)
    return (jnp.dot(a_hi, b_hi, preferred_element_type=F32)
            + jnp.dot(a_lo, b_hi, preferred_element_type=F32)
            + jnp.dot(a_hi, b_lo, preferred_element_type=F32))


def _dot32_nt(a, b):
    return lax.dot_general(a, b, (((1,), (1,)), ((), ())), preferred_element_type=F32,
                           precision=HIGHEST)


def _dot32_tn(a, b):
    return lax.dot_general(a, b, (((0,), (0,)), ((), ())), preferred_element_type=F32,
                           precision=HIGHEST)


def _dot_tn(a, b):
    return lax.dot_general(a.astype(BF16), b.astype(BF16), (((0,), (0,)), ((), ())),
                           preferred_element_type=F32)


def _dot_nt(a, b):
    return lax.dot_general(a.astype(BF16), b.astype(BF16), (((1,), (1,)), ((), ())),
                           preferred_element_type=F32)


def _rope_kernel(pos_ref, inv_ref, c_ref, s_ref):
    ang = pos_ref[...] * inv_ref[...]
    c_ref[...] = jnp.cos(ang)
    s_ref[...] = jnp.sin(ang)


def _rope_tables(positions):
    b, t = positions.shape
    m = b * t
    inv = ROPE_THETA ** (-jnp.arange(0, QK_ROPE, 2, dtype=F32) / QK_ROPE)
    inv_row = jnp.concatenate([jnp.zeros((QK_NOPE,), F32), inv, inv,
                               jnp.zeros((LANES - QK_NOPE - QK_ROPE,), F32)])[None, :]
    posf = jnp.broadcast_to(positions.astype(F32).reshape(m, 1), (m, LANES))
    tm = min(1024, m)
    return pl.pallas_call(
        _rope_kernel,
        grid=(m // tm,),
        in_specs=[pl.BlockSpec((tm, LANES), lambda i: (i, 0)),
                  pl.BlockSpec((1, LANES), lambda i: (0, 0))],
        out_specs=[pl.BlockSpec((tm, LANES), lambda i: (i, 0))] * 2,
        out_shape=[jax.ShapeDtypeStruct((m, LANES), F32)] * 2,
        compiler_params=_cparams(("parallel",)),
        name="rope_tables",
    )(posf, inv_row)


def _in_kernel(x_ref, g_ref, w_ref, o_ref):
    h = _rms(x_ref[...], g_ref[...])
    o_ref[...] = jnp.dot(h.astype(BF16), w_ref[...], preferred_element_type=F32)


def _in_proj(x2d, g, w_ext):
    m, d = x2d.shape
    n = w_ext.shape[1]
    tm = min(512, m)
    return pl.pallas_call(
        _in_kernel,
        grid=(m // tm,),
        in_specs=[pl.BlockSpec((tm, d), lambda i: (i, 0)),
                  pl.BlockSpec((1, d), lambda i: (0, 0)),
                  pl.BlockSpec((d, n), lambda i: (0, 0))],
        out_specs=pl.BlockSpec((tm, n), lambda i: (i, 0)),
        out_shape=jax.ShapeDtypeStruct((m, n), F32),
        compiler_params=_cparams(("parallel",)),
        name="in_proj",
    )(x2d, g, w_ext)


def _pair_masks():
    i = lax.broadcasted_iota(I32, (PAIR, PAIR), 0)
    j = lax.broadcasted_iota(I32, (PAIR, PAIR), 1)
    same = (i // CHUNK) == (j // CHUNK)
    li = i % CHUNK
    lj = j % CHUNK
    return same, li, lj, i == j


def _chunk_pair(a_t, b_t, k_t, r_t, v, b_h, k_h, g_last, consts):
    n = len(a_t)
    idx = range(n)
    strict = [c[0] for c in consts]
    incl = [c[1] for c in consts]
    levels = [c[2] for c in consts]
    eye, m0, m1 = consts[0][3:]

    def stack(x):
        return jnp.concatenate([jnp.where(m0, x, jnp.zeros_like(x)),
                                jnp.where(m1, x, jnp.zeros_like(x))], axis=0)

    a2, b2, k2, v2, bh2, kh2, r2 = ([stack(x) for x in xs] for xs in (a_t, b_t, k_t, v, b_h, k_h, r_t))
    prod = [_dot_nt(jnp.concatenate([a2[i], r2[i].astype(BF16)], axis=0),
                    jnp.concatenate([b2[i], k2[i]], axis=0)) for i in idx]
    n_mat = [jnp.where(strict[i], prod[i][:PAIR, :PAIR], 0.0) for i in idx]
    m_ak = [jnp.where(strict[i], prod[i][:PAIR, PAIR:], 0.0).astype(BF16) for i in idx]
    m_rb = [jnp.where(incl[i], prod[i][PAIR:, :PAIR], 0.0).astype(BF16) for i in idx]
    m_rk = [jnp.where(incl[i], prod[i][PAIR:, PAIR:], 0.0).astype(BF16) for i in idx]
    mv = [_dot(jnp.concatenate([m_ak[i], m_rk[i]], axis=0), v2[i]) for i in idx]

    x = [jnp.where(eye, 1.0, 0.0) - jnp.where(levels[i][0], n_mat[i], 0.0) for i in idx]
    n_bf = [n_mat[i].astype(BF16) for i in idx]
    zero = jnp.zeros((PAIR, PAIR), BF16)
    for lv in range(1, len(levels[0])):
        cx = [_dot(jnp.where(levels[i][lv], n_bf[i], zero), x[i]) for i in idx]
        x = [x[i] - _dot(x[i], cx[i]) for i in idx]

    tw = [_dot(x[i], jnp.concatenate([a2[i], mv[i][:PAIR].astype(BF16)], axis=1))
          for i in idx]
    tw_bf = [t.astype(BF16) for t in tw]
    qo = [jnp.concatenate([r2[i], mv[i][PAIR:]], axis=1) - _dot(m_rb[i], tw_bf[i])
          for i in idx]
    bt = [_dot_tn(bh2[i], tw_bf[i]) for i in idx]
    kv = [_dot_tn(kh2[i], v2[i]) for i in idx]
    out = []
    for i in idx:
        g_mat = jnp.where(eye, g_last[i], 0.0) - bt[i][:, :PAIR]
        h_mat = kv[i] - bt[i][:, PAIR:]
        q = qo[i][:CHUNK] + qo[i][CHUNK:]
        out.append((q[:, :PAIR], q[:, PAIR:], g_mat, h_mat))
    return out


def _rwkv_a_kernel(z_ref, zp_ref, zn_ref, mu_ref, w0_ref, wuph_ref, wupl_ref, a0_ref, auph_ref,
                   aupl_ref, gup_ref, kk_ref, ka_ref, rk_ref, hsum_ref, trif_ref, trib_ref,
                   q_out, ol_out, g_out, h_out, bonus_out, gate_out,
                   at_s, bt_s, kt_s, rt_s, bh_s, kh_s, v_s, gl_s, *, tm, seq):
    i = pl.program_id(0)
    z = z_ref[...]
    row = lax.broadcasted_iota(I32, (tm, 1), 0)
    has_prev = (i * tm) % seq != 0
    has_next = ((i + 1) * tm) % seq != 0
    prev_row = jnp.where(has_prev, zp_ref[7:8, :], 0.0)
    next_row = jnp.where(has_next, zn_ref[0:1, :], 0.0)
    z_dn = jnp.where(row == 0, prev_row, pltpu.roll(z, 1, axis=0))
    z_up = jnp.where(row == tm - 1, next_row, pltpu.roll(z, tm - 1, axis=0))
    zs = z + mu_ref[...] * (0.5 * (z_dn + z_up) - z)

    w = RW_WIDTH
    r = zs[:, :w]
    k = zs[:, w:2 * w]
    v = zs[:, 2 * w:3 * w]
    wd = zs[:, 3 * w:3 * w + 128]
    ad = zs[:, 3 * w + 128:3 * w + 256]
    gd = zs[:, 3 * w + 256:3 * w + 384]

    hsum = hsum_ref[...]
    w_logit = w0_ref[...] + _dot_3pass(jnp.tanh(wd), wuph_ref[...], wupl_ref[...])
    lw = -_sigmoid(w_logit) * jnp.exp(jnp.float32(-0.5))
    a = _sigmoid(a0_ref[...] + _dot_3pass(ad, auph_ref[...], aupl_ref[...]))
    gate_out[...] = _dot(_sigmoid(gd), gup_ref[...])
    kkr = k * kk_ref[...]
    kkn = kkr / jnp.maximum(jnp.sqrt(_dot_lsplit(kkr * kkr, hsum, 2)), 1e-12)
    kd = [k * (1.0 + (a[:, d * w:(d + 1) * w] - 1.0) * ka_ref[...]) for d in range(2)]
    bonus_out[...] = _dot_lsplit(r * (0.5 * (kd[0] + kd[1])) * rk_ref[...], hsum, 2) * v
    v_s[...] = v.astype(BF16)

    for d, tri_ref in enumerate((trif_ref, trib_ref)):
        cols = slice(d * w, (d + 1) * w)
        lw_d = lw[:, cols]
        cum = _dot_rsplit(tri_ref[...], lw_d, 3)
        ends = [c * CHUNK if d == 1 else (c + 1) * CHUNK - 1 for c in range(tm // CHUNK)]
        tot = jnp.concatenate([jnp.broadcast_to(cum[e:e + 1, :], (CHUNK, w)) for e in ends], axis=0)
        g_inv = jnp.exp(-cum)
        g_end = jnp.exp(tot - cum)
        b = kkn * a[:, cols]
        at_s[d] = (kkn * jnp.exp(cum - lw_d)).astype(BF16)
        bt_s[d] = (b * g_inv).astype(BF16)
        kt_s[d] = (kd[d] * g_inv).astype(BF16)
        rt_s[d] = r * jnp.exp(cum)
        bh_s[d] = (b * g_end).astype(BF16)
        kh_s[d] = (kd[d] * g_end).astype(BF16)
        gl_s[d] = jnp.exp(tot)

    same, li, lj, eye = _pair_masks()
    lane = lax.broadcasted_iota(I32, (1, PAIR), 1)
    m0 = lane < HEAD_DIM
    m1 = lane >= HEAD_DIM
    consts = []
    for reverse in (False, True):
        before = (lj > li) if reverse else (lj < li)
        strict = same & before
        incl = same & (before | (li == lj))
        levels = []
        s = 1
        while s < CHUNK:
            blk = same & ((li // (2 * s)) == (lj // (2 * s)))
            hi_row = (li // s) % 2 == 1
            hi_col = (lj // s) % 2 == 1
            levels.append(blk & ((~hi_row & hi_col) if reverse else (hi_row & ~hi_col)))
            s *= 2
        consts.append((strict, incl, levels, eye, m0, m1))

    def chunk_body(c, carry):
        r0 = pl.multiple_of(c * CHUNK, CHUNK)
        rows = pl.ds(r0, CHUNK)
        inst = [(d, slice(p * PAIR, (p + 1) * PAIR)) for d in range(2) for p in range(N_PAIRS)]
        outs = _chunk_pair(
            [at_s[d, rows, ln] for d, ln in inst], [bt_s[d, rows, ln] for d, ln in inst],
            [kt_s[d, rows, ln] for d, ln in inst], [rt_s[d, rows, ln] for d, ln in inst],
            [v_s[rows, ln] for d, ln in inst], [bh_s[d, rows, ln] for d, ln in inst],
            [kh_s[d, rows, ln] for d, ln in inst], [gl_s[d, pl.ds(r0, 1), ln] for d, ln in inst],
            [consts[d] for d, ln in inst])
        for (d, ln), (qe, ol, g_mat, h_mat) in zip(inst, outs):
            q_out[d, rows, ln] = qe.astype(BF16)
            ol_out[d, rows, ln] = ol
            g_out[d, c, :, ln] = g_mat.astype(BF16)
            h_out[d, c, :, ln] = h_mat
        return carry

    lax.fori_loop(0, tm // CHUNK, chunk_body, 0)


def _rwkv_a(z2d, seq, mu, w0, wup, a0, aup, gup, k_k, k_a, r_k, hsum):
    m = z2d.shape[0]
    tm = min(256, seq)
    nc = tm // CHUNK
    w = RW_WIDTH
    full = lambda shape: pl.BlockSpec(shape, lambda i: (0,) * len(shape))
    last8 = m // 8 - 1
    ti = jnp.arange(tm)[:, None]
    tj = jnp.arange(tm)[None, :]
    same_chunk = (ti // CHUNK) == (tj // CHUNK)
    tri_f = (same_chunk & (tj <= ti)).astype(BF16)
    tri_b = (same_chunk & (tj >= ti)).astype(BF16)
    wup_hi, wup_lo = _hi_lo(wup)
    aup_hi, aup_lo = _hi_lo(aup)
    kern = functools.partial(_rwkv_a_kernel, tm=tm, seq=seq)
    return pl.pallas_call(
        kern,
        grid=(m // tm,),
        in_specs=[
            pl.BlockSpec((tm, RW_COLS), lambda i: (i, 0)),
            pl.BlockSpec((8, RW_COLS), lambda i: (jnp.maximum(i * (tm // 8) - 1, 0), 0)),
            pl.BlockSpec((8, RW_COLS), lambda i: (jnp.minimum((i + 1) * (tm // 8), last8), 0)),
            full((1, RW_COLS)), full((1, 2 * w)), full((128, 2 * w)), full((128, 2 * w)),
            full((1, 2 * w)), full((128, 2 * w)), full((128, 2 * w)), full((128, w)),
            full((1, w)), full((1, w)), full((1, w)), full((w, w)), full((tm, tm)), full((tm, tm)),
        ],
        out_specs=[
            pl.BlockSpec((2, tm, w), lambda i: (0, i, 0)),
            pl.BlockSpec((2, tm, w), lambda i: (0, i, 0)),
            pl.BlockSpec((2, nc, PAIR, w), lambda i: (0, i, 0, 0)),
            pl.BlockSpec((2, nc, PAIR, w), lambda i: (0, i, 0, 0)),
            pl.BlockSpec((tm, w), lambda i: (i, 0)),
            pl.BlockSpec((tm, w), lambda i: (i, 0)),
        ],
        out_shape=[
            jax.ShapeDtypeStruct((2, m, w), BF16),
            jax.ShapeDtypeStruct((2, m, w), F32),
            jax.ShapeDtypeStruct((2, m // CHUNK, PAIR, w), BF16),
            jax.ShapeDtypeStruct((2, m // CHUNK, PAIR, w), F32),
            jax.ShapeDtypeStruct((m, w), F32),
            jax.ShapeDtypeStruct((m, w), F32),
        ],
        scratch_shapes=[pltpu.VMEM((2, tm, w), BF16)] * 3 + [pltpu.VMEM((2, tm, w), F32)]
        + [pltpu.VMEM((2, tm, w), BF16)] * 2 + [pltpu.VMEM((tm, w), BF16), pltpu.VMEM((2, tm, w), F32)],
        compiler_params=_cparams(("parallel",)),
        name="rwkv_chunk_local",
    )(z2d, z2d, z2d, mu, w0, wup_hi, wup_lo, a0, aup_hi, aup_lo, gup.astype(BF16), k_k, k_a, r_k,
      hsum, tri_f, tri_b)


def _rwkv_b_kernel(qf_ref, olf_ref, gf_ref, hf_ref, qb_ref, olb_ref, gb_ref, hb_ref,
                   of_ref, ob_ref, s_ref, *, cb):
    @pl.when(pl.program_id(1) == 0)
    def _():
        s_ref[...] = jnp.zeros_like(s_ref)

    for step in range(cb):
        inst = []
        for d, refs in enumerate(((qf_ref, olf_ref, gf_ref, hf_ref, of_ref),
                                  (qb_ref, olb_ref, gb_ref, hb_ref, ob_ref))):
            c = cb - 1 - step if d == 1 else step
            for p in range(N_PAIRS):
                inst.append((d, c, slice(c * CHUNK, (c + 1) * CHUNK),
                             slice(p * PAIR, (p + 1) * PAIR)) + refs)
        s_bf = [s_ref[d, :, ln].astype(BF16) for d, c, rows, ln, *_ in inst]
        s_new = [jnp.dot(g_ref[c, :, ln], sb, preferred_element_type=F32) + h_ref[c, :, ln]
                 for (d, c, rows, ln, q_ref, ol_ref, g_ref, h_ref, o_ref), sb in zip(inst, s_bf)]
        o_val = [jnp.dot(q_ref[rows, ln], sb, preferred_element_type=F32) + ol_ref[rows, ln]
                 for (d, c, rows, ln, q_ref, ol_ref, g_ref, h_ref, o_ref), sb in zip(inst, s_bf)]
        for (d, c, rows, ln, q_ref, ol_ref, g_ref, h_ref, o_ref), sn, ov in zip(inst, s_new, o_val):
            s_ref[d, :, ln] = sn
            o_ref[rows, ln] = ov


def _rwkv_b(qeff, oloc, g_all, h_all, batch, seq):
    m = batch * seq
    w = RW_WIDTH
    cb = min(4, seq // CHUNK)
    tm = cb * CHUNK
    nb = seq // tm

    def fwd(b, j):
        return b * nb + j

    def bwd(b, j):
        return b * nb + nb - 1 - j

    def specs(d, blk):
        return [
            pl.BlockSpec((None, tm, w), lambda b, j: (d, blk(b, j), 0)),
            pl.BlockSpec((None, tm, w), lambda b, j: (d, blk(b, j), 0)),
            pl.BlockSpec((None, cb, PAIR, w), lambda b, j: (d, blk(b, j), 0, 0)),
            pl.BlockSpec((None, cb, PAIR, w), lambda b, j: (d, blk(b, j), 0, 0)),
        ]

    return pl.pallas_call(
        functools.partial(_rwkv_b_kernel, cb=cb),
        grid=(batch, nb),
        in_specs=specs(0, fwd) + specs(1, bwd),
        out_specs=[pl.BlockSpec((tm, w), lambda b, j: (fwd(b, j), 0)),
                   pl.BlockSpec((tm, w), lambda b, j: (bwd(b, j), 0))],
        out_shape=[jax.ShapeDtypeStruct((m, w), F32)] * 2,
        scratch_shapes=[pltpu.VMEM((2, PAIR, w), F32)],
        compiler_params=_cparams(("parallel", "arbitrary")),
        name="rwkv_recurrence",
    )(qeff, oloc, g_all, h_all, qeff, oloc, g_all, h_all)


def _mla_prep_kernel(z_ref, c_ref, s_ref, qn_ref, kvn_ref, qa_ref, qb_ref, kk_ref, kvv_ref,
                     q_out, k_out, v_out, *, scale):
    z = z_ref[...]
    cos = c_ref[...]
    sin = s_ref[...]
    qd = _rms(z[:, :Q_LORA], qn_ref[...]).astype(BF16)
    kvd = _rms(z[:, Q_LORA:Q_LORA + KV_LORA], kvn_ref[...]).astype(BF16)
    o = Q_LORA + KV_LORA
    kr = z[:, o:o + LANES] * cos + z[:, o + LANES:o + 2 * LANES] * sin
    qa = jnp.dot(qd, qa_ref[...], preferred_element_type=F32)
    qb = jnp.dot(qd, qb_ref[...], preferred_element_type=F32)
    kn = jnp.dot(kvd, kk_ref[...], preferred_element_type=F32)
    v_out[...] = lax.dot_general(kvv_ref[...], kvd, (((1,), (1,)), ((), ())),
                                 preferred_element_type=F32).astype(BF16)
    for h in range(MLA_HEADS):
        lanes = slice(h * LANES, (h + 1) * LANES)
        q_out[h] = ((qa[:, lanes] * cos + qb[:, lanes] * sin) * scale).astype(BF16)
        k_out[h] = (kn[:, lanes] + kr).astype(BF16)


def _mla_prep(z2d, cos, sin, batch, seq, q_norm, kv_norm, q_a, q_b, kv_k, kv_v):
    tm = min(512, seq)
    nt = seq // tm
    hw = MLA_HEADS * LANES
    scale = float((QK_NOPE + QK_ROPE) ** -0.5 * math.log2(math.e))
    full = lambda shape: pl.BlockSpec(shape, lambda b, i: (0,) * len(shape))
    col_blk = RW_COLS // MLA_IN
    assert col_blk * MLA_IN == RW_COLS
    return pl.pallas_call(
        functools.partial(_mla_prep_kernel, scale=scale),
        grid=(batch, nt),
        in_specs=[
            pl.BlockSpec((tm, MLA_IN), lambda b, i: (b * nt + i, col_blk)),
            pl.BlockSpec((tm, LANES), lambda b, i: (b * nt + i, 0)),
            pl.BlockSpec((tm, LANES), lambda b, i: (b * nt + i, 0)),
            full((1, Q_LORA)), full((1, KV_LORA)), full((Q_LORA, hw)), full((Q_LORA, hw)),
            full((KV_LORA, hw)), full((MLA_WIDTH, KV_LORA)),
        ],
        out_specs=[
            pl.BlockSpec((None, MLA_HEADS, tm, LANES), lambda b, i: (b, 0, i, 0)),
            pl.BlockSpec((None, MLA_HEADS, tm, LANES), lambda b, i: (b, 0, i, 0)),
            pl.BlockSpec((None, MLA_WIDTH, tm), lambda b, i: (b, 0, i)),
        ],
        out_shape=[
            jax.ShapeDtypeStruct((batch, MLA_HEADS, seq, LANES), BF16),
            jax.ShapeDtypeStruct((batch, MLA_HEADS, seq, LANES), BF16),
            jax.ShapeDtypeStruct((batch, MLA_WIDTH, seq), BF16),
        ],
        compiler_params=_cparams(("parallel", "parallel")),
        name="mla_prep",
    )(z2d, cos, sin, q_norm, kv_norm, q_a, q_b, kv_k, kv_v)


def _flash_kernel(q_ref, k_ref, vt_ref, o_ref, m_ref, l_ref, acc_ref):
    j = pl.program_id(3)

    @pl.when(j == 0)
    def _():
        m_ref[...] = jnp.full_like(m_ref, -jnp.inf)
        l_ref[...] = jnp.zeros_like(l_ref)
        acc_ref[...] = jnp.zeros_like(acc_ref)

    tk = k_ref.shape[1]
    sub = min(FLASH_SUB, tk)
    inst = [(h, slice(b * sub, (b + 1) * sub)) for b in range(tk // sub) for h in range(2)]
    rows = [slice(h * V_HEAD, (h + 1) * V_HEAD) for h in range(2)]
    s = [lax.dot_general(k_ref[h, kb, :], q_ref[h], (((1,), (1,)), ((), ())),
                         preferred_element_type=F32) for h, kb in inst]
    m_loc = [jnp.max(x, axis=0, keepdims=True) for x in s]
    p = [jnp.exp2(x - m).astype(BF16) for x, m in zip(s, m_loc)]
    ones = jnp.ones((ONES_ROWS, sub), BF16)
    pv = [jnp.dot(jnp.concatenate([vt_ref[rows[h], kb], ones], axis=0), x,
                  preferred_element_type=F32) for (h, kb), x in zip(inst, p)]
    l_loc = [x[V_HEAD:V_HEAD + 1, :] for x in pv]
    pv = [x[:V_HEAD, :] for x in pv]
    for h in range(2):
        mine = [i for i, (hh, _) in enumerate(inst) if hh == h]
        m_prev = m_ref[h:h + 1, :]
        m_new = m_prev
        for i in mine:
            m_new = jnp.maximum(m_new, m_loc[i])
        alpha = jnp.exp2(m_prev - m_new)
        l_new = alpha * l_ref[h:h + 1, :]
        acc = alpha * acc_ref[rows[h], :]
        for i in mine:
            w = jnp.exp2(m_loc[i] - m_new)
            l_new = l_new + w * l_loc[i]
            acc = acc + w * pv[i]
        m_ref[h:h + 1, :] = m_new
        l_ref[h:h + 1, :] = l_new
        acc_ref[rows[h], :] = acc

    @pl.when(j == pl.num_programs(3) - 1)
    def _():
        inv = 1.0 / l_ref[...]
        o_t = jnp.concatenate([acc_ref[:V_HEAD, :] * inv[0:1, :], acc_ref[V_HEAD:, :] * inv[1:2, :]],
                              axis=0)
        o_ref[...] = o_t.T


def _flash(q, k, v_t):
    batch, heads, seq, _ = q.shape
    tq = min(FLASH_TQ, seq)
    tk = min(FLASH_TK, seq)
    return pl.pallas_call(
        _flash_kernel,
        grid=(batch, heads // 2, seq // tq, seq // tk),
        in_specs=[
            pl.BlockSpec((None, 2, tq, LANES), lambda b, p, i, j: (b, p, i, 0)),
            pl.BlockSpec((None, 2, tk, LANES), lambda b, p, i, j: (b, p, j, 0)),
            pl.BlockSpec((None, 2 * V_HEAD, tk), lambda b, p, i, j: (b, p, j)),
        ],
        out_specs=pl.BlockSpec((None, tq, LANES), lambda b, p, i, j: (b, i, p)),
        out_shape=jax.ShapeDtypeStruct((batch, seq, MLA_WIDTH), F32),
        scratch_shapes=[pltpu.VMEM((2, tq), F32), pltpu.VMEM((2, tq), F32),
                        pltpu.VMEM((2 * V_HEAD, tq), F32)],
        compiler_params=_cparams(("parallel", "parallel", "parallel", "arbitrary")),
        name="mla_flash",
    )(q, k, v_t)


def _out_kernel(of_ref, ob_ref, bonus_ref, gate_ref, ym_ref, x_ref, hsum_ref, lw_ref, lb_ref,
                on_ref, w_ref, o_ref, ot_ref):
    o = of_ref[...] + ob_ref[...]
    hsum = hsum_ref[...]
    inv_n = 1.0 / HEAD_DIM
    mean = _dot_lsplit(o, hsum, 2) * inv_n
    d = o - mean
    var = _dot_lsplit(d * d, hsum, 2) * inv_n
    y_rw = (d * lax.rsqrt(var + GN_EPS) * lw_ref[...] + lb_ref[...] + bonus_ref[...]) * gate_ref[...]
    y_mla = _rms(ym_ref[...], on_ref[...])
    w = RW_WIDTH
    out = x_ref[...] + _dot(y_rw, w_ref[:w, :]) + _dot(y_mla, w_ref[w:, :])
    o_ref[...] = out
    _rows_to_tiles(ot_ref, out)


def _out_proj(o_f, o_b, bonus, gate, y_mla, x2d, hsum, lnx_w, lnx_b, o_norm, w_out):
    m, d = x2d.shape
    w = RW_WIDTH
    tm = min(256, m)
    row = lambda n: pl.BlockSpec((tm, n), lambda i: (i, 0))
    full = lambda shape: pl.BlockSpec(shape, lambda i: (0,) * len(shape))
    return pl.pallas_call(
        _out_kernel,
        grid=(m // tm,),
        in_specs=[row(w), row(w), row(w), row(w), row(MLA_WIDTH), row(d), full((w, w)),
                  full((1, w)), full((1, w)), full((1, MLA_WIDTH)), full((w + MLA_WIDTH, d))],
        out_specs=[row(d), pl.BlockSpec((tm * SUBLANES, LANES), lambda i: (i, 0))],
        out_shape=[jax.ShapeDtypeStruct((m, d), F32),
                   jax.ShapeDtypeStruct((m * SUBLANES, LANES), F32)],
        compiler_params=_cparams(("parallel",)),
        name="out_proj",
    )(o_f, o_b, bonus, gate, y_mla, x2d, hsum, lnx_w, lnx_b, o_norm, w_out)


def _router_kernel(x_ref, g_ref, rt_ref, a_ref):
    xn = _rms(x_ref[...], g_ref[...])
    logits = _dot32_nt(rt_ref[...], xn)
    mx = jnp.max(logits, axis=0, keepdims=True)
    e = jnp.exp(logits - mx)
    a_ref[...] = e / jnp.sum(e, axis=0, keepdims=True)


def _router(x3d, g, router_t):
    batch, seq, d = x3d.shape
    e = router_t.shape[0]
    tm = min(512, seq)
    return pl.pallas_call(
        _router_kernel,
        grid=(batch, seq // tm),
        in_specs=[pl.BlockSpec((None, tm, d), lambda b, i: (b, i, 0)),
                  pl.BlockSpec((1, d), lambda b, i: (0, 0)),
                  pl.BlockSpec((e, d), lambda b, i: (0, 0))],
        out_specs=pl.BlockSpec((None, e, tm), lambda b, i: (b, 0, i)),
        out_shape=jax.ShapeDtypeStruct((batch, e, seq), F32),
        compiler_params=_cparams(("parallel", "parallel")),
        name="moe_router",
    )(x3d, g, router_t)


def _threshold_kernel(a_ref, thr_ref, *, cap):
    bits = pltpu.bitcast(a_ref[...], I32)
    n_e = bits.shape[0]

    def search(i, cur):
        cand = cur | jnp.left_shift(jnp.int32(1), 30 - i)
        cnt = jnp.sum(jnp.where(bits >= cand, 1, 0), axis=1, keepdims=True)
        return jnp.where(cnt >= cap, cand, cur)

    thr_ref[...] = lax.fori_loop(0, 31, search, jnp.zeros((n_e, 1), I32))


def _compact_kernel(a_ref, thr_ref, idx_ref, gate_ref, *, cap):
    a = a_ref[...]
    nb = a.shape[0]
    bits = pltpu.bitcast(a, I32)
    thr = thr_ref[...]
    ri = lax.broadcasted_iota(I32, (LANES, LANES), 0)
    ci = lax.broadcasted_iota(I32, (LANES, LANES), 1)
    upper = jnp.where(ri <= ci, 1.0, 0.0).astype(BF16)
    bi = lax.broadcasted_iota(I32, (nb, nb), 0)
    bj = lax.broadcasted_iota(I32, (nb, nb), 1)
    before = jnp.where(bj < bi, 1.0, 0.0).astype(BF16)

    def total(x):
        return jnp.sum(jnp.sum(x, axis=1, keepdims=True), axis=0, keepdims=True)

    def running(mask):
        within = jnp.dot(mask, upper, preferred_element_type=F32)
        tot = jnp.broadcast_to(within[:, LANES - 1:LANES], (nb, LANES))
        return within, jnp.dot(before, tot.astype(BF16), preferred_element_type=F32)

    gt = bits > thr
    eq = bits == thr
    need = cap - total(jnp.where(gt, 1.0, 0.0))
    w_eq, b_eq = running(jnp.where(eq, 1.0, 0.0).astype(BF16))
    sel = gt | (eq & (w_eq + b_eq <= need))
    sel_b = jnp.where(sel, 1.0, 0.0).astype(BF16)
    within, base = running(sel_b)

    tot_row = lax.dot_general(jnp.ones((SUBLANES, LANES), BF16), sel_b, (((1,), (1,)), ((), ())),
                              preferred_element_type=F32)
    base_row = jnp.dot(tot_row.astype(BF16), jnp.where(bi < bj, 1.0, 0.0).astype(BF16),
                       preferred_element_type=F32)
    c_col = lax.broadcasted_iota(I32, (cap, 1), 0).astype(F32)
    in_block = (base_row[0:1, :] <= c_col) & (c_col < base_row[0:1, :] + tot_row[0:1, :])
    onehot = jnp.where(in_block, 1.0, 0.0).astype(BF16)

    a_parts = _split_bf16(a, 3)
    base_hi = jnp.floor(base * (1.0 / 32.0))
    lane = lax.broadcasted_iota(I32, (1, LANES), 1)
    block_id = lax.broadcasted_iota(I32, (nb, LANES), 0).astype(F32)
    small = jnp.where(lane == 0, base_hi, jnp.where(lane == 1, base - 32.0 * base_hi,
                                                     jnp.where(lane == 2, block_id, 0.0)))
    table = jnp.concatenate([within.astype(BF16), sel_b] + a_parts + [small.astype(BF16)], axis=1)
    g = jnp.dot(onehot, table, preferred_element_type=F32)
    w_c = g[:, :LANES]
    s_c = g[:, LANES:2 * LANES]
    a_c = g[:, 2 * LANES:3 * LANES] + g[:, 3 * LANES:4 * LANES] + g[:, 4 * LANES:5 * LANES]
    sm = g[:, 5 * LANES:]
    target = c_col - (32.0 * sm[:, 0:1] + sm[:, 1:2]) + 1.0
    match = (w_c == target) & (s_c > 0.5)
    pos = jnp.sum(jnp.where(match, lane.astype(F32), 0.0), axis=1, keepdims=True)
    gate_ref[...] = jnp.sum(jnp.where(match, a_c, 0.0), axis=1, keepdims=True)
    idx_ref[...] = (LANES * sm[:, 2:3] + pos).astype(I32)


def _select(aff_t, cap):
    batch, e, seq = aff_t.shape
    nb = seq // LANES
    thr = pl.pallas_call(
        functools.partial(_threshold_kernel, cap=cap),
        grid=(batch,),
        in_specs=[pl.BlockSpec((None, e, seq), lambda b: (b, 0, 0))],
        out_specs=pl.BlockSpec((None, e, 1), lambda b: (b, 0, 0)),
        out_shape=jax.ShapeDtypeStruct((batch, e, 1), I32),
        compiler_params=_cparams(("parallel",)),
        name="moe_threshold",
    )(aff_t)
    idx, gate = pl.pallas_call(
        functools.partial(_compact_kernel, cap=cap),
        grid=(batch, e),
        in_specs=[pl.BlockSpec((None, None, nb, LANES), lambda b, j: (b, j, 0, 0)),
                  pl.BlockSpec((None, None, 1, 1), lambda b, j: (b, j, 0, 0))],
        out_specs=[pl.BlockSpec((None, None, cap, 1), lambda b, j: (b, j, 0, 0))] * 2,
        out_shape=[jax.ShapeDtypeStruct((batch, e, cap, 1), I32),
                   jax.ShapeDtypeStruct((batch, e, cap, 1), F32)],
        compiler_params=_cparams(("parallel", "parallel")),
        name="moe_compact",
    )(aff_t.reshape(batch, e, nb, LANES), thr.reshape(batch, e, 1, 1))
    return idx.reshape(-1), gate


def _rows_from_tiles(ref):
    n = ref.shape[0] // SUBLANES
    return jnp.concatenate([ref[pl.ds(s, n, stride=SUBLANES), :] for s in range(SUBLANES)], axis=-1)


def _rows_to_tiles(ref, val):
    n = val.shape[0]
    for s in range(SUBLANES):
        ref[pl.ds(s, n, stride=SUBLANES), :] = val[:, s * LANES:(s + 1) * LANES]


def _tile(r):
    return pl.ds(pl.multiple_of(r * SUBLANES, SUBLANES), SUBLANES)


def _gather_kernel(idx_ref, x_ref, o_ref, *, cap, n_e):
    base = (pl.program_id(0) * n_e + pl.program_id(1)) * cap

    def body(c, carry):
        o_ref[_tile(c), :] = x_ref[_tile(idx_ref[base + c]), :]
        return carry

    lax.fori_loop(0, cap, body, 0, unroll=8)


def _gather(idx_flat, x_tiles, n_e, cap):
    batch, rows, _ = x_tiles.shape
    return pl.pallas_call(
        functools.partial(_gather_kernel, cap=cap, n_e=n_e),
        grid_spec=pltpu.PrefetchScalarGridSpec(
            num_scalar_prefetch=1,
            grid=(batch, n_e),
            in_specs=[pl.BlockSpec((None, rows, LANES), lambda b, e, idx: (b, 0, 0),
                                   pipeline_mode=pl.Buffered(1))],
            out_specs=pl.BlockSpec((None, None, cap * SUBLANES, LANES),
                                   lambda b, e, idx: (b, e, 0, 0)),
        ),
        out_shape=jax.ShapeDtypeStruct((batch, n_e, cap * SUBLANES, LANES), F32),
        compiler_params=_cparams(("parallel", "arbitrary")),
        name="moe_gather",
    )(idx_flat, x_tiles)


def _ffn_kernel(x_ref, gate_ref, g_ref, wg_ref, wu_ref, wd_ref, o_ref, wg_s, wu_s, wd_s):
    @pl.when((pl.program_id(1) == 0) & (pl.program_id(2) == 0))
    def _():
        wg_s[...] = wg_ref[...].astype(BF16)
        wu_s[...] = wu_ref[...].astype(BF16)
        wd_s[...] = wd_ref[...].astype(BF16)

    xn = _rms(_rows_from_tiles(x_ref), g_ref[...]).astype(BF16)
    h1 = jnp.dot(xn, wg_s[...], preferred_element_type=F32)
    h2 = jnp.dot(xn, wu_s[...], preferred_element_type=F32)
    hid = (h1 * _sigmoid(h1) * h2).astype(BF16)
    _rows_to_tiles(o_ref, jnp.dot(hid, wd_s[...], preferred_element_type=F32) * gate_ref[...])


def _expert_ffn(xe, gate_col, g, w_gate, w_up, w_down, layer):
    batch, n_e, rows, _ = xe.shape
    cap = rows // SUBLANES
    d, f = w_gate.shape[2:]
    tc = min(256, cap)
    return pl.pallas_call(
        _ffn_kernel,
        grid=(n_e, batch, cap // tc),
        in_specs=[
            pl.BlockSpec((None, None, tc * SUBLANES, LANES), lambda e, b, c: (b, e, c, 0)),
            pl.BlockSpec((None, None, tc, 1), lambda e, b, c: (b, e, c, 0)),
            pl.BlockSpec((1, d), lambda e, b, c: (0, 0)),
            pl.BlockSpec((None, None, d, f), lambda e, b, c: (layer, e, 0, 0)),
            pl.BlockSpec((None, None, d, f), lambda e, b, c: (layer, e, 0, 0)),
            pl.BlockSpec((None, None, f, d), lambda e, b, c: (layer, e, 0, 0)),
        ],
        out_specs=pl.BlockSpec((None, None, tc * SUBLANES, LANES), lambda e, b, c: (b, e, c, 0)),
        out_shape=jax.ShapeDtypeStruct(xe.shape, F32),
        scratch_shapes=[pltpu.VMEM((d, f), BF16), pltpu.VMEM((d, f), BF16), pltpu.VMEM((f, d), BF16)],
        compiler_params=_cparams(("parallel", "arbitrary", "arbitrary")),
        name="moe_ffn",
    )(xe, gate_col, g, w_gate, w_up, w_down)


SCATTER_GROUP = 8


def _scatter_kernel(idx_ref, y_ref, o_ref, *, cap, n_e):
    e = pl.program_id(1)
    base = (pl.program_id(0) * n_e + e) * cap

    @pl.when(e == 0)
    def _():
        o_ref[...] = jnp.zeros_like(o_ref)

    def body(g, carry):
        c0 = g * SCATTER_GROUP
        rows = [idx_ref[base + c0 + i] for i in range(SCATTER_GROUP)]
        new = [o_ref[_tile(r), :] + y_ref[_tile(c0 + i), :] for i, r in enumerate(rows)]
        for r, v in zip(rows, new):
            o_ref[_tile(r), :] = v
        return carry

    lax.fori_loop(0, cap // SCATTER_GROUP, body, 0)


def _scatter_add(idx_flat, ye, seq):
    batch, n_e, rows, _ = ye.shape
    cap = rows // SUBLANES
    return pl.pallas_call(
        functools.partial(_scatter_kernel, cap=cap, n_e=n_e),
        grid_spec=pltpu.PrefetchScalarGridSpec(
            num_scalar_prefetch=1,
            grid=(batch, n_e),
            in_specs=[pl.BlockSpec((None, None, rows, LANES), lambda b, e, idx: (b, e, 0, 0))],
            out_specs=pl.BlockSpec((None, seq * SUBLANES, LANES), lambda b, e, idx: (b, 0, 0),
                                   pipeline_mode=pl.Buffered(1)),
        ),
        out_shape=jax.ShapeDtypeStruct((batch, seq * SUBLANES, LANES), F32),
        compiler_params=_cparams(("parallel", "arbitrary")),
        name="moe_scatter_add",
    )(idx_flat, ye)


def _ple_kernel(x_ref, d_ref, p_ref, g_ref, wp_ref, wg_ref, fg_ref, o_ref, *, final):
    x = x_ref[...] + _rows_from_tiles(d_ref)
    gate = _sigmoid(_dot(_rms(x, g_ref[...]), wg_ref[...]))
    out = x + _dot(p_ref[...], wp_ref[...]) * gate
    if final:
        out = _rms(out, fg_ref[...])
    o_ref[...] = out


def _ple(x2d, delta_tiles, p_all, layer, g, w_proj, w_gate, final_g, final):
    m, d = x2d.shape
    dp = p_all.shape[1]
    tm = min(512, m)
    first = layer * (m // tm)
    full = lambda shape: pl.BlockSpec(shape, lambda i: (0,) * len(shape))
    return pl.pallas_call(
        functools.partial(_ple_kernel, final=final),
        grid=(m // tm,),
        in_specs=[pl.BlockSpec((tm, d), lambda i: (i, 0)),
                  pl.BlockSpec((tm * SUBLANES, LANES), lambda i: (i, 0)),
                  pl.BlockSpec((tm, dp), lambda i: (first + i, 0)),
                  full((1, d)), full((dp, d)), full((d, d)), full((1, d))],
        out_specs=pl.BlockSpec((tm, d), lambda i: (i, 0)),
        out_shape=jax.ShapeDtypeStruct((m, d), F32),
        compiler_params=_cparams(("parallel",)),
        name="ple_final" if final else "ple",
    )(x2d, delta_tiles, p_all, g, w_proj, w_gate, final_g)


def _hi_lo(w):
    hi = w.astype(BF16)
    return hi, (w - hi.astype(F32)).astype(BF16)


def _rot_cols(w):
    half = QK_ROPE // 2
    return jnp.concatenate([-w[..., half:], w[..., :half]], axis=-1)


def _pad_head(nope, rope):
    lead = (nope if nope is not None else rope).shape[:-1]
    n = nope if nope is not None else jnp.zeros(lead + (QK_NOPE,), F32)
    r = rope if rope is not None else jnp.zeros(lead + (QK_ROPE,), F32)
    return jnp.concatenate([n, r, jnp.zeros(lead + (LANES - QK_NOPE - QK_ROPE,), F32)], axis=-1)


def _block_rows(w_pair):
    z = jnp.zeros_like(w_pair[0])
    return jnp.concatenate([jnp.concatenate([w_pair[0], z], axis=1),
                            jnp.concatenate([z, w_pair[1]], axis=1)], axis=0)


def kernel(x, p, positions, attn_norm, w_in, rw_mu, rw_w0, rw_w_up, rw_a0, rw_a_up, rw_g_up, rw_k_k,
           rw_k_a, rw_r_k, rw_lnx_w, rw_lnx_b, mla_q_norm, mla_q_up, mla_kv_norm, mla_kv_up,
           mla_o_norm, w_out, ffn_norm, router, exp_w_gate, exp_w_up, exp_w_down, ple_norm,
           ple_proj, ple_gate, final_norm):
    batch, seq, d = x.shape
    depth = w_in.shape[0]
    m = batch * seq
    cap = EC_FACTOR * seq // N_EXPERTS
    w = RW_WIDTH

    cos, sin = _rope_tables(positions)
    hsum = (jnp.arange(w)[:, None] // HEAD_DIM == jnp.arange(w)[None, :] // HEAD_DIM).astype(BF16)
    x2d = x.reshape(m, d)

    for i in range(depth):
        w_mla = w_in[i][:, RW_COLS:]
        w_kr = w_mla[:, Q_LORA + KV_LORA:]
        w_ext = jnp.concatenate([w_in[i][:, :RW_COLS], w_mla[:, :Q_LORA + KV_LORA],
                                 _pad_head(None, w_kr), _pad_head(None, _rot_cols(w_kr))],
                                axis=1).astype(BF16)
        q_up = mla_q_up[i].reshape(Q_LORA, MLA_HEADS, QK_NOPE + QK_ROPE)
        q_a = _pad_head(q_up[..., :QK_NOPE], q_up[..., QK_NOPE:]).reshape(Q_LORA, -1).astype(BF16)
        q_b = _pad_head(None, _rot_cols(q_up[..., QK_NOPE:])).reshape(Q_LORA, -1).astype(BF16)
        kv_up = mla_kv_up[i].reshape(KV_LORA, MLA_HEADS, QK_NOPE + V_HEAD)
        kv_k = _pad_head(kv_up[..., :QK_NOPE], None).reshape(KV_LORA, -1).astype(BF16)
        kv_v = kv_up[..., QK_NOPE:].reshape(KV_LORA, MLA_WIDTH).T.astype(BF16)

        z2d = _in_proj(x2d, attn_norm[i][None, :], w_ext)
        qeff, oloc, g_all, h_all, bonus, gate = _rwkv_a(
            z2d, seq, rw_mu[i][None, :], rw_w0[i].reshape(1, 2 * w), _block_rows(rw_w_up[i]),
            rw_a0[i].reshape(1, 2 * w), _block_rows(rw_a_up[i]), rw_g_up[i], rw_k_k[i][None, :],
            rw_k_a[i][None, :], rw_r_k[i].reshape(1, w), hsum)
        o_f, o_b = _rwkv_b(qeff, oloc, g_all, h_all, batch, seq)
        q, k, v = _mla_prep(z2d, cos, sin, batch, seq, mla_q_norm[i][None, :],
                            mla_kv_norm[i][None, :], q_a, q_b, kv_k, kv_v)
        y_mla = _flash(q, k, v).reshape(m, MLA_WIDTH)
        x2d, x_tiles = _out_proj(o_f, o_b, bonus, gate, y_mla, x2d, hsum, rw_lnx_w[i][None, :],
                                 rw_lnx_b[i][None, :], mla_o_norm[i][None, :],
                                 w_out[i].astype(BF16))

        aff_t = _router(x2d.reshape(batch, seq, d), ffn_norm[i][None, :], router[i].T)
        idx_flat, gates = _select(aff_t, cap)
        xe = _gather(idx_flat, x_tiles.reshape(batch, seq * SUBLANES, LANES), N_EXPERTS, cap)
        ye = _expert_ffn(xe, gates, ffn_norm[i][None, :], exp_w_gate, exp_w_up, exp_w_down, i)
        delta = _scatter_add(idx_flat, ye, seq).reshape(m * SUBLANES, LANES)

        x2d = _ple(x2d, delta, p.reshape(depth * m, -1), i, ple_norm[i][None, :], ple_proj[i].astype(BF16),
                   ple_gate[i].astype(BF16), final_norm[None, :], final=(i == depth - 1))

    return x2d.reshape(batch, seq, d)
```

```python
import functools
import math

import jax
import jax.numpy as jnp
from jax import lax
from jax.experimental import pallas as pl
from jax.experimental.pallas import tpu as pltpu

F32 = jnp.float32
BF16 = jnp.bfloat16
I32 = jnp.int32
HIGHEST = lax.Precision.HIGHEST

RW_HEADS = 8
HEAD_DIM = 64
RW_WIDTH = RW_HEADS * HEAD_DIM
RW_COLS = 3 * RW_WIDTH + 2 * 64 + 2 * 64 + 128
MLA_HEADS = 8
QK_NOPE = 64
QK_ROPE = 32
V_HEAD = 64
Q_LORA = 256
KV_LORA = 128
MLA_WIDTH = MLA_HEADS * V_HEAD
MLA_IN = Q_LORA + KV_LORA + 2 * 128
ROPE_THETA = 10000.0
N_EXPERTS = 16
EC_FACTOR = 2
NORM_EPS = 1e-6
GN_EPS = 64e-5

LANES = 128
SUBLANES = 8
CHUNK = 64
CHUNKS_PER_ITER = 2
PAIR = 2 * HEAD_DIM
N_PAIRS = RW_WIDTH // PAIR
VMEM_LIMIT = 56 * 1024 * 1024
FLASH_TQ = 1024
FLASH_TK = 2048
FLASH_SUB = 256
ONES_ROWS = 16
FLASH_SAFE_LOG2 = 40.0
NORM_MARGIN = 1.05


def _cparams(sem):
    return pltpu.CompilerParams(dimension_semantics=sem, vmem_limit_bytes=VMEM_LIMIT)


def _rms(x, g):
    return x * lax.rsqrt(jnp.mean(x * x, axis=-1, keepdims=True) + NORM_EPS) * g


def _sigmoid(x):
    return 1.0 / (1.0 + jnp.exp(-x))


def _dot(a, b):
    return jnp.dot(a.astype(BF16), b.astype(BF16), preferred_element_type=F32)


def _dot32(a, b):
    return jnp.dot(a, b, preferred_element_type=F32, precision=HIGHEST)


def _split_bf16(x, parts):
    out = []
    rest = x
    for _ in range(parts):
        hi = rest.astype(BF16)
        out.append(hi)
        rest = rest - hi.astype(F32)
    return out


def _dot_lsplit(a, b_exact, parts):
    acc = None
    for term in _split_bf16(a, parts):
        d = jnp.dot(term, b_exact, preferred_element_type=F32)
        acc = d if acc is None else acc + d
    return acc


def _dot_rsplit(a_exact, b, parts):
    acc = None
    for term in _split_bf16(b, parts):
        d = jnp.dot(a_exact, term, preferred_element_type=F32)
        acc = d if acc is None else acc + d
    return acc


def _dot_3pass(a, b_hi, b_lo):
    a_hi, a_lo = _split_bf16(a, 2)
    return (jnp.dot(a_hi, b_hi, preferred_element_type=F32)
            + jnp.dot(a_lo, b_hi, preferred_element_type=F32)
            + jnp.dot(a_hi, b_lo, preferred_element_type=F32))


def _dot32_nt(a, b):
    return lax.dot_general(a, b, (((1,), (1,)), ((), ())), preferred_element_type=F32,
                           precision=HIGHEST)


def _dot32_tn(a, b):
    return lax.dot_general(a, b, (((0,), (0,)), ((), ())), preferred_element_type=F32,
                           precision=HIGHEST)


def _dot_tn(a, b):
    return lax.dot_general(a.astype(BF16), b.astype(BF16), (((0,), (0,)), ((), ())),
                           preferred_element_type=F32)


def _dot_nt(a, b):
    return lax.dot_general(a.astype(BF16), b.astype(BF16), (((1,), (1,)), ((), ())),
                           preferred_element_type=F32)


def _rope_kernel(pos_ref, inv_ref, c_ref, s_ref):
    ang = pos_ref[...] * inv_ref[...]
    c_ref[...] = jnp.cos(ang)
    s_ref[...] = jnp.sin(ang)


def _rope_tables(positions):
    b, t = positions.shape
    m = b * t
    inv = ROPE_THETA ** (-jnp.arange(0, QK_ROPE, 2, dtype=F32) / QK_ROPE)
    inv_row = jnp.concatenate([jnp.zeros((QK_NOPE,), F32), inv, inv,
                               jnp.zeros((LANES - QK_NOPE - QK_ROPE,), F32)])[None, :]
    posf = jnp.broadcast_to(positions.astype(F32).reshape(m, 1), (m, LANES))
    tm = min(1024, m)
    return pl.pallas_call(
        _rope_kernel,
        grid=(m // tm,),
        in_specs=[pl.BlockSpec((tm, LANES), lambda i: (i, 0)),
                  pl.BlockSpec((1, LANES), lambda i: (0, 0))],
        out_specs=[pl.BlockSpec((tm, LANES), lambda i: (i, 0))] * 2,
        out_shape=[jax.ShapeDtypeStruct((m, LANES), F32)] * 2,
        compiler_params=_cparams(("parallel",)),
        name="rope_tables",
    )(posf, inv_row)


def _in_kernel(x_ref, g_ref, w_ref, o_ref):
    h = _rms(x_ref[...], g_ref[...])
    o_ref[...] = jnp.dot(h.astype(BF16), w_ref[...], preferred_element_type=F32)


def _in_proj(x2d, g, w_ext):
    m, d = x2d.shape
    n = w_ext.shape[1]
    tm = min(512, m)
    return pl.pallas_call(
        _in_kernel,
        grid=(m // tm,),
        in_specs=[pl.BlockSpec((tm, d), lambda i: (i, 0)),
                  pl.BlockSpec((1, d), lambda i: (0, 0)),
                  pl.BlockSpec((d, n), lambda i: (0, 0))],
        out_specs=pl.BlockSpec((tm, n), lambda i: (i, 0)),
        out_shape=jax.ShapeDtypeStruct((m, n), F32),
        compiler_params=_cparams(("parallel",)),
        name="in_proj",
    )(x2d, g, w_ext)


def _pair_masks():
    i = lax.broadcasted_iota(I32, (PAIR, PAIR), 0)
    j = lax.broadcasted_iota(I32, (PAIR, PAIR), 1)
    same = (i // CHUNK) == (j // CHUNK)
    li = i % CHUNK
    lj = j % CHUNK
    return same, li, lj, i == j


def _chunk_pair(a_t, b_t, k_t, r_t, v, b_h, k_h, g_last, consts):
    n = len(a_t)
    idx = range(n)
    strict = [c[0] for c in consts]
    incl = [c[1] for c in consts]
    levels = [c[2] for c in consts]
    eye, m0, m1 = consts[0][3:6]
    reverse = [c[6] for c in consts]

    def stack(x):
        return jnp.concatenate([jnp.where(m0, x, jnp.zeros_like(x)),
                                jnp.where(m1, x, jnp.zeros_like(x))], axis=0)

    a2, b2, k2, v2, bh2, kh2, r2 = ([stack(x) for x in xs] for xs in (a_t, b_t, k_t, v, b_h, k_h, r_t))
    prod = [_dot_nt(jnp.concatenate([a2[i], r2[i].astype(BF16)], axis=0),
                    jnp.concatenate([b2[i], k2[i]], axis=0)) for i in idx]
    n_mat = [jnp.where(strict[i], prod[i][:PAIR, :PAIR], 0.0) for i in idx]
    m_ak = [jnp.where(strict[i], prod[i][:PAIR, PAIR:], 0.0).astype(BF16) for i in idx]
    m_rb = [jnp.where(incl[i], prod[i][PAIR:, :PAIR], 0.0).astype(BF16) for i in idx]
    m_rk = [jnp.where(incl[i], prod[i][PAIR:, PAIR:], 0.0).astype(BF16) for i in idx]
    mv = [_dot(jnp.concatenate([m_ak[i], m_rk[i]], axis=0), v2[i]) for i in idx]

    x = [jnp.where(eye, 1.0, 0.0) - jnp.where(levels[i][0], n_mat[i], 0.0) for i in idx]
    n_bf = [n_mat[i].astype(BF16) for i in idx]
    zero = jnp.zeros((PAIR, PAIR), BF16)
    for lv in range(1, len(levels[0])):
        s = 2 ** lv
        if s < SUBLANES:
            cx = [_dot(jnp.where(levels[i][lv], n_bf[i], zero), x[i]) for i in idx]
            x = [x[i] - _dot(x[i], cx[i]) for i in idx]
            continue
        blocks = [(r, r + s) for r in range(0, PAIR, s)]
        upd = [[((r % CHUNK) // s) % 2 == (0 if reverse[i] else 1) for r, _ in blocks] for i in idx]

        def take(mat, i):
            return jnp.concatenate([mat[r0:r1] for (r0, r1), u in zip(blocks, upd[i]) if u], axis=0)

        c_h = [take(jnp.where(levels[i][lv], n_mat[i], 0.0), i) for i in idx]
        cx_h = [_dot(c_h[i], x[i]) for i in idx]
        zrows = jnp.zeros((s, PAIR), F32)
        cx = []
        for i in idx:
            it = iter(range(CHUNK // s))
            cx.append(jnp.concatenate(
                [cx_h[i][k * s:(k + 1) * s] if u else zrows
                 for u in upd[i] for k in ([next(it)] if u else [0])], axis=0))
        du = [_dot(take(x[i], i), cx[i]) for i in idx]
        x_new = []
        for i in idx:
            it = iter(range(CHUNK // s))
            x_new.append(jnp.concatenate(
                [x[i][r0:r1] - du[i][k * s:(k + 1) * s] if u else x[i][r0:r1]
                 for (r0, r1), u in zip(blocks, upd[i]) for k in ([next(it)] if u else [0])], axis=0))
        x = x_new

    tw = [_dot(x[i], jnp.concatenate([a2[i], mv[i][:PAIR].astype(BF16)], axis=1))
          for i in idx]
    tw_bf = [t.astype(BF16) for t in tw]
    qo = [jnp.concatenate([r2[i], mv[i][PAIR:]], axis=1) - _dot(m_rb[i], tw_bf[i])
          for i in idx]
    bt = [_dot_tn(bh2[i], tw_bf[i]) for i in idx]
    kv = [_dot_tn(kh2[i], v2[i]) for i in idx]
    out = []
    for i in idx:
        g_mat = jnp.where(eye, g_last[i], 0.0) - bt[i][:, :PAIR]
        h_mat = kv[i] - bt[i][:, PAIR:]
        q = qo[i][:CHUNK] + qo[i][CHUNK:]
        out.append((q[:, :PAIR], q[:, PAIR:], g_mat, h_mat))
    return out


def _rwkv_a_kernel(z_ref, zp_ref, zn_ref, mu_ref, w0_ref, wuph_ref, wupl_ref, a0_ref, auph_ref,
                   aupl_ref, gup_ref, kk_ref, ka_ref, rk_ref, hsum_ref, trif_ref, trib_ref,
                   q_out, ol_out, g_out, h_out, bonus_out, gate_out,
                   at_s, bt_s, kt_s, rt_s, bh_s, kh_s, v_s, gl_s, *, tm, seq):
    i = pl.program_id(0)
    z = z_ref[...]
    has_prev = (i * tm) % seq != 0
    has_next = ((i + 1) * tm) % seq != 0
    prev_row = jnp.where(has_prev, zp_ref[SUBLANES - 1:SUBLANES, :], 0.0)
    next_row = jnp.where(has_next, zn_ref[0:1, :], 0.0)
    sub = lax.broadcasted_iota(I32, (SUBLANES, 1), 0)
    z_dn = pltpu.roll(z, 1, axis=0)
    z_dn = jnp.concatenate([jnp.where(sub == 0, prev_row, z_dn[:SUBLANES]), z_dn[SUBLANES:]], axis=0)
    z_up = pltpu.roll(z, tm - 1, axis=0)
    z_up = jnp.concatenate([z_up[:tm - SUBLANES],
                            jnp.where(sub == SUBLANES - 1, next_row, z_up[tm - SUBLANES:])], axis=0)
    zs = z + mu_ref[...] * (0.5 * (z_dn + z_up) - z)

    w = RW_WIDTH
    r = zs[:, :w]
    k = zs[:, w:2 * w]
    v = zs[:, 2 * w:3 * w]
    wd = zs[:, 3 * w:3 * w + 128]
    ad = zs[:, 3 * w + 128:3 * w + 256]
    gd = zs[:, 3 * w + 256:3 * w + 384]

    hsum = hsum_ref[...]
    w_logit = w0_ref[...] + _dot_3pass(jnp.tanh(wd), wuph_ref[...], wupl_ref[...])
    lw = -_sigmoid(w_logit) * jnp.exp(jnp.float32(-0.5))
    a = _sigmoid(a0_ref[...] + _dot_3pass(ad, auph_ref[...], aupl_ref[...]))
    gate_out[...] = _dot(_sigmoid(gd), gup_ref[...])
    kkr = k * kk_ref[...]
    kkn = kkr * jnp.minimum(lax.rsqrt(_dot(kkr * kkr, hsum)), 1e12)
    kd = [k * (1.0 + (a[:, d * w:(d + 1) * w] - 1.0) * ka_ref[...]) for d in range(2)]
    bonus_out[...] = _dot(r * (0.5 * (kd[0] + kd[1])) * rk_ref[...], hsum) * v
    v_s[...] = v.astype(BF16)

    for d, tri_ref in enumerate((trif_ref, trib_ref)):
        cols = slice(d * w, (d + 1) * w)
        lw_d = lw[:, cols]
        cum = _dot_rsplit(tri_ref[...], lw_d, 3)
        ends = [c * CHUNK if d == 1 else (c + 1) * CHUNK - 1 for c in range(tm // CHUNK)]
        tot = jnp.concatenate([jnp.broadcast_to(cum[e:e + 1, :], (CHUNK, w)) for e in ends], axis=0)
        g_inv = jnp.exp(-cum)
        g_end = jnp.exp(tot - cum)
        b = kkn * a[:, cols]
        at_s[d] = (kkn * jnp.exp(cum - lw_d)).astype(BF16)
        bt_s[d] = (b * g_inv).astype(BF16)
        kt_s[d] = (kd[d] * g_inv).astype(BF16)
        rt_s[d] = r * jnp.exp(cum)
        bh_s[d] = (b * g_end).astype(BF16)
        kh_s[d] = (kd[d] * g_end).astype(BF16)
        gl_s[d] = jnp.exp(tot)

    same, li, lj, eye = _pair_masks()
    lane = lax.broadcasted_iota(I32, (1, PAIR), 1)
    m0 = lane < HEAD_DIM
    m1 = lane >= HEAD_DIM
    consts = []
    for reverse in (False, True):
        before = (lj > li) if reverse else (lj < li)
        strict = same & before
        incl = same & (before | (li == lj))
        levels = []
        s = 1
        while s < CHUNK:
            blk = same & ((li // (2 * s)) == (lj // (2 * s)))
            hi_row = (li // s) % 2 == 1
            hi_col = (lj // s) % 2 == 1
            levels.append(blk & ((~hi_row & hi_col) if reverse else (hi_row & ~hi_col)))
            s *= 2
        consts.append((strict, incl, levels, eye, m0, m1, reverse))

    def chunk_body(it, carry):
        inst = []
        for j in range(CHUNKS_PER_ITER):
            c = it * CHUNKS_PER_ITER + j
            r0 = pl.multiple_of(c * CHUNK, CHUNK)
            inst += [(c, r0, pl.ds(r0, CHUNK), d, slice(p * PAIR, (p + 1) * PAIR))
                     for d in range(2) for p in range(N_PAIRS)]
        outs = _chunk_pair(
            [at_s[d, rows, ln] for c, r0, rows, d, ln in inst],
            [bt_s[d, rows, ln] for c, r0, rows, d, ln in inst],
            [kt_s[d, rows, ln] for c, r0, rows, d, ln in inst],
            [rt_s[d, rows, ln] for c, r0, rows, d, ln in inst],
            [v_s[rows, ln] for c, r0, rows, d, ln in inst],
            [bh_s[d, rows, ln] for c, r0, rows, d, ln in inst],
            [kh_s[d, rows, ln] for c, r0, rows, d, ln in inst],
            [gl_s[d, pl.ds(r0, 1), ln] for c, r0, rows, d, ln in inst],
            [consts[d] for c, r0, rows, d, ln in inst])
        for (c, r0, rows, d, ln), (qe, ol, g_mat, h_mat) in zip(inst, outs):
            q_out[d, rows, ln] = qe.astype(BF16)
            ol_out[d, rows, ln] = ol.astype(BF16)
            g_out[d, c, :, ln] = g_mat.astype(BF16)
            h_out[d, c, :, ln] = h_mat.astype(BF16)
        return carry

    lax.fori_loop(0, tm // (CHUNK * CHUNKS_PER_ITER), chunk_body, 0)


def _rwkv_a(z2d, seq, mu, w0, wup, a0, aup, gup, k_k, k_a, r_k, hsum):
    m = z2d.shape[0]
    tm = min(256, seq)
    nc = tm // CHUNK
    w = RW_WIDTH
    full = lambda shape: pl.BlockSpec(shape, lambda i: (0,) * len(shape))
    last8 = m // 8 - 1
    ti = jnp.arange(tm)[:, None]
    tj = jnp.arange(tm)[None, :]
    same_chunk = (ti // CHUNK) == (tj // CHUNK)
    tri_f = (same_chunk & (tj <= ti)).astype(BF16)
    tri_b = (same_chunk & (tj >= ti)).astype(BF16)
    wup_hi, wup_lo = _hi_lo(wup)
    aup_hi, aup_lo = _hi_lo(aup)
    kern = functools.partial(_rwkv_a_kernel, tm=tm, seq=seq)
    return pl.pallas_call(
        kern,
        grid=(m // tm,),
        in_specs=[
            pl.BlockSpec((tm, RW_COLS), lambda i: (i, 0)),
            pl.BlockSpec((8, RW_COLS), lambda i: (jnp.maximum(i * (tm // 8) - 1, 0), 0)),
            pl.BlockSpec((8, RW_COLS), lambda i: (jnp.minimum((i + 1) * (tm // 8), last8), 0)),
            full((1, RW_COLS)), full((1, 2 * w)), full((128, 2 * w)), full((128, 2 * w)),
            full((1, 2 * w)), full((128, 2 * w)), full((128, 2 * w)), full((128, w)),
            full((1, w)), full((1, w)), full((1, w)), full((w, w)), full((tm, tm)), full((tm, tm)),
        ],
        out_specs=[
            pl.BlockSpec((2, tm, w), lambda i: (0, i, 0)),
            pl.BlockSpec((2, tm, w), lambda i: (0, i, 0)),
            pl.BlockSpec((2, nc, PAIR, w), lambda i: (0, i, 0, 0)),
            pl.BlockSpec((2, nc, PAIR, w), lambda i: (0, i, 0, 0)),
            pl.BlockSpec((tm, w), lambda i: (i, 0)),
            pl.BlockSpec((tm, w), lambda i: (i, 0)),
        ],
        out_shape=[
            jax.ShapeDtypeStruct((2, m, w), BF16),
            jax.ShapeDtypeStruct((2, m, w), BF16),
            jax.ShapeDtypeStruct((2, m // CHUNK, PAIR, w), BF16),
            jax.ShapeDtypeStruct((2, m // CHUNK, PAIR, w), BF16),
            jax.ShapeDtypeStruct((m, w), F32),
            jax.ShapeDtypeStruct((m, w), F32),
        ],
        scratch_shapes=[pltpu.VMEM((2, tm, w), BF16)] * 3 + [pltpu.VMEM((2, tm, w), F32)]
        + [pltpu.VMEM((2, tm, w), BF16)] * 2 + [pltpu.VMEM((tm, w), BF16), pltpu.VMEM((2, tm, w), F32)],
        compiler_params=_cparams(("parallel",)),
        name="rwkv_chunk_local",
    )(z2d, z2d, z2d, mu, w0, wup_hi, wup_lo, a0, aup_hi, aup_lo, gup.astype(BF16), k_k, k_a, r_k,
      hsum, tri_f, tri_b)


def _rwkv_b_kernel(qf_ref, olf_ref, gf_ref, hf_ref, qb_ref, olb_ref, gb_ref, hb_ref,
                   of_ref, ob_ref, s_ref, *, cb):
    @pl.when(pl.program_id(1) == 0)
    def _():
        s_ref[...] = jnp.zeros_like(s_ref)

    for step in range(cb):
        inst = []
        for d, refs in enumerate(((qf_ref, olf_ref, gf_ref, hf_ref, of_ref),
                                  (qb_ref, olb_ref, gb_ref, hb_ref, ob_ref))):
            c = cb - 1 - step if d == 1 else step
            for p in range(N_PAIRS):
                inst.append((d, c, slice(c * CHUNK, (c + 1) * CHUNK),
                             slice(p * PAIR, (p + 1) * PAIR)) + refs)
        s_bf = [s_ref[d, :, ln].astype(BF16) for d, c, rows, ln, *_ in inst]
        s_new = [jnp.dot(g_ref[c, :, ln], sb, preferred_element_type=F32) + h_ref[c, :, ln]
                 for (d, c, rows, ln, q_ref, ol_ref, g_ref, h_ref, o_ref), sb in zip(inst, s_bf)]
        o_val = [jnp.dot(q_ref[rows, ln], sb, preferred_element_type=F32) + ol_ref[rows, ln]
                 for (d, c, rows, ln, q_ref, ol_ref, g_ref, h_ref, o_ref), sb in zip(inst, s_bf)]
        for (d, c, rows, ln, q_ref, ol_ref, g_ref, h_ref, o_ref), sn, ov in zip(inst, s_new, o_val):
            s_ref[d, :, ln] = sn
            o_ref[rows, ln] = ov


def _rwkv_b(qeff, oloc, g_all, h_all, batch, seq):
    m = batch * seq
    w = RW_WIDTH
    cb = min(4, seq // CHUNK)
    tm = cb * CHUNK
    nb = seq // tm

    def fwd(b, j):
        return b * nb + j

    def bwd(b, j):
        return b * nb + nb - 1 - j

    def specs(d, blk):
        return [
            pl.BlockSpec((None, tm, w), lambda b, j: (d, blk(b, j), 0)),
            pl.BlockSpec((None, tm, w), lambda b, j: (d, blk(b, j), 0)),
            pl.BlockSpec((None, cb, PAIR, w), lambda b, j: (d, blk(b, j), 0, 0)),
            pl.BlockSpec((None, cb, PAIR, w), lambda b, j: (d, blk(b, j), 0, 0)),
        ]

    return pl.pallas_call(
        functools.partial(_rwkv_b_kernel, cb=cb),
        grid=(batch, nb),
        in_specs=specs(0, fwd) + specs(1, bwd),
        out_specs=[pl.BlockSpec((tm, w), lambda b, j: (fwd(b, j), 0)),
                   pl.BlockSpec((tm, w), lambda b, j: (bwd(b, j), 0))],
        out_shape=[jax.ShapeDtypeStruct((m, w), F32)] * 2,
        scratch_shapes=[pltpu.VMEM((2, PAIR, w), F32)],
        compiler_params=_cparams(("parallel", "arbitrary")),
        name="rwkv_recurrence",
    )(qeff, oloc, g_all, h_all, qeff, oloc, g_all, h_all)


def _mla_prep_kernel(z_ref, c_ref, s_ref, qn_ref, kvn_ref, qa_ref, qb_ref, kk_ref, kvv_ref, hsel_ref,
                     q_out, k_out, v_out, qmax_out, kmax_out, *, scale):
    z = z_ref[...]
    cos = c_ref[...]
    sin = s_ref[...]
    qd = _rms(z[:, :Q_LORA], qn_ref[...]).astype(BF16)
    kvd = _rms(z[:, Q_LORA:Q_LORA + KV_LORA], kvn_ref[...]).astype(BF16)
    o = Q_LORA + KV_LORA
    kr = z[:, o:o + LANES] * cos + z[:, o + LANES:o + 2 * LANES] * sin
    qa = jnp.dot(qd, qa_ref[...], preferred_element_type=F32)
    qb = jnp.dot(qd, qb_ref[...], preferred_element_type=F32)
    kn = jnp.dot(kvd, kk_ref[...], preferred_element_type=F32)
    v_out[...] = lax.dot_general(kvv_ref[...], kvd, (((1,), (1,)), ((), ())),
                                 preferred_element_type=F32).astype(BF16)
    qs, ks = [], []
    for h in range(MLA_HEADS):
        lanes = slice(h * LANES, (h + 1) * LANES)
        qs.append(((qa[:, lanes] * cos + qb[:, lanes] * sin) * scale).astype(BF16))
        ks.append((kn[:, lanes] + kr).astype(BF16))
        q_out[h] = qs[h]
        k_out[h] = ks[h]
    for vals, out in ((qs, qmax_out), (ks, kmax_out)):
        full = jnp.concatenate([v.astype(F32) for v in vals], axis=1)
        n2 = jnp.dot((full * full).astype(BF16), hsel_ref[...], preferred_element_type=F32)
        out[...] = jnp.broadcast_to(jnp.max(n2, axis=0, keepdims=True), (SUBLANES, LANES))


def _mla_prep(z2d, cos, sin, batch, seq, q_norm, kv_norm, q_a, q_b, kv_k, kv_v):
    tm = min(512, seq)
    nt = seq // tm
    hw = MLA_HEADS * LANES
    scale = float((QK_NOPE + QK_ROPE) ** -0.5 * math.log2(math.e))
    full = lambda shape: pl.BlockSpec(shape, lambda b, i: (0,) * len(shape))
    col_blk = RW_COLS // MLA_IN
    assert col_blk * MLA_IN == RW_COLS
    head_sel = (jnp.arange(hw)[:, None] // LANES == jnp.arange(LANES)[None, :]).astype(BF16)
    q, k, v_t, qmax, kmax = pl.pallas_call(
        functools.partial(_mla_prep_kernel, scale=scale),
        grid=(batch, nt),
        in_specs=[
            pl.BlockSpec((tm, MLA_IN), lambda b, i: (b * nt + i, col_blk)),
            pl.BlockSpec((tm, LANES), lambda b, i: (b * nt + i, 0)),
            pl.BlockSpec((tm, LANES), lambda b, i: (b * nt + i, 0)),
            full((1, Q_LORA)), full((1, KV_LORA)), full((Q_LORA, hw)), full((Q_LORA, hw)),
            full((KV_LORA, hw)), full((MLA_WIDTH, KV_LORA)), full((hw, LANES)),
        ],
        out_specs=[
            pl.BlockSpec((None, MLA_HEADS, tm, LANES), lambda b, i: (b, 0, i, 0)),
            pl.BlockSpec((None, MLA_HEADS, tm, LANES), lambda b, i: (b, 0, i, 0)),
            pl.BlockSpec((None, MLA_WIDTH, tm), lambda b, i: (b, 0, i)),
            pl.BlockSpec((None, None, SUBLANES, LANES), lambda b, i: (b, i, 0, 0)),
            pl.BlockSpec((None, None, SUBLANES, LANES), lambda b, i: (b, i, 0, 0)),
        ],
        out_shape=[
            jax.ShapeDtypeStruct((batch, MLA_HEADS, seq, LANES), BF16),
            jax.ShapeDtypeStruct((batch, MLA_HEADS, seq, LANES), BF16),
            jax.ShapeDtypeStruct((batch, MLA_WIDTH, seq), BF16),
            jax.ShapeDtypeStruct((batch, nt, SUBLANES, LANES), F32),
            jax.ShapeDtypeStruct((batch, nt, SUBLANES, LANES), F32),
        ],
        compiler_params=_cparams(("parallel", "parallel")),
        name="mla_prep",
    )(z2d, cos, sin, q_norm, kv_norm, q_a, q_b, kv_k, kv_v, head_sel)
    bound = jnp.sqrt(jnp.max(qmax[:, :, 0, :MLA_HEADS], axis=1)
                     * jnp.max(kmax[:, :, 0, :MLA_HEADS], axis=1)) * NORM_MARGIN
    small = (bound <= FLASH_SAFE_LOG2).reshape(batch, MLA_HEADS // 2, 2).all(axis=-1)
    return q, k, v_t, small.astype(I32).reshape(-1)


def _flash_kernel(small_ref, q_ref, k_ref, vt_ref, o_ref, m_ref, l_ref, acc_ref):
    j = pl.program_id(3)
    small = small_ref[pl.program_id(0) * pl.num_programs(1) + pl.program_id(1)] != 0

    @pl.when(j == 0)
    def _():
        m_ref[...] = jnp.full_like(m_ref, -jnp.inf)
        l_ref[...] = jnp.zeros_like(l_ref)
        acc_ref[...] = jnp.zeros_like(acc_ref)

    tk = k_ref.shape[1]
    sub = min(FLASH_SUB, tk)
    inst = [(h, slice(b * sub, (b + 1) * sub)) for b in range(tk // sub) for h in range(2)]
    rows = [slice(h * V_HEAD, (h + 1) * V_HEAD) for h in range(2)]
    ones = jnp.ones((ONES_ROWS, sub), BF16)

    def scores():
        return [lax.dot_general(k_ref[h, kb, :], q_ref[h], (((1,), (1,)), ((), ())),
                                preferred_element_type=F32) for h, kb in inst]

    def weighted_values(p):
        pv = [jnp.dot(jnp.concatenate([vt_ref[rows[h], kb], ones], axis=0), x,
                      preferred_element_type=F32) for (h, kb), x in zip(inst, p)]
        return [x[:V_HEAD, :] for x in pv], [x[V_HEAD:V_HEAD + 1, :] for x in pv]

    @pl.when(small)
    def _():
        pv, l_loc = weighted_values([jnp.exp2(x).astype(BF16) for x in scores()])
        for h in range(2):
            mine = [i for i, (hh, _) in enumerate(inst) if hh == h]
            l_ref[h:h + 1, :] = l_ref[h:h + 1, :] + sum(l_loc[i] for i in mine)
            acc_ref[rows[h], :] = acc_ref[rows[h], :] + sum(pv[i] for i in mine)

    @pl.when(jnp.logical_not(small))
    def _():
        s = scores()
        m_loc = [jnp.max(x, axis=0, keepdims=True) for x in s]
        pv, l_loc = weighted_values([jnp.exp2(x - m).astype(BF16) for x, m in zip(s, m_loc)])
        for h in range(2):
            mine = [i for i, (hh, _) in enumerate(inst) if hh == h]
            m_prev = m_ref[h:h + 1, :]
            m_new = m_prev
            for i in mine:
                m_new = jnp.maximum(m_new, m_loc[i])
            alpha = jnp.exp2(m_prev - m_new)
            l_new = alpha * l_ref[h:h + 1, :]
            acc = alpha * acc_ref[rows[h], :]
            for i in mine:
                w = jnp.exp2(m_loc[i] - m_new)
                l_new = l_new + w * l_loc[i]
                acc = acc + w * pv[i]
            m_ref[h:h + 1, :] = m_new
            l_ref[h:h + 1, :] = l_new
            acc_ref[rows[h], :] = acc

    @pl.when(j == pl.num_programs(3) - 1)
    def _():
        inv = 1.0 / l_ref[...]
        o_t = jnp.concatenate([acc_ref[:V_HEAD, :] * inv[0:1, :], acc_ref[V_HEAD:, :] * inv[1:2, :]],
                              axis=0)
        o_ref[...] = o_t.T


def _flash(small, q, k, v_t):
    batch, heads, seq, _ = q.shape
    tq = min(FLASH_TQ, seq)
    tk = min(FLASH_TK, seq)
    return pl.pallas_call(
        _flash_kernel,
        grid_spec=pltpu.PrefetchScalarGridSpec(
            num_scalar_prefetch=1,
            grid=(batch, heads // 2, seq // tq, seq // tk),
            in_specs=[
                pl.BlockSpec((None, 2, tq, LANES), lambda b, p, i, j, sm: (b, p, i, 0)),
                pl.BlockSpec((None, 2, tk, LANES), lambda b, p, i, j, sm: (b, p, j, 0)),
                pl.BlockSpec((None, 2 * V_HEAD, tk), lambda b, p, i, j, sm: (b, p, j)),
            ],
            out_specs=pl.BlockSpec((None, tq, LANES), lambda b, p, i, j, sm: (b, i, p)),
            scratch_shapes=[pltpu.VMEM((2, tq), F32), pltpu.VMEM((2, tq), F32),
                            pltpu.VMEM((2 * V_HEAD, tq), F32)],
        ),
        out_shape=jax.ShapeDtypeStruct((batch, seq, MLA_WIDTH), F32),
        compiler_params=_cparams(("parallel", "parallel", "parallel", "arbitrary")),
        name="mla_flash",
    )(small, q, k, v_t)


def _out_kernel(of_ref, ob_ref, bonus_ref, gate_ref, ym_ref, x_ref, hsum_ref, lw_ref, lb_ref,
                on_ref, w_ref, o_ref, ot_ref):
    o = of_ref[...] + ob_ref[...]
    hsum = hsum_ref[...]
    inv_n = 1.0 / HEAD_DIM
    mean = _dot_lsplit(o, hsum, 2) * inv_n
    d = o - mean
    var = _dot_lsplit(d * d, hsum, 2) * inv_n
    y_rw = (d * lax.rsqrt(var + GN_EPS) * lw_ref[...] + lb_ref[...] + bonus_ref[...]) * gate_ref[...]
    y_mla = _rms(ym_ref[...], on_ref[...])
    w = RW_WIDTH
    out = x_ref[...] + _dot(y_rw, w_ref[:w, :]) + _dot(y_mla, w_ref[w:, :])
    o_ref[...] = out
    _rows_to_tiles(ot_ref, out)


def _out_proj(o_f, o_b, bonus, gate, y_mla, x2d, hsum, lnx_w, lnx_b, o_norm, w_out):
    m, d = x2d.shape
    w = RW_WIDTH
    tm = min(256, m)
    row = lambda n: pl.BlockSpec((tm, n), lambda i: (i, 0))
    full = lambda shape: pl.BlockSpec(shape, lambda i: (0,) * len(shape))
    return pl.pallas_call(
        _out_kernel,
        grid=(m // tm,),
        in_specs=[row(w), row(w), row(w), row(w), row(MLA_WIDTH), row(d), full((w, w)),
                  full((1, w)), full((1, w)), full((1, MLA_WIDTH)), full((w + MLA_WIDTH, d))],
        out_specs=[row(d), pl.BlockSpec((tm * SUBLANES, LANES), lambda i: (i, 0))],
        out_shape=[jax.ShapeDtypeStruct((m, d), F32),
                   jax.ShapeDtypeStruct((m * SUBLANES, LANES), F32)],
        compiler_params=_cparams(("parallel",)),
        name="out_proj",
    )(o_f, o_b, bonus, gate, y_mla, x2d, hsum, lnx_w, lnx_b, o_norm, w_out)


def _router_kernel(x_ref, g_ref, rt_ref, a_ref):
    xn = _rms(x_ref[...], g_ref[...])
    logits = _dot32_nt(rt_ref[...], xn)
    mx = jnp.max(logits, axis=0, keepdims=True)
    e = jnp.exp(logits - mx)
    a_ref[...] = e / jnp.sum(e, axis=0, keepdims=True)


def _router(x3d, g, router_t):
    batch, seq, d = x3d.shape
    e = router_t.shape[0]
    tm = min(512, seq)
    return pl.pallas_call(
        _router_kernel,
        grid=(batch, seq // tm),
        in_specs=[pl.BlockSpec((None, tm, d), lambda b, i: (b, i, 0)),
                  pl.BlockSpec((1, d), lambda b, i: (0, 0)),
                  pl.BlockSpec((e, d), lambda b, i: (0, 0))],
        out_specs=pl.BlockSpec((None, e, tm), lambda b, i: (b, 0, i)),
        out_shape=jax.ShapeDtypeStruct((batch, e, seq), F32),
        compiler_params=_cparams(("parallel", "parallel")),
        name="moe_router",
    )(x3d, g, router_t)


def _threshold_kernel(a_ref, thr_ref, *, cap):
    bits = pltpu.bitcast(a_ref[...], I32)
    n_e = bits.shape[0]

    def search(i, cur):
        cand = cur | jnp.left_shift(jnp.int32(1), 30 - i)
        cnt = jnp.sum(jnp.where(bits >= cand, 1, 0), axis=1, keepdims=True)
        return jnp.where(cnt >= cap, cand, cur)

    thr_ref[...] = lax.fori_loop(0, 31, search, jnp.zeros((n_e, 1), I32))


def _compact_kernel(a_ref, thr_ref, idx_ref, gate_ref, *, cap):
    a = a_ref[...]
    nb = a.shape[0]
    bits = pltpu.bitcast(a, I32)
    thr = thr_ref[...]
    ri = lax.broadcasted_iota(I32, (LANES, LANES), 0)
    ci = lax.broadcasted_iota(I32, (LANES, LANES), 1)
    upper = jnp.where(ri <= ci, 1.0, 0.0).astype(BF16)
    bi = lax.broadcasted_iota(I32, (nb, nb), 0)
    bj = lax.broadcasted_iota(I32, (nb, nb), 1)
    before = jnp.where(bj < bi, 1.0, 0.0).astype(BF16)

    def total(x):
        return jnp.sum(jnp.sum(x, axis=1, keepdims=True), axis=0, keepdims=True)

    def running(mask):
        within = jnp.dot(mask, upper, preferred_element_type=F32)
        tot = jnp.broadcast_to(within[:, LANES - 1:LANES], (nb, LANES))
        return within, jnp.dot(before, tot.astype(BF16), preferred_element_type=F32)

    gt = bits > thr
    eq = bits == thr
    need = cap - total(jnp.where(gt, 1.0, 0.0))
    w_eq, b_eq = running(jnp.where(eq, 1.0, 0.0).astype(BF16))
    sel = gt | (eq & (w_eq + b_eq <= need))
    sel_b = jnp.where(sel, 1.0, 0.0).astype(BF16)
    within, base = running(sel_b)

    tot_row = lax.dot_general(jnp.ones((SUBLANES, LANES), BF16), sel_b, (((1,), (1,)), ((), ())),
                              preferred_element_type=F32)
    base_row = jnp.dot(tot_row.astype(BF16), jnp.where(bi < bj, 1.0, 0.0).astype(BF16),
                       preferred_element_type=F32)
    c_col = lax.broadcasted_iota(I32, (cap, 1), 0).astype(F32)
    in_block = (base_row[0:1, :] <= c_col) & (c_col < base_row[0:1, :] + tot_row[0:1, :])
    onehot = jnp.where(in_block, 1.0, 0.0).astype(BF16)

    a_parts = _split_bf16(a, 3)
    base_hi = jnp.floor(base * (1.0 / 32.0))
    lane = lax.broadcasted_iota(I32, (1, LANES), 1)
    block_id = lax.broadcasted_iota(I32, (nb, LANES), 0).astype(F32)
    small = jnp.where(lane == 0, base_hi, jnp.where(lane == 1, base - 32.0 * base_hi,
                                                     jnp.where(lane == 2, block_id, 0.0)))
    table = jnp.concatenate([within.astype(BF16), sel_b] + a_parts + [small.astype(BF16)], axis=1)
    g = jnp.dot(onehot, table, preferred_element_type=F32)
    w_c = g[:, :LANES]
    s_c = g[:, LANES:2 * LANES]
    a_c = g[:, 2 * LANES:3 * LANES] + g[:, 3 * LANES:4 * LANES] + g[:, 4 * LANES:5 * LANES]
    sm = g[:, 5 * LANES:]
    target = c_col - (32.0 * sm[:, 0:1] + sm[:, 1:2]) + 1.0
    match = (w_c == target) & (s_c > 0.5)
    pos = jnp.sum(jnp.where(match, lane.astype(F32), 0.0), axis=1, keepdims=True)
    gate_ref[...] = jnp.sum(jnp.where(match, a_c, 0.0), axis=1, keepdims=True)
    idx_ref[...] = (LANES * sm[:, 2:3] + pos).astype(I32)


def _select(aff_t, cap):
    batch, e, seq = aff_t.shape
    nb = seq // LANES
    thr = pl.pallas_call(
        functools.partial(_threshold_kernel, cap=cap),
        grid=(batch,),
        in_specs=[pl.BlockSpec((None, e, seq), lambda b: (b, 0, 0))],
        out_specs=pl.BlockSpec((None, e, 1), lambda b: (b, 0, 0)),
        out_shape=jax.ShapeDtypeStruct((batch, e, 1), I32),
        compiler_params=_cparams(("parallel",)),
        name="moe_threshold",
    )(aff_t)
    idx, gate = pl.pallas_call(
        functools.partial(_compact_kernel, cap=cap),
        grid=(batch, e),
        in_specs=[pl.BlockSpec((None, None, nb, LANES), lambda b, j: (b, j, 0, 0)),
                  pl.BlockSpec((None, None, 1, 1), lambda b, j: (b, j, 0, 0))],
        out_specs=[pl.BlockSpec((None, None, cap, 1), lambda b, j: (b, j, 0, 0))] * 2,
        out_shape=[jax.ShapeDtypeStruct((batch, e, cap, 1), I32),
                   jax.ShapeDtypeStruct((batch, e, cap, 1), F32)],
        compiler_params=_cparams(("parallel", "parallel")),
        name="moe_compact",
    )(aff_t.reshape(batch, e, nb, LANES), thr.reshape(batch, e, 1, 1))
    return idx.reshape(-1), gate


def _rows_from_tiles(ref):
    n = ref.shape[0] // SUBLANES
    return jnp.concatenate([ref[pl.ds(s, n, stride=SUBLANES), :] for s in range(SUBLANES)], axis=-1)


def _rows_to_tiles(ref, val):
    n = val.shape[0]
    for s in range(SUBLANES):
        ref[pl.ds(s, n, stride=SUBLANES), :] = val[:, s * LANES:(s + 1) * LANES]


def _tile(r):
    return pl.ds(pl.multiple_of(r * SUBLANES, SUBLANES), SUBLANES)


def _gather_kernel(idx_ref, x_ref, o_ref, *, cap, n_e):
    base = (pl.program_id(0) * n_e + pl.program_id(1)) * cap

    def body(c, carry):
        o_ref[_tile(c), :] = x_ref[_tile(idx_ref[base + c]), :]
        return carry

    lax.fori_loop(0, cap, body, 0, unroll=8)


def _gather(idx_flat, x_tiles, n_e, cap):
    batch, rows, _ = x_tiles.shape
    return pl.pallas_call(
        functools.partial(_gather_kernel, cap=cap, n_e=n_e),
        grid_spec=pltpu.PrefetchScalarGridSpec(
            num_scalar_prefetch=1,
            grid=(batch, n_e),
            in_specs=[pl.BlockSpec((None, rows, LANES), lambda b, e, idx: (b, 0, 0),
                                   pipeline_mode=pl.Buffered(1))],
            out_specs=pl.BlockSpec((None, None, cap * SUBLANES, LANES),
                                   lambda b, e, idx: (b, e, 0, 0)),
        ),
        out_shape=jax.ShapeDtypeStruct((batch, n_e, cap * SUBLANES, LANES), F32),
        compiler_params=_cparams(("parallel", "arbitrary")),
        name="moe_gather",
    )(idx_flat, x_tiles)


def _ffn_kernel(x_ref, gate_ref, g_ref, wg_ref, wu_ref, wd_ref, o_ref, wg_s, wu_s, wd_s):
    @pl.when((pl.program_id(1) == 0) & (pl.program_id(2) == 0))
    def _():
        wg_s[...] = wg_ref[...].astype(BF16)
        wu_s[...] = wu_ref[...].astype(BF16)
        wd_s[...] = wd_ref[...].astype(BF16)

    xn = _rms(_rows_from_tiles(x_ref), g_ref[...]).astype(BF16)
    h1 = jnp.dot(xn, wg_s[...], preferred_element_type=F32)
    h2 = jnp.dot(xn, wu_s[...], preferred_element_type=F32)
    hid = (h1 * _sigmoid(h1) * h2).astype(BF16)
    _rows_to_tiles(o_ref, jnp.dot(hid, wd_s[...], preferred_element_type=F32) * gate_ref[...])


def _expert_ffn(xe, gate_col, g, w_gate, w_up, w_down, layer):
    batch, n_e, rows, _ = xe.shape
    cap = rows // SUBLANES
    d, f = w_gate.shape[2:]
    tc = min(512, cap)
    return pl.pallas_call(
        _ffn_kernel,
        grid=(n_e, batch, cap // tc),
        in_specs=[
            pl.BlockSpec((None, None, tc * SUBLANES, LANES), lambda e, b, c: (b, e, c, 0)),
            pl.BlockSpec((None, None, tc, 1), lambda e, b, c: (b, e, c, 0)),
            pl.BlockSpec((1, d), lambda e, b, c: (0, 0)),
            pl.BlockSpec((None, None, d, f), lambda e, b, c: (layer, e, 0, 0)),
            pl.BlockSpec((None, None, d, f), lambda e, b, c: (layer, e, 0, 0)),
            pl.BlockSpec((None, None, f, d), lambda e, b, c: (layer, e, 0, 0)),
        ],
        out_specs=pl.BlockSpec((None, None, tc * SUBLANES, LANES), lambda e, b, c: (b, e, c, 0)),
        out_shape=jax.ShapeDtypeStruct(xe.shape, F32),
        scratch_shapes=[pltpu.VMEM((d, f), BF16), pltpu.VMEM((d, f), BF16), pltpu.VMEM((f, d), BF16)],
        compiler_params=_cparams(("parallel", "arbitrary", "arbitrary")),
        name="moe_ffn",
    )(xe, gate_col, g, w_gate, w_up, w_down)


SCATTER_GROUP = 8


def _scatter_kernel(idx_ref, y_ref, o_ref, *, cap, n_e):
    e = pl.program_id(1)
    base = (pl.program_id(0) * n_e + e) * cap

    @pl.when(e == 0)
    def _():
        o_ref[...] = jnp.zeros_like(o_ref)

    def body(g, carry):
        c0 = g * SCATTER_GROUP
        rows = [idx_ref[base + c0 + i] for i in range(SCATTER_GROUP)]
        new = [o_ref[_tile(r), :] + y_ref[_tile(c0 + i), :] for i, r in enumerate(rows)]
        for r, v in zip(rows, new):
            o_ref[_tile(r), :] = v
        return carry

    lax.fori_loop(0, cap // SCATTER_GROUP, body, 0)


def _scatter_add(idx_flat, ye, seq):
    batch, n_e, rows, _ = ye.shape
    cap = rows // SUBLANES
    return pl.pallas_call(
        functools.partial(_scatter_kernel, cap=cap, n_e=n_e),
        grid_spec=pltpu.PrefetchScalarGridSpec(
            num_scalar_prefetch=1,
            grid=(batch, n_e),
            in_specs=[pl.BlockSpec((None, None, rows, LANES), lambda b, e, idx: (b, e, 0, 0))],
            out_specs=pl.BlockSpec((None, seq * SUBLANES, LANES), lambda b, e, idx: (b, 0, 0),
                                   pipeline_mode=pl.Buffered(1)),
        ),
        out_shape=jax.ShapeDtypeStruct((batch, seq * SUBLANES, LANES), F32),
        compiler_params=_cparams(("parallel", "arbitrary")),
        name="moe_scatter_add",
    )(idx_flat, ye)


def _ple_kernel(x_ref, d_ref, p_ref, g_ref, wp_ref, wg_ref, fg_ref, o_ref, *, final):
    x = x_ref[...] + _rows_from_tiles(d_ref)
    gate = _sigmoid(_dot(_rms(x, g_ref[...]), wg_ref[...]))
    out = x + _dot(p_ref[...], wp_ref[...]) * gate
    if final:
        out = _rms(out, fg_ref[...])
    o_ref[...] = out


def _ple(x2d, delta_tiles, p_all, layer, g, w_proj, w_gate, final_g, final):
    m, d = x2d.shape
    dp = p_all.shape[1]
    tm = min(512, m)
    first = layer * (m // tm)
    full = lambda shape: pl.BlockSpec(shape, lambda i: (0,) * len(shape))
    return pl.pallas_call(
        functools.partial(_ple_kernel, final=final),
        grid=(m // tm,),
        in_specs=[pl.BlockSpec((tm, d), lambda i: (i, 0)),
                  pl.BlockSpec((tm * SUBLANES, LANES), lambda i: (i, 0)),
                  pl.BlockSpec((tm, dp), lambda i: (first + i, 0)),
                  full((1, d)), full((dp, d)), full((d, d)), full((1, d))],
        out_specs=pl.BlockSpec((tm, d), lambda i: (i, 0)),
        out_shape=jax.ShapeDtypeStruct((m, d), F32),
        compiler_params=_cparams(("parallel",)),
        name="ple_final" if final else "ple",
    )(x2d, delta_tiles, p_all, g, w_proj, w_gate, final_g)


def _hi_lo(w):
    hi = w.astype(BF16)
    return hi, (w - hi.astype(F32)).astype(BF16)


def _rot_cols(w):
    half = QK_ROPE // 2
    return jnp.concatenate([-w[..., half:], w[..., :half]], axis=-1)


def _pad_head(nope, rope):
    lead = (nope if nope is not None else rope).shape[:-1]
    n = nope if nope is not None else jnp.zeros(lead + (QK_NOPE,), F32)
    r = rope if rope is not None else jnp.zeros(lead + (QK_ROPE,), F32)
    return jnp.concatenate([n, r, jnp.zeros(lead + (LANES - QK_NOPE - QK_ROPE,), F32)], axis=-1)


def _block_rows(w_pair):
    z = jnp.zeros_like(w_pair[0])
    return jnp.concatenate([jnp.concatenate([w_pair[0], z], axis=1),
                            jnp.concatenate([z, w_pair[1]], axis=1)], axis=0)


def kernel(x, p, positions, attn_norm, w_in, rw_mu, rw_w0, rw_w_up, rw_a0, rw_a_up, rw_g_up, rw_k_k,
           rw_k_a, rw_r_k, rw_lnx_w, rw_lnx_b, mla_q_norm, mla_q_up, mla_kv_norm, mla_kv_up,
           mla_o_norm, w_out, ffn_norm, router, exp_w_gate, exp_w_up, exp_w_down, ple_norm,
           ple_proj, ple_gate, final_norm):
    batch, seq, d = x.shape
    depth = w_in.shape[0]
    m = batch * seq
    cap = EC_FACTOR * seq // N_EXPERTS
    w = RW_WIDTH

    cos, sin = _rope_tables(positions)
    hsum = (jnp.arange(w)[:, None] // HEAD_DIM == jnp.arange(w)[None, :] // HEAD_DIM).astype(BF16)
    x2d = x.reshape(m, d)

    for i in range(depth):
        w_mla = w_in[i][:, RW_COLS:]
        w_kr = w_mla[:, Q_LORA + KV_LORA:]
        w_ext = jnp.concatenate([w_in[i][:, :RW_COLS], w_mla[:, :Q_LORA + KV_LORA],
                                 _pad_head(None, w_kr), _pad_head(None, _rot_cols(w_kr))],
                                axis=1).astype(BF16)
        q_up = mla_q_up[i].reshape(Q_LORA, MLA_HEADS, QK_NOPE + QK_ROPE)
        q_a = _pad_head(q_up[..., :QK_NOPE], q_up[..., QK_NOPE:]).reshape(Q_LORA, -1).astype(BF16)
        q_b = _pad_head(None, _rot_cols(q_up[..., QK_NOPE:])).reshape(Q_LORA, -1).astype(BF16)
        kv_up = mla_kv_up[i].reshape(KV_LORA, MLA_HEADS, QK_NOPE + V_HEAD)
        kv_k = _pad_head(kv_up[..., :QK_NOPE], None).reshape(KV_LORA, -1).astype(BF16)
        kv_v = kv_up[..., QK_NOPE:].reshape(KV_LORA, MLA_WIDTH).T.astype(BF16)

        z2d = _in_proj(x2d, attn_norm[i][None, :], w_ext)
        qeff, oloc, g_all, h_all, bonus, gate = _rwkv_a(
            z2d, seq, rw_mu[i][None, :], rw_w0[i].reshape(1, 2 * w), _block_rows(rw_w_up[i]),
            rw_a0[i].reshape(1, 2 * w), _block_rows(rw_a_up[i]), rw_g_up[i], rw_k_k[i][None, :],
            rw_k_a[i][None, :], rw_r_k[i].reshape(1, w), hsum)
        o_f, o_b = _rwkv_b(qeff, oloc, g_all, h_all, batch, seq)
        q, k, v_t, small = _mla_prep(z2d, cos, sin, batch, seq, mla_q_norm[i][None, :],
                                     mla_kv_norm[i][None, :], q_a, q_b, kv_k, kv_v)
        y_mla = _flash(small, q, k, v_t).reshape(m, MLA_WIDTH)
        x2d, x_tiles = _out_proj(o_f, o_b, bonus, gate, y_mla, x2d, hsum, rw_lnx_w[i][None, :],
                                 rw_lnx_b[i][None, :], mla_o_norm[i][None, :],
                                 w_out[i].astype(BF16))

        aff_t = _router(x2d.reshape(batch, seq, d), ffn_norm[i][None, :], router[i].T)
        idx_flat, gates = _select(aff_t, cap)
        xe = _gather(idx_flat, x_tiles.reshape(batch, seq * SUBLANES, LANES), N_EXPERTS, cap)
        ye = _expert_ffn(xe, gates, ffn_norm[i][None, :], exp_w_gate, exp_w_up, exp_w_down, i)
        delta = _scatter_add(idx_flat, ye, seq).reshape(m * SUBLANES, LANES)

        x2d = _ple(x2d, delta, p.reshape(depth * m, -1), i, ple_norm[i][None, :], ple_proj[i].astype(BF16),
                   ple_gate[i].astype(BF16), final_norm[None, :], final=(i == depth - 1))

    return x2d.reshape(batch, seq, d)
```

```python
import functools
import math

import jax
import jax.numpy as jnp
from jax import lax
from jax.experimental import pallas as pl
from jax.experimental.pallas import tpu as pltpu

F32 = jnp.float32
BF16 = jnp.bfloat16
I32 = jnp.int32
HIGHEST = lax.Precision.HIGHEST

RW_HEADS = 8
HEAD_DIM = 64
RW_WIDTH = RW_HEADS * HEAD_DIM
RW_COLS = 3 * RW_WIDTH + 2 * 64 + 2 * 64 + 128
MLA_HEADS = 8
QK_NOPE = 64
QK_ROPE = 32
V_HEAD = 64
Q_LORA = 256
KV_LORA = 128
MLA_WIDTH = MLA_HEADS * V_HEAD
MLA_IN = Q_LORA + KV_LORA + 2 * 128
ROPE_THETA = 10000.0
N_EXPERTS = 16
EC_FACTOR = 2
NORM_EPS = 1e-6
GN_EPS = 64e-5

LANES = 128
SUBLANES = 8
CHUNK = 64
CHUNKS_PER_ITER = 2
PAIR = 2 * HEAD_DIM
N_PAIRS = RW_WIDTH // PAIR
VMEM_LIMIT = 56 * 1024 * 1024
FLASH_TQ = 2048
FLASH_TK = 2048
FLASH_SUB = 256
ONES_ROWS = 16
FLASH_SAFE_LOG2 = 40.0
NORM_MARGIN = 1.05


def _cparams(sem):
    return pltpu.CompilerParams(dimension_semantics=sem, vmem_limit_bytes=VMEM_LIMIT)


def _rms(x, g):
    return x * lax.rsqrt(jnp.mean(x * x, axis=-1, keepdims=True) + NORM_EPS) * g


def _sigmoid(x):
    return 1.0 / (1.0 + jnp.exp(-x))


def _dot(a, b):
    return jnp.dot(a.astype(BF16), b.astype(BF16), preferred_element_type=F32)


def _dot32(a, b):
    return jnp.dot(a, b, preferred_element_type=F32, precision=HIGHEST)


def _split_bf16(x, parts):
    out = []
    rest = x
    for _ in range(parts):
        hi = rest.astype(BF16)
        out.append(hi)
        rest = rest - hi.astype(F32)
    return out


def _dot_lsplit(a, b_exact, parts):
    acc = None
    for term in _split_bf16(a, parts):
        d = jnp.dot(term, b_exact, preferred_element_type=F32)
        acc = d if acc is None else acc + d
    return acc


def _dot_rsplit(a_exact, b, parts):
    acc = None
    for term in _split_bf16(b, parts):
        d = jnp.dot(a_exact, term, preferred_element_type=F32)
        acc = d if acc is None else acc + d
    return acc


def _dot_3pass(a, b_hi, b_lo):
    a_hi, a_lo = _split_bf16(a, 2)
    return (jnp.dot(a_hi, b_hi, preferred_element_type=F32)
            + jnp.dot(a_lo, b_hi, preferred_element_type=F32)
            + jnp.dot(a_hi, b_lo, preferred_element_type=F32))


def _dot32_nt(a, b):
    return lax.dot_general(a, b, (((1,), (1,)), ((), ())), preferred_element_type=F32,
                           precision=HIGHEST)


def _dot32_tn(a, b):
    return lax.dot_general(a, b, (((0,), (0,)), ((), ())), preferred_element_type=F32,
                           precision=HIGHEST)


def _dot_tn(a, b):
    return lax.dot_general(a.astype(BF16), b.astype(BF16), (((0,), (0,)), ((), ())),
                           preferred_element_type=F32)


def _dot_nt(a, b):
    return lax.dot_general(a.astype(BF16), b.astype(BF16), (((1,), (1,)), ((), ())),
                           preferred_element_type=F32)


def _rope_kernel(pos_ref, inv_ref, c_ref, s_ref):
    ang = pos_ref[...] * inv_ref[...]
    c_ref[...] = jnp.cos(ang)
    s_ref[...] = jnp.sin(ang)


def _rope_tables(positions):
    b, t = positions.shape
    m = b * t
    inv = ROPE_THETA ** (-jnp.arange(0, QK_ROPE, 2, dtype=F32) / QK_ROPE)
    inv_row = jnp.concatenate([jnp.zeros((QK_NOPE,), F32), inv, inv,
                               jnp.zeros((LANES - QK_NOPE - QK_ROPE,), F32)])[None, :]
    posf = jnp.broadcast_to(positions.astype(F32).reshape(m, 1), (m, LANES))
    tm = min(1024, m)
    return pl.pallas_call(
        _rope_kernel,
        grid=(m // tm,),
        in_specs=[pl.BlockSpec((tm, LANES), lambda i: (i, 0)),
                  pl.BlockSpec((1, LANES), lambda i: (0, 0))],
        out_specs=[pl.BlockSpec((tm, LANES), lambda i: (i, 0))] * 2,
        out_shape=[jax.ShapeDtypeStruct((m, LANES), F32)] * 2,
        compiler_params=_cparams(("parallel",)),
        name="rope_tables",
    )(posf, inv_row)


def _in_kernel(x_ref, g_ref, w_ref, o_ref):
    h = _rms(x_ref[...], g_ref[...])
    o_ref[...] = jnp.dot(h.astype(BF16), w_ref[...], preferred_element_type=F32)


def _in_proj(x2d, g, w_ext):
    m, d = x2d.shape
    n = w_ext.shape[1]
    tm = min(512, m)
    return pl.pallas_call(
        _in_kernel,
        grid=(m // tm,),
        in_specs=[pl.BlockSpec((tm, d), lambda i: (i, 0)),
                  pl.BlockSpec((1, d), lambda i: (0, 0)),
                  pl.BlockSpec((d, n), lambda i: (0, 0))],
        out_specs=pl.BlockSpec((tm, n), lambda i: (i, 0)),
        out_shape=jax.ShapeDtypeStruct((m, n), F32),
        compiler_params=_cparams(("parallel",)),
        name="in_proj",
    )(x2d, g, w_ext)


def _pair_masks():
    i = lax.broadcasted_iota(I32, (PAIR, PAIR), 0)
    j = lax.broadcasted_iota(I32, (PAIR, PAIR), 1)
    same = (i // CHUNK) == (j // CHUNK)
    li = i % CHUNK
    lj = j % CHUNK
    return same, li, lj, i == j


def _chunk_pair(a_t, b_t, k_t, r_t, v, b_h, k_h, g_last, consts):
    n = len(a_t)
    idx = range(n)
    strict = [c[0] for c in consts]
    incl = [c[1] for c in consts]
    levels = [c[2] for c in consts]
    eye, m0, m1 = consts[0][3:6]
    reverse = [c[6] for c in consts]

    def stack(x):
        return jnp.concatenate([jnp.where(m0, x, jnp.zeros_like(x)),
                                jnp.where(m1, x, jnp.zeros_like(x))], axis=0)

    a2, b2, k2, v2, bh2, kh2, r2 = ([stack(x) for x in xs] for xs in (a_t, b_t, k_t, v, b_h, k_h, r_t))
    prod = [_dot_nt(jnp.concatenate([a2[i], r2[i].astype(BF16)], axis=0),
                    jnp.concatenate([b2[i], k2[i]], axis=0)) for i in idx]
    n_mat = [jnp.where(strict[i], prod[i][:PAIR, :PAIR], 0.0) for i in idx]
    m_ak = [jnp.where(strict[i], prod[i][:PAIR, PAIR:], 0.0).astype(BF16) for i in idx]
    m_rb = [jnp.where(incl[i], prod[i][PAIR:, :PAIR], 0.0).astype(BF16) for i in idx]
    m_rk = [jnp.where(incl[i], prod[i][PAIR:, PAIR:], 0.0).astype(BF16) for i in idx]
    mv = [_dot(jnp.concatenate([m_ak[i], m_rk[i]], axis=0), v2[i]) for i in idx]

    x = [jnp.where(eye, 1.0, 0.0) - jnp.where(levels[i][0], n_mat[i], 0.0) for i in idx]
    n_bf = [n_mat[i].astype(BF16) for i in idx]
    zero = jnp.zeros((PAIR, PAIR), BF16)
    for lv in range(1, len(levels[0])):
        s = 2 ** lv
        if s < SUBLANES:
            cx = [_dot(jnp.where(levels[i][lv], n_bf[i], zero), x[i]) for i in idx]
            x = [x[i] - _dot(x[i], cx[i]) for i in idx]
            continue
        blocks = [(r, r + s) for r in range(0, PAIR, s)]
        upd = [[((r % CHUNK) // s) % 2 == (0 if reverse[i] else 1) for r, _ in blocks] for i in idx]

        def take(mat, i):
            return jnp.concatenate([mat[r0:r1] for (r0, r1), u in zip(blocks, upd[i]) if u], axis=0)

        c_h = [take(jnp.where(levels[i][lv], n_mat[i], 0.0), i) for i in idx]
        cx_h = [_dot(c_h[i], x[i]) for i in idx]
        zrows = jnp.zeros((s, PAIR), F32)
        cx = []
        for i in idx:
            it = iter(range(CHUNK // s))
            cx.append(jnp.concatenate(
                [cx_h[i][k * s:(k + 1) * s] if u else zrows
                 for u in upd[i] for k in ([next(it)] if u else [0])], axis=0))
        du = [_dot(take(x[i], i), cx[i]) for i in idx]
        x_new = []
        for i in idx:
            it = iter(range(CHUNK // s))
            x_new.append(jnp.concatenate(
                [x[i][r0:r1] - du[i][k * s:(k + 1) * s] if u else x[i][r0:r1]
                 for (r0, r1), u in zip(blocks, upd[i]) for k in ([next(it)] if u else [0])], axis=0))
        x = x_new

    tw = [_dot(x[i], jnp.concatenate([a2[i], mv[i][:PAIR].astype(BF16)], axis=1))
          for i in idx]
    tw_bf = [t.astype(BF16) for t in tw]
    qo = [jnp.concatenate([r2[i], mv[i][PAIR:]], axis=1) - _dot(m_rb[i], tw_bf[i])
          for i in idx]
    bt = [_dot_tn(bh2[i], tw_bf[i]) for i in idx]
    kv = [_dot_tn(kh2[i], v2[i]) for i in idx]
    out = []
    for i in idx:
        g_mat = jnp.where(eye, g_last[i], 0.0) - bt[i][:, :PAIR]
        h_mat = kv[i] - bt[i][:, PAIR:]
        q = qo[i][:CHUNK] + qo[i][CHUNK:]
        out.append((q[:, :PAIR], q[:, PAIR:], g_mat, h_mat))
    return out


def _rwkv_a_kernel(z_ref, zp_ref, zn_ref, mu_ref, w0_ref, wuph_ref, wupl_ref, a0_ref, auph_ref,
                   aupl_ref, gup_ref, kk_ref, ka_ref, rk_ref, hsum_ref, trif_ref, trib_ref,
                   q_out, ol_out, g_out, h_out, bonus_out, gate_out,
                   at_s, bt_s, kt_s, rt_s, bh_s, kh_s, v_s, gl_s, *, tm, seq):
    i = pl.program_id(0)
    z = z_ref[...]
    has_prev = (i * tm) % seq != 0
    has_next = ((i + 1) * tm) % seq != 0
    prev_row = jnp.where(has_prev, zp_ref[SUBLANES - 1:SUBLANES, :], 0.0)
    next_row = jnp.where(has_next, zn_ref[0:1, :], 0.0)
    sub = lax.broadcasted_iota(I32, (SUBLANES, 1), 0)
    z_dn = pltpu.roll(z, 1, axis=0)
    z_dn = jnp.concatenate([jnp.where(sub == 0, prev_row, z_dn[:SUBLANES]), z_dn[SUBLANES:]], axis=0)
    z_up = pltpu.roll(z, tm - 1, axis=0)
    z_up = jnp.concatenate([z_up[:tm - SUBLANES],
                            jnp.where(sub == SUBLANES - 1, next_row, z_up[tm - SUBLANES:])], axis=0)
    zs = z + mu_ref[...] * (0.5 * (z_dn + z_up) - z)

    w = RW_WIDTH
    r = zs[:, :w]
    k = zs[:, w:2 * w]
    v = zs[:, 2 * w:3 * w]
    wd = zs[:, 3 * w:3 * w + 128]
    ad = zs[:, 3 * w + 128:3 * w + 256]
    gd = zs[:, 3 * w + 256:3 * w + 384]

    hsum = hsum_ref[...]
    w_logit = w0_ref[...] + _dot_3pass(jnp.tanh(wd), wuph_ref[...], wupl_ref[...])
    lw = -_sigmoid(w_logit) * jnp.exp(jnp.float32(-0.5))
    a = _sigmoid(a0_ref[...] + _dot_3pass(ad, auph_ref[...], aupl_ref[...]))
    gate_out[...] = _dot(_sigmoid(gd), gup_ref[...]).astype(BF16)
    kkr = k * kk_ref[...]
    kkn = kkr * jnp.minimum(lax.rsqrt(_dot(kkr * kkr, hsum)), 1e12)
    kd = [k * (1.0 + (a[:, d * w:(d + 1) * w] - 1.0) * ka_ref[...]) for d in range(2)]
    bonus_out[...] = (_dot(r * (0.5 * (kd[0] + kd[1])) * rk_ref[...], hsum) * v).astype(BF16)
    v_s[...] = v.astype(BF16)

    for d, tri_ref in enumerate((trif_ref, trib_ref)):
        cols = slice(d * w, (d + 1) * w)
        lw_d = lw[:, cols]
        cum = _dot_rsplit(tri_ref[...], lw_d, 3)
        ends = [c * CHUNK if d == 1 else (c + 1) * CHUNK - 1 for c in range(tm // CHUNK)]
        tot = jnp.concatenate([jnp.broadcast_to(cum[e:e + 1, :], (CHUNK, w)) for e in ends], axis=0)
        g_inv = jnp.exp(-cum)
        g_end = jnp.exp(tot - cum)
        b = kkn * a[:, cols]
        at_s[d] = (kkn * jnp.exp(cum - lw_d)).astype(BF16)
        bt_s[d] = (b * g_inv).astype(BF16)
        kt_s[d] = (kd[d] * g_inv).astype(BF16)
        rt_s[d] = r * jnp.exp(cum)
        bh_s[d] = (b * g_end).astype(BF16)
        kh_s[d] = (kd[d] * g_end).astype(BF16)
        gl_s[d] = jnp.exp(tot)

    same, li, lj, eye = _pair_masks()
    lane = lax.broadcasted_iota(I32, (1, PAIR), 1)
    m0 = lane < HEAD_DIM
    m1 = lane >= HEAD_DIM
    consts = []
    for reverse in (False, True):
        before = (lj > li) if reverse else (lj < li)
        strict = same & before
        incl = same & (before | (li == lj))
        levels = []
        s = 1
        while s < CHUNK:
            blk = same & ((li // (2 * s)) == (lj // (2 * s)))
            hi_row = (li // s) % 2 == 1
            hi_col = (lj // s) % 2 == 1
            levels.append(blk & ((~hi_row & hi_col) if reverse else (hi_row & ~hi_col)))
            s *= 2
        consts.append((strict, incl, levels, eye, m0, m1, reverse))

    def chunk_body(it, carry):
        inst = []
        for j in range(CHUNKS_PER_ITER):
            c = it * CHUNKS_PER_ITER + j
            r0 = pl.multiple_of(c * CHUNK, CHUNK)
            inst += [(c, r0, pl.ds(r0, CHUNK), d, slice(p * PAIR, (p + 1) * PAIR))
                     for d in range(2) for p in range(N_PAIRS)]
        outs = _chunk_pair(
            [at_s[d, rows, ln] for c, r0, rows, d, ln in inst],
            [bt_s[d, rows, ln] for c, r0, rows, d, ln in inst],
            [kt_s[d, rows, ln] for c, r0, rows, d, ln in inst],
            [rt_s[d, rows, ln] for c, r0, rows, d, ln in inst],
            [v_s[rows, ln] for c, r0, rows, d, ln in inst],
            [bh_s[d, rows, ln] for c, r0, rows, d, ln in inst],
            [kh_s[d, rows, ln] for c, r0, rows, d, ln in inst],
            [gl_s[d, pl.ds(r0, 1), ln] for c, r0, rows, d, ln in inst],
            [consts[d] for c, r0, rows, d, ln in inst])
        for (c, r0, rows, d, ln), (qe, ol, g_mat, h_mat) in zip(inst, outs):
            q_out[d, rows, ln] = qe.astype(BF16)
            ol_out[d, rows, ln] = ol.astype(BF16)
            g_out[d, c, :, ln] = g_mat.astype(BF16)
            h_out[d, c, :, ln] = h_mat.astype(BF16)
        return carry

    lax.fori_loop(0, tm // (CHUNK * CHUNKS_PER_ITER), chunk_body, 0)


def _rwkv_a(z2d, seq, mu, w0, wup, a0, aup, gup, k_k, k_a, r_k, hsum):
    m = z2d.shape[0]
    tm = min(256, seq)
    nc = tm // CHUNK
    w = RW_WIDTH
    full = lambda shape: pl.BlockSpec(shape, lambda i: (0,) * len(shape))
    last8 = m // 8 - 1
    ti = jnp.arange(tm)[:, None]
    tj = jnp.arange(tm)[None, :]
    same_chunk = (ti // CHUNK) == (tj // CHUNK)
    tri_f = (same_chunk & (tj <= ti)).astype(BF16)
    tri_b = (same_chunk & (tj >= ti)).astype(BF16)
    wup_hi, wup_lo = _hi_lo(wup)
    aup_hi, aup_lo = _hi_lo(aup)
    kern = functools.partial(_rwkv_a_kernel, tm=tm, seq=seq)
    return pl.pallas_call(
        kern,
        grid=(m // tm,),
        in_specs=[
            pl.BlockSpec((tm, RW_COLS), lambda i: (i, 0)),
            pl.BlockSpec((8, RW_COLS), lambda i: (jnp.maximum(i * (tm // 8) - 1, 0), 0)),
            pl.BlockSpec((8, RW_COLS), lambda i: (jnp.minimum((i + 1) * (tm // 8), last8), 0)),
            full((1, RW_COLS)), full((1, 2 * w)), full((128, 2 * w)), full((128, 2 * w)),
            full((1, 2 * w)), full((128, 2 * w)), full((128, 2 * w)), full((128, w)),
            full((1, w)), full((1, w)), full((1, w)), full((w, w)), full((tm, tm)), full((tm, tm)),
        ],
        out_specs=[
            pl.BlockSpec((2, tm, w), lambda i: (0, i, 0)),
            pl.BlockSpec((2, tm, w), lambda i: (0, i, 0)),
            pl.BlockSpec((2, nc, PAIR, w), lambda i: (0, i, 0, 0)),
            pl.BlockSpec((2, nc, PAIR, w), lambda i: (0, i, 0, 0)),
            pl.BlockSpec((tm, w), lambda i: (i, 0)),
            pl.BlockSpec((tm, w), lambda i: (i, 0)),
        ],
        out_shape=[
            jax.ShapeDtypeStruct((2, m, w), BF16),
            jax.ShapeDtypeStruct((2, m, w), BF16),
            jax.ShapeDtypeStruct((2, m // CHUNK, PAIR, w), BF16),
            jax.ShapeDtypeStruct((2, m // CHUNK, PAIR, w), BF16),
            jax.ShapeDtypeStruct((m, w), BF16),
            jax.ShapeDtypeStruct((m, w), BF16),
        ],
        scratch_shapes=[pltpu.VMEM((2, tm, w), BF16)] * 3 + [pltpu.VMEM((2, tm, w), F32)]
        + [pltpu.VMEM((2, tm, w), BF16)] * 2 + [pltpu.VMEM((tm, w), BF16), pltpu.VMEM((2, tm, w), F32)],
        compiler_params=_cparams(("parallel",)),
        name="rwkv_chunk_local",
    )(z2d, z2d, z2d, mu, w0, wup_hi, wup_lo, a0, aup_hi, aup_lo, gup.astype(BF16), k_k, k_a, r_k,
      hsum, tri_f, tri_b)


def _rwkv_b_kernel(qf_ref, olf_ref, gf_ref, hf_ref, qb_ref, olb_ref, gb_ref, hb_ref,
                   of_ref, ob_ref, s_ref, *, cb):
    @pl.when(pl.program_id(1) == 0)
    def _():
        s_ref[...] = jnp.zeros_like(s_ref)

    for step in range(cb):
        inst = []
        for d, refs in enumerate(((qf_ref, olf_ref, gf_ref, hf_ref, of_ref),
                                  (qb_ref, olb_ref, gb_ref, hb_ref, ob_ref))):
            c = cb - 1 - step if d == 1 else step
            for p in range(N_PAIRS):
                inst.append((d, c, slice(c * CHUNK, (c + 1) * CHUNK),
                             slice(p * PAIR, (p + 1) * PAIR)) + refs)
        s_bf = [s_ref[d, :, ln].astype(BF16) for d, c, rows, ln, *_ in inst]
        s_new = [jnp.dot(g_ref[c, :, ln], sb, preferred_element_type=F32) + h_ref[c, :, ln]
                 for (d, c, rows, ln, q_ref, ol_ref, g_ref, h_ref, o_ref), sb in zip(inst, s_bf)]
        o_val = [jnp.dot(q_ref[rows, ln], sb, preferred_element_type=F32) + ol_ref[rows, ln]
                 for (d, c, rows, ln, q_ref, ol_ref, g_ref, h_ref, o_ref), sb in zip(inst, s_bf)]
        for (d, c, rows, ln, q_ref, ol_ref, g_ref, h_ref, o_ref), sn, ov in zip(inst, s_new, o_val):
            s_ref[d, :, ln] = sn
            o_ref[rows, ln] = ov.astype(BF16)


def _rwkv_b(qeff, oloc, g_all, h_all, batch, seq):
    m = batch * seq
    w = RW_WIDTH
    cb = min(4, seq // CHUNK)
    tm = cb * CHUNK
    nb = seq // tm

    def fwd(b, j):
        return b * nb + j

    def bwd(b, j):
        return b * nb + nb - 1 - j

    def specs(d, blk):
        return [
            pl.BlockSpec((None, tm, w), lambda b, j: (d, blk(b, j), 0)),
            pl.BlockSpec((None, tm, w), lambda b, j: (d, blk(b, j), 0)),
            pl.BlockSpec((None, cb, PAIR, w), lambda b, j: (d, blk(b, j), 0, 0)),
            pl.BlockSpec((None, cb, PAIR, w), lambda b, j: (d, blk(b, j), 0, 0)),
        ]

    return pl.pallas_call(
        functools.partial(_rwkv_b_kernel, cb=cb),
        grid=(batch, nb),
        in_specs=specs(0, fwd) + specs(1, bwd),
        out_specs=[pl.BlockSpec((tm, w), lambda b, j: (fwd(b, j), 0)),
                   pl.BlockSpec((tm, w), lambda b, j: (bwd(b, j), 0))],
        out_shape=[jax.ShapeDtypeStruct((m, w), BF16)] * 2,
        scratch_shapes=[pltpu.VMEM((2, PAIR, w), F32)],
        compiler_params=_cparams(("parallel", "arbitrary")),
        name="rwkv_recurrence",
    )(qeff, oloc, g_all, h_all, qeff, oloc, g_all, h_all)


def _mla_prep_kernel(z_ref, c_ref, s_ref, qn_ref, kvn_ref, qa_ref, qb_ref, kk_ref, kvv_ref, hsel_ref,
                     q_out, k_out, v_out, qmax_out, kmax_out, *, scale):
    z = z_ref[...]
    cos = c_ref[...]
    sin = s_ref[...]
    qd = _rms(z[:, :Q_LORA], qn_ref[...]).astype(BF16)
    kvd = _rms(z[:, Q_LORA:Q_LORA + KV_LORA], kvn_ref[...]).astype(BF16)
    o = Q_LORA + KV_LORA
    kr = z[:, o:o + LANES] * cos + z[:, o + LANES:o + 2 * LANES] * sin
    qa = jnp.dot(qd, qa_ref[...], preferred_element_type=F32)
    qb = jnp.dot(qd, qb_ref[...], preferred_element_type=F32)
    kn = jnp.dot(kvd, kk_ref[...], preferred_element_type=F32)
    v_out[...] = lax.dot_general(kvv_ref[...], kvd, (((1,), (1,)), ((), ())),
                                 preferred_element_type=F32).astype(BF16)
    qs, ks = [], []
    for h in range(MLA_HEADS):
        lanes = slice(h * LANES, (h + 1) * LANES)
        qs.append(((qa[:, lanes] * cos + qb[:, lanes] * sin) * scale).astype(BF16))
        ks.append((kn[:, lanes] + kr).astype(BF16))
        q_out[h] = qs[h]
        k_out[h] = ks[h]
    for vals, out in ((qs, qmax_out), (ks, kmax_out)):
        full = jnp.concatenate([v.astype(F32) for v in vals], axis=1)
        n2 = jnp.dot((full * full).astype(BF16), hsel_ref[...], preferred_element_type=F32)
        out[...] = jnp.broadcast_to(jnp.max(n2, axis=0, keepdims=True), (SUBLANES, LANES))


def _mla_prep(z2d, cos, sin, batch, seq, q_norm, kv_norm, q_a, q_b, kv_k, kv_v):
    tm = min(512, seq)
    nt = seq // tm
    hw = MLA_HEADS * LANES
    scale = float((QK_NOPE + QK_ROPE) ** -0.5 * math.log2(math.e))
    full = lambda shape: pl.BlockSpec(shape, lambda b, i: (0,) * len(shape))
    col_blk = RW_COLS // MLA_IN
    assert col_blk * MLA_IN == RW_COLS
    head_sel = (jnp.arange(hw)[:, None] // LANES == jnp.arange(LANES)[None, :]).astype(BF16)
    q, k, v_t, qmax, kmax = pl.pallas_call(
        functools.partial(_mla_prep_kernel, scale=scale),
        grid=(batch, nt),
        in_specs=[
            pl.BlockSpec((tm, MLA_IN), lambda b, i: (b * nt + i, col_blk)),
            pl.BlockSpec((tm, LANES), lambda b, i: (b * nt + i, 0)),
            pl.BlockSpec((tm, LANES), lambda b, i: (b * nt + i, 0)),
            full((1, Q_LORA)), full((1, KV_LORA)), full((Q_LORA, hw)), full((Q_LORA, hw)),
            full((KV_LORA, hw)), full((MLA_WIDTH, KV_LORA)), full((hw, LANES)),
        ],
        out_specs=[
            pl.BlockSpec((None, MLA_HEADS, tm, LANES), lambda b, i: (b, 0, i, 0)),
            pl.BlockSpec((None, MLA_HEADS, tm, LANES), lambda b, i: (b, 0, i, 0)),
            pl.BlockSpec((None, MLA_WIDTH, tm), lambda b, i: (b, 0, i)),
            pl.BlockSpec((None, None, SUBLANES, LANES), lambda b, i: (b, i, 0, 0)),
            pl.BlockSpec((None, None, SUBLANES, LANES), lambda b, i: (b, i, 0, 0)),
        ],
        out_shape=[
            jax.ShapeDtypeStruct((batch, MLA_HEADS, seq, LANES), BF16),
            jax.ShapeDtypeStruct((batch, MLA_HEADS, seq, LANES), BF16),
            jax.ShapeDtypeStruct((batch, MLA_WIDTH, seq), BF16),
            jax.ShapeDtypeStruct((batch, nt, SUBLANES, LANES), F32),
            jax.ShapeDtypeStruct((batch, nt, SUBLANES, LANES), F32),
        ],
        compiler_params=_cparams(("parallel", "parallel")),
        name="mla_prep",
    )(z2d, cos, sin, q_norm, kv_norm, q_a, q_b, kv_k, kv_v, head_sel)
    bound = jnp.sqrt(jnp.max(qmax[:, :, 0, :MLA_HEADS], axis=1)
                     * jnp.max(kmax[:, :, 0, :MLA_HEADS], axis=1)) * NORM_MARGIN
    small = (bound <= FLASH_SAFE_LOG2).reshape(batch, MLA_HEADS // 2, 2).all(axis=-1)
    return q, k, v_t, small.astype(I32).reshape(-1)


def _flash_kernel(small_ref, q_ref, k_ref, vt_ref, o_ref, m_ref, l_ref, acc_ref):
    j = pl.program_id(3)
    small = small_ref[pl.program_id(0) * pl.num_programs(1) + pl.program_id(1)] != 0

    @pl.when(j == 0)
    def _():
        m_ref[...] = jnp.full_like(m_ref, -jnp.inf)
        l_ref[...] = jnp.zeros_like(l_ref)
        acc_ref[...] = jnp.zeros_like(acc_ref)

    tk = k_ref.shape[1]
    sub = min(FLASH_SUB, tk)
    inst = [(h, slice(b * sub, (b + 1) * sub)) for b in range(tk // sub) for h in range(2)]
    rows = [slice(h * V_HEAD, (h + 1) * V_HEAD) for h in range(2)]
    ones = jnp.ones((ONES_ROWS, sub), BF16)

    def scores():
        return [lax.dot_general(k_ref[h, kb, :], q_ref[h], (((1,), (1,)), ((), ())),
                                preferred_element_type=F32) for h, kb in inst]

    def weighted_values(p):
        pv = [jnp.dot(jnp.concatenate([vt_ref[rows[h], kb], ones], axis=0), x,
                      preferred_element_type=F32) for (h, kb), x in zip(inst, p)]
        return [x[:V_HEAD, :] for x in pv], [x[V_HEAD:V_HEAD + 1, :] for x in pv]

    @pl.when(small)
    def _():
        pv, l_loc = weighted_values([jnp.exp2(x).astype(BF16) for x in scores()])
        for h in range(2):
            mine = [i for i, (hh, _) in enumerate(inst) if hh == h]
            l_ref[h:h + 1, :] = l_ref[h:h + 1, :] + sum(l_loc[i] for i in mine)
            acc_ref[rows[h], :] = acc_ref[rows[h], :] + sum(pv[i] for i in mine)

    @pl.when(jnp.logical_not(small))
    def _():
        s = scores()
        m_loc = [jnp.max(x, axis=0, keepdims=True) for x in s]
        pv, l_loc = weighted_values([jnp.exp2(x - m).astype(BF16) for x, m in zip(s, m_loc)])
        for h in range(2):
            mine = [i for i, (hh, _) in enumerate(inst) if hh == h]
            m_prev = m_ref[h:h + 1, :]
            m_new = m_prev
            for i in mine:
                m_new = jnp.maximum(m_new, m_loc[i])
            alpha = jnp.exp2(m_prev - m_new)
            l_new = alpha * l_ref[h:h + 1, :]
            acc = alpha * acc_ref[rows[h], :]
            for i in mine:
                w = jnp.exp2(m_loc[i] - m_new)
                l_new = l_new + w * l_loc[i]
                acc = acc + w * pv[i]
            m_ref[h:h + 1, :] = m_new
            l_ref[h:h + 1, :] = l_new
            acc_ref[rows[h], :] = acc

    @pl.when(j == pl.num_programs(3) - 1)
    def _():
        inv = 1.0 / l_ref[...]
        o_t = jnp.concatenate([acc_ref[:V_HEAD, :] * inv[0:1, :], acc_ref[V_HEAD:, :] * inv[1:2, :]],
                              axis=0)
        o_ref[...] = o_t.T


def _flash(small, q, k, v_t):
    batch, heads, seq, _ = q.shape
    tq = min(FLASH_TQ, seq)
    tk = min(FLASH_TK, seq)
    return pl.pallas_call(
        _flash_kernel,
        grid_spec=pltpu.PrefetchScalarGridSpec(
            num_scalar_prefetch=1,
            grid=(batch, heads // 2, seq // tq, seq // tk),
            in_specs=[
                pl.BlockSpec((None, 2, tq, LANES), lambda b, p, i, j, sm: (b, p, i, 0)),
                pl.BlockSpec((None, 2, tk, LANES), lambda b, p, i, j, sm: (b, p, j, 0)),
                pl.BlockSpec((None, 2 * V_HEAD, tk), lambda b, p, i, j, sm: (b, p, j)),
            ],
            out_specs=pl.BlockSpec((None, tq, LANES), lambda b, p, i, j, sm: (b, i, p)),
            scratch_shapes=[pltpu.VMEM((2, tq), F32), pltpu.VMEM((2, tq), F32),
                            pltpu.VMEM((2 * V_HEAD, tq), F32)],
        ),
        out_shape=jax.ShapeDtypeStruct((batch, seq, MLA_WIDTH), F32),
        compiler_params=_cparams(("parallel", "parallel", "parallel", "arbitrary")),
        name="mla_flash",
    )(small, q, k, v_t)


def _out_kernel(of_ref, ob_ref, bonus_ref, gate_ref, ym_ref, x_ref, hsum_ref, lw_ref, lb_ref,
                on_ref, w_ref, o_ref, ot_ref):
    o = of_ref[...].astype(F32) + ob_ref[...].astype(F32)
    hsum = hsum_ref[...]
    inv_n = 1.0 / HEAD_DIM
    mean = _dot_lsplit(o, hsum, 2) * inv_n
    d = o - mean
    var = _dot_lsplit(d * d, hsum, 2) * inv_n
    y_rw = (d * lax.rsqrt(var + GN_EPS) * lw_ref[...] + lb_ref[...] + bonus_ref[...]) * gate_ref[...]
    y_mla = _rms(ym_ref[...], on_ref[...])
    w = RW_WIDTH
    out = x_ref[...] + _dot(y_rw, w_ref[:w, :]) + _dot(y_mla, w_ref[w:, :])
    o_ref[...] = out
    _rows_to_tiles(ot_ref, out)


def _out_proj(o_f, o_b, bonus, gate, y_mla, x2d, hsum, lnx_w, lnx_b, o_norm, w_out):
    m, d = x2d.shape
    w = RW_WIDTH
    tm = min(256, m)
    row = lambda n: pl.BlockSpec((tm, n), lambda i: (i, 0))
    full = lambda shape: pl.BlockSpec(shape, lambda i: (0,) * len(shape))
    return pl.pallas_call(
        _out_kernel,
        grid=(m // tm,),
        in_specs=[row(w), row(w), row(w), row(w), row(MLA_WIDTH), row(d), full((w, w)),
                  full((1, w)), full((1, w)), full((1, MLA_WIDTH)), full((w + MLA_WIDTH, d))],
        out_specs=[row(d), pl.BlockSpec((tm * SUBLANES, LANES), lambda i: (i, 0))],
        out_shape=[jax.ShapeDtypeStruct((m, d), F32),
                   jax.ShapeDtypeStruct((m * SUBLANES, LANES), F32)],
        compiler_params=_cparams(("parallel",)),
        name="out_proj",
    )(o_f, o_b, bonus, gate, y_mla, x2d, hsum, lnx_w, lnx_b, o_norm, w_out)


def _router_kernel(x_ref, g_ref, rt_ref, a_ref):
    xn = _rms(x_ref[...], g_ref[...])
    logits = _dot32_nt(rt_ref[...], xn)
    mx = jnp.max(logits, axis=0, keepdims=True)
    e = jnp.exp(logits - mx)
    a_ref[...] = e / jnp.sum(e, axis=0, keepdims=True)


def _router(x3d, g, router_t):
    batch, seq, d = x3d.shape
    e = router_t.shape[0]
    tm = min(512, seq)
    return pl.pallas_call(
        _router_kernel,
        grid=(batch, seq // tm),
        in_specs=[pl.BlockSpec((None, tm, d), lambda b, i: (b, i, 0)),
                  pl.BlockSpec((1, d), lambda b, i: (0, 0)),
                  pl.BlockSpec((e, d), lambda b, i: (0, 0))],
        out_specs=pl.BlockSpec((None, e, tm), lambda b, i: (b, 0, i)),
        out_shape=jax.ShapeDtypeStruct((batch, e, seq), F32),
        compiler_params=_cparams(("parallel", "parallel")),
        name="moe_router",
    )(x3d, g, router_t)


def _threshold_kernel(a_ref, thr_ref, *, cap):
    bits = pltpu.bitcast(a_ref[...], I32)
    n_e = bits.shape[0]

    def search(i, cur):
        cand = cur | jnp.left_shift(jnp.int32(1), 30 - i)
        cnt = jnp.sum(jnp.where(bits >= cand, 1, 0), axis=1, keepdims=True)
        return jnp.where(cnt >= cap, cand, cur)

    thr_ref[...] = lax.fori_loop(0, 31, search, jnp.zeros((n_e, 1), I32))


def _compact_kernel(a_ref, thr_ref, idx_ref, gate_ref, *, cap):
    a = a_ref[...]
    nb = a.shape[0]
    bits = pltpu.bitcast(a, I32)
    thr = thr_ref[...]
    ri = lax.broadcasted_iota(I32, (LANES, LANES), 0)
    ci = lax.broadcasted_iota(I32, (LANES, LANES), 1)
    upper = jnp.where(ri <= ci, 1.0, 0.0).astype(BF16)
    bi = lax.broadcasted_iota(I32, (nb, nb), 0)
    bj = lax.broadcasted_iota(I32, (nb, nb), 1)
    before = jnp.where(bj < bi, 1.0, 0.0).astype(BF16)

    def total(x):
        return jnp.sum(jnp.sum(x, axis=1, keepdims=True), axis=0, keepdims=True)

    def running(mask):
        within = jnp.dot(mask, upper, preferred_element_type=F32)
        tot = jnp.broadcast_to(within[:, LANES - 1:LANES], (nb, LANES))
        return within, jnp.dot(before, tot.astype(BF16), preferred_element_type=F32)

    gt = bits > thr
    eq = bits == thr
    need = cap - total(jnp.where(gt, 1.0, 0.0))
    w_eq, b_eq = running(jnp.where(eq, 1.0, 0.0).astype(BF16))
    sel = gt | (eq & (w_eq + b_eq <= need))
    sel_b = jnp.where(sel, 1.0, 0.0).astype(BF16)
    within, base = running(sel_b)

    tot_row = lax.dot_general(jnp.ones((SUBLANES, LANES), BF16), sel_b, (((1,), (1,)), ((), ())),
                              preferred_element_type=F32)
    base_row = jnp.dot(tot_row.astype(BF16), jnp.where(bi < bj, 1.0, 0.0).astype(BF16),
                       preferred_element_type=F32)
    c_col = lax.broadcasted_iota(I32, (cap, 1), 0).astype(F32)
    in_block = (base_row[0:1, :] <= c_col) & (c_col < base_row[0:1, :] + tot_row[0:1, :])
    onehot = jnp.where(in_block, 1.0, 0.0).astype(BF16)

    a_parts = _split_bf16(a, 3)
    base_hi = jnp.floor(base * (1.0 / 32.0))
    lane = lax.broadcasted_iota(I32, (1, LANES), 1)
    block_id = lax.broadcasted_iota(I32, (nb, LANES), 0).astype(F32)
    small = jnp.where(lane == 0, base_hi, jnp.where(lane == 1, base - 32.0 * base_hi,
                                                     jnp.where(lane == 2, block_id, 0.0)))
    table = jnp.concatenate([within.astype(BF16), sel_b] + a_parts + [small.astype(BF16)], axis=1)
    g = jnp.dot(onehot, table, preferred_element_type=F32)
    w_c = g[:, :LANES]
    s_c = g[:, LANES:2 * LANES]
    a_c = g[:, 2 * LANES:3 * LANES] + g[:, 3 * LANES:4 * LANES] + g[:, 4 * LANES:5 * LANES]
    sm = g[:, 5 * LANES:]
    target = c_col - (32.0 * sm[:, 0:1] + sm[:, 1:2]) + 1.0
    match = (w_c == target) & (s_c > 0.5)
    pos = jnp.sum(jnp.where(match, lane.astype(F32), 0.0), axis=1, keepdims=True)
    gate_ref[...] = jnp.sum(jnp.where(match, a_c, 0.0), axis=1, keepdims=True)
    idx_ref[...] = (LANES * sm[:, 2:3] + pos).astype(I32)


def _select(aff_t, cap):
    batch, e, seq = aff_t.shape
    nb = seq // LANES
    thr = pl.pallas_call(
        functools.partial(_threshold_kernel, cap=cap),
        grid=(batch,),
        in_specs=[pl.BlockSpec((None, e, seq), lambda b: (b, 0, 0))],
        out_specs=pl.BlockSpec((None, e, 1), lambda b: (b, 0, 0)),
        out_shape=jax.ShapeDtypeStruct((batch, e, 1), I32),
        compiler_params=_cparams(("parallel",)),
        name="moe_threshold",
    )(aff_t)
    idx, gate = pl.pallas_call(
        functools.partial(_compact_kernel, cap=cap),
        grid=(batch, e),
        in_specs=[pl.BlockSpec((None, None, nb, LANES), lambda b, j: (b, j, 0, 0)),
                  pl.BlockSpec((None, None, 1, 1), lambda b, j: (b, j, 0, 0))],
        out_specs=[pl.BlockSpec((None, None, cap, 1), lambda b, j: (b, j, 0, 0))] * 2,
        out_shape=[jax.ShapeDtypeStruct((batch, e, cap, 1), I32),
                   jax.ShapeDtypeStruct((batch, e, cap, 1), F32)],
        compiler_params=_cparams(("parallel", "parallel")),
        name="moe_compact",
    )(aff_t.reshape(batch, e, nb, LANES), thr.reshape(batch, e, 1, 1))
    return idx.reshape(-1), gate


def _rows_from_tiles(ref):
    n = ref.shape[0] // SUBLANES
    return jnp.concatenate([ref[pl.ds(s, n, stride=SUBLANES), :] for s in range(SUBLANES)], axis=-1)


def _rows_to_tiles(ref, val):
    n = val.shape[0]
    for s in range(SUBLANES):
        ref[pl.ds(s, n, stride=SUBLANES), :] = val[:, s * LANES:(s + 1) * LANES]


def _tile(r):
    return pl.ds(pl.multiple_of(r * SUBLANES, SUBLANES), SUBLANES)


def _gather_kernel(idx_ref, x_ref, o_ref, *, cap, n_e):
    base = (pl.program_id(0) * n_e + pl.program_id(1)) * cap

    def body(c, carry):
        o_ref[_tile(c), :] = x_ref[_tile(idx_ref[base + c]), :]
        return carry

    lax.fori_loop(0, cap, body, 0, unroll=8)


def _gather(idx_flat, x_tiles, n_e, cap):
    batch, rows, _ = x_tiles.shape
    return pl.pallas_call(
        functools.partial(_gather_kernel, cap=cap, n_e=n_e),
        grid_spec=pltpu.PrefetchScalarGridSpec(
            num_scalar_prefetch=1,
            grid=(batch, n_e),
            in_specs=[pl.BlockSpec((None, rows, LANES), lambda b, e, idx: (b, 0, 0),
                                   pipeline_mode=pl.Buffered(1))],
            out_specs=pl.BlockSpec((None, None, cap * SUBLANES, LANES),
                                   lambda b, e, idx: (b, e, 0, 0)),
        ),
        out_shape=jax.ShapeDtypeStruct((batch, n_e, cap * SUBLANES, LANES), F32),
        compiler_params=_cparams(("parallel", "arbitrary")),
        name="moe_gather",
    )(idx_flat, x_tiles)


def _ffn_kernel(x_ref, gate_ref, g_ref, wg_ref, wu_ref, wd_ref, o_ref, wg_s, wu_s, wd_s):
    @pl.when((pl.program_id(1) == 0) & (pl.program_id(2) == 0))
    def _():
        wg_s[...] = wg_ref[...].astype(BF16)
        wu_s[...] = wu_ref[...].astype(BF16)
        wd_s[...] = wd_ref[...].astype(BF16)

    xn = _rms(_rows_from_tiles(x_ref), g_ref[...]).astype(BF16)
    h1 = jnp.dot(xn, wg_s[...], preferred_element_type=F32)
    h2 = jnp.dot(xn, wu_s[...], preferred_element_type=F32)
    hid = (h1 * _sigmoid(h1) * h2).astype(BF16)
    _rows_to_tiles(o_ref, jnp.dot(hid, wd_s[...], preferred_element_type=F32) * gate_ref[...])


def _expert_ffn(xe, gate_col, g, w_gate, w_up, w_down, layer):
    batch, n_e, rows, _ = xe.shape
    cap = rows // SUBLANES
    d, f = w_gate.shape[2:]
    tc = min(512, cap)
    return pl.pallas_call(
        _ffn_kernel,
        grid=(n_e, batch, cap // tc),
        in_specs=[
            pl.BlockSpec((None, None, tc * SUBLANES, LANES), lambda e, b, c: (b, e, c, 0)),
            pl.BlockSpec((None, None, tc, 1), lambda e, b, c: (b, e, c, 0)),
            pl.BlockSpec((1, d), lambda e, b, c: (0, 0)),
            pl.BlockSpec((None, None, d, f), lambda e, b, c: (layer, e, 0, 0)),
            pl.BlockSpec((None, None, d, f), lambda e, b, c: (layer, e, 0, 0)),
            pl.BlockSpec((None, None, f, d), lambda e, b, c: (layer, e, 0, 0)),
        ],
        out_specs=pl.BlockSpec((None, None, tc * SUBLANES, LANES), lambda e, b, c: (b, e, c, 0)),
        out_shape=jax.ShapeDtypeStruct(xe.shape, F32),
        scratch_shapes=[pltpu.VMEM((d, f), BF16), pltpu.VMEM((d, f), BF16), pltpu.VMEM((f, d), BF16)],
        compiler_params=_cparams(("parallel", "arbitrary", "arbitrary")),
        name="moe_ffn",
    )(xe, gate_col, g, w_gate, w_up, w_down)


SCATTER_GROUP = 8


def _scatter_kernel(idx_ref, y_ref, o_ref, *, cap, n_e):
    e = pl.program_id(1)
    base = (pl.program_id(0) * n_e + e) * cap

    @pl.when(e == 0)
    def _():
        o_ref[...] = jnp.zeros_like(o_ref)

    def body(g, carry):
        c0 = g * SCATTER_GROUP
        rows = [idx_ref[base + c0 + i] for i in range(SCATTER_GROUP)]
        new = [o_ref[_tile(r), :] + y_ref[_tile(c0 + i), :] for i, r in enumerate(rows)]
        for r, v in zip(rows, new):
            o_ref[_tile(r), :] = v
        return carry

    lax.fori_loop(0, cap // SCATTER_GROUP, body, 0)


def _scatter_add(idx_flat, ye, seq):
    batch, n_e, rows, _ = ye.shape
    cap = rows // SUBLANES
    return pl.pallas_call(
        functools.partial(_scatter_kernel, cap=cap, n_e=n_e),
        grid_spec=pltpu.PrefetchScalarGridSpec(
            num_scalar_prefetch=1,
            grid=(batch, n_e),
            in_specs=[pl.BlockSpec((None, None, rows, LANES), lambda b, e, idx: (b, e, 0, 0))],
            out_specs=pl.BlockSpec((None, seq * SUBLANES, LANES), lambda b, e, idx: (b, 0, 0),
                                   pipeline_mode=pl.Buffered(1)),
        ),
        out_shape=jax.ShapeDtypeStruct((batch, seq * SUBLANES, LANES), F32),
        compiler_params=_cparams(("parallel", "arbitrary")),
        name="moe_scatter_add",
    )(idx_flat, ye)


def _ple_kernel(x_ref, d_ref, p_ref, g_ref, wp_ref, wg_ref, fg_ref, o_ref, *, final):
    x = x_ref[...] + _rows_from_tiles(d_ref)
    gate = _sigmoid(_dot(_rms(x, g_ref[...]), wg_ref[...]))
    out = x + _dot(p_ref[...], wp_ref[...]) * gate
    if final:
        out = _rms(out, fg_ref[...])
    o_ref[...] = out


def _ple(x2d, delta_tiles, p_all, layer, g, w_proj, w_gate, final_g, final):
    m, d = x2d.shape
    dp = p_all.shape[1]
    tm = min(512, m)
    first = layer * (m // tm)
    full = lambda shape: pl.BlockSpec(shape, lambda i: (0,) * len(shape))
    return pl.pallas_call(
        functools.partial(_ple_kernel, final=final),
        grid=(m // tm,),
        in_specs=[pl.BlockSpec((tm, d), lambda i: (i, 0)),
                  pl.BlockSpec((tm * SUBLANES, LANES), lambda i: (i, 0)),
                  pl.BlockSpec((tm, dp), lambda i: (first + i, 0)),
                  full((1, d)), full((dp, d)), full((d, d)), full((1, d))],
        out_specs=pl.BlockSpec((tm, d), lambda i: (i, 0)),
        out_shape=jax.ShapeDtypeStruct((m, d), F32),
        compiler_params=_cparams(("parallel",)),
        name="ple_final" if final else "ple",
    )(x2d, delta_tiles, p_all, g, w_proj, w_gate, final_g)


def _hi_lo(w):
    hi = w.astype(BF16)
    return hi, (w - hi.astype(F32)).astype(BF16)


def _rot_cols(w):
    half = QK_ROPE // 2
    return jnp.concatenate([-w[..., half:], w[..., :half]], axis=-1)


def _pad_head(nope, rope):
    lead = (nope if nope is not None else rope).shape[:-1]
    n = nope if nope is not None else jnp.zeros(lead + (QK_NOPE,), F32)
    r = rope if rope is not None else jnp.zeros(lead + (QK_ROPE,), F32)
    return jnp.concatenate([n, r, jnp.zeros(lead + (LANES - QK_NOPE - QK_ROPE,), F32)], axis=-1)


def _block_rows(w_pair):
    z = jnp.zeros_like(w_pair[0])
    return jnp.concatenate([jnp.concatenate([w_pair[0], z], axis=1),
                            jnp.concatenate([z, w_pair[1]], axis=1)], axis=0)


def kernel(x, p, positions, attn_norm, w_in, rw_mu, rw_w0, rw_w_up, rw_a0, rw_a_up, rw_g_up, rw_k_k,
           rw_k_a, rw_r_k, rw_lnx_w, rw_lnx_b, mla_q_norm, mla_q_up, mla_kv_norm, mla_kv_up,
           mla_o_norm, w_out, ffn_norm, router, exp_w_gate, exp_w_up, exp_w_down, ple_norm,
           ple_proj, ple_gate, final_norm):
    batch, seq, d = x.shape
    depth = w_in.shape[0]
    m = batch * seq
    cap = EC_FACTOR * seq // N_EXPERTS
    w = RW_WIDTH

    cos, sin = _rope_tables(positions)
    hsum = (jnp.arange(w)[:, None] // HEAD_DIM == jnp.arange(w)[None, :] // HEAD_DIM).astype(BF16)
    x2d = x.reshape(m, d)

    for i in range(depth):
        w_mla = w_in[i][:, RW_COLS:]
        w_kr = w_mla[:, Q_LORA + KV_LORA:]
        w_ext = jnp.concatenate([w_in[i][:, :RW_COLS], w_mla[:, :Q_LORA + KV_LORA],
                                 _pad_head(None, w_kr), _pad_head(None, _rot_cols(w_kr))],
                                axis=1).astype(BF16)
        q_up = mla_q_up[i].reshape(Q_LORA, MLA_HEADS, QK_NOPE + QK_ROPE)
        q_a = _pad_head(q_up[..., :QK_NOPE], q_up[..., QK_NOPE:]).reshape(Q_LORA, -1).astype(BF16)
        q_b = _pad_head(None, _rot_cols(q_up[..., QK_NOPE:])).reshape(Q_LORA, -1).astype(BF16)
        kv_up = mla_kv_up[i].reshape(KV_LORA, MLA_HEADS, QK_NOPE + V_HEAD)
        kv_k = _pad_head(kv_up[..., :QK_NOPE], None).reshape(KV_LORA, -1).astype(BF16)
        kv_v = kv_up[..., QK_NOPE:].reshape(KV_LORA, MLA_WIDTH).T.astype(BF16)

        z2d = _in_proj(x2d, attn_norm[i][None, :], w_ext)
        qeff, oloc, g_all, h_all, bonus, gate = _rwkv_a(
            z2d, seq, rw_mu[i][None, :], rw_w0[i].reshape(1, 2 * w), _block_rows(rw_w_up[i]),
            rw_a0[i].reshape(1, 2 * w), _block_rows(rw_a_up[i]), rw_g_up[i], rw_k_k[i][None, :],
            rw_k_a[i][None, :], rw_r_k[i].reshape(1, w), hsum)
        o_f, o_b = _rwkv_b(qeff, oloc, g_all, h_all, batch, seq)
        q, k, v_t, small = _mla_prep(z2d, cos, sin, batch, seq, mla_q_norm[i][None, :],
                                     mla_kv_norm[i][None, :], q_a, q_b, kv_k, kv_v)
        y_mla = _flash(small, q, k, v_t).reshape(m, MLA_WIDTH)
        x2d, x_tiles = _out_proj(o_f, o_b, bonus, gate, y_mla, x2d, hsum, rw_lnx_w[i][None, :],
                                 rw_lnx_b[i][None, :], mla_o_norm[i][None, :],
                                 w_out[i].astype(BF16))

        aff_t = _router(x2d.reshape(batch, seq, d), ffn_norm[i][None, :], router[i].T)
        idx_flat, gates = _select(aff_t, cap)
        xe = _gather(idx_flat, x_tiles.reshape(batch, seq * SUBLANES, LANES), N_EXPERTS, cap)
        ye = _expert_ffn(xe, gates, ffn_norm[i][None, :], exp_w_gate, exp_w_up, exp_w_down, i)
        delta = _scatter_add(idx_flat, ye, seq).reshape(m * SUBLANES, LANES)

        x2d = _ple(x2d, delta, p.reshape(depth * m, -1), i, ple_norm[i][None, :], ple_proj[i].astype(BF16),
                   ple_gate[i].astype(BF16), final_norm[None, :], final=(i == depth - 1))

    return x2d.reshape(batch, seq, d)
```

```python
import functools
import math

import jax
import jax.numpy as jnp
from jax import lax
from jax.experimental import pallas as pl
from jax.experimental.pallas import tpu as pltpu

F32 = jnp.float32
BF16 = jnp.bfloat16
I32 = jnp.int32

RW_HEADS = 8
HEAD_DIM = 64
RW_WIDTH = RW_HEADS * HEAD_DIM
RW_COLS = 3 * RW_WIDTH + 2 * 64 + 2 * 64 + 128
MLA_HEADS = 8
QK_NOPE = 64
QK_ROPE = 32
V_HEAD = 64
Q_LORA = 256
KV_LORA = 128
MLA_WIDTH = MLA_HEADS * V_HEAD
MLA_IN = Q_LORA + KV_LORA + 2 * 128
ROPE_THETA = 10000.0
N_EXPERTS = 16
EC_FACTOR = 2
NORM_EPS = 1e-6
GN_EPS = 64e-5

LANES = 128
SUBLANES = 8
CHUNK = 64
CHUNKS_PER_ITER = 4
PAIR = 2 * HEAD_DIM
N_PAIRS = RW_WIDTH // PAIR
VMEM_LIMIT = 56 * 1024 * 1024
FLASH_TQ = 2048
FLASH_TK = 2048
FLASH_SUB = 256
ONES_ROWS = 16
FLASH_SAFE_LOG2 = 40.0
NORM_MARGIN = 1.05


def _cparams(sem):
    return pltpu.CompilerParams(dimension_semantics=sem, vmem_limit_bytes=VMEM_LIMIT)


def _rms(x, g):
    return x * lax.rsqrt(jnp.mean(x * x, axis=-1, keepdims=True) + NORM_EPS) * g


def _sigmoid(x):
    return 1.0 / (1.0 + jnp.exp(-x))


def _dot(a, b):
    return jnp.dot(a.astype(BF16), b.astype(BF16), preferred_element_type=F32)


def _split_bf16(x, parts):
    out = []
    rest = x
    for _ in range(parts):
        hi = rest.astype(BF16)
        out.append(hi)
        rest = rest - hi.astype(F32)
    return out


def _dot_lsplit(a, b_exact, parts):
    acc = None
    for term in _split_bf16(a, parts):
        d = jnp.dot(term, b_exact, preferred_element_type=F32)
        acc = d if acc is None else acc + d
    return acc


def _dot_rsplit(a_exact, b, parts):
    acc = None
    for term in _split_bf16(b, parts):
        d = jnp.dot(a_exact, term, preferred_element_type=F32)
        acc = d if acc is None else acc + d
    return acc


def _dot_3pass(a, b_hi, b_lo):
    a_hi, a_lo = _split_bf16(a, 2)
    return (jnp.dot(a_hi, b_hi, preferred_element_type=F32)
            + jnp.dot(a_lo, b_hi, preferred_element_type=F32)
            + jnp.dot(a_hi, b_lo, preferred_element_type=F32))


def _dot_tn(a, b):
    return lax.dot_general(a.astype(BF16), b.astype(BF16), (((0,), (0,)), ((), ())),
                           preferred_element_type=F32)


def _dot_nt(a, b):
    return lax.dot_general(a.astype(BF16), b.astype(BF16), (((1,), (1,)), ((), ())),
                           preferred_element_type=F32)


def _rope_kernel(pos_ref, inv_ref, c_ref, s_ref):
    ang = pos_ref[...] * inv_ref[...]
    c_ref[...] = jnp.cos(ang)
    s_ref[...] = jnp.sin(ang)


def _rope_tables(positions):
    b, t = positions.shape
    m = b * t
    inv = ROPE_THETA ** (-jnp.arange(0, QK_ROPE, 2, dtype=F32) / QK_ROPE)
    inv_row = jnp.concatenate([jnp.zeros((QK_NOPE,), F32), inv, inv,
                               jnp.zeros((LANES - QK_NOPE - QK_ROPE,), F32)])[None, :]
    posf = jnp.broadcast_to(positions.astype(F32).reshape(m, 1), (m, LANES))
    tm = min(1024, m)
    return pl.pallas_call(
        _rope_kernel,
        grid=(m // tm,),
        in_specs=[pl.BlockSpec((tm, LANES), lambda i: (i, 0)),
                  pl.BlockSpec((1, LANES), lambda i: (0, 0))],
        out_specs=[pl.BlockSpec((tm, LANES), lambda i: (i, 0))] * 2,
        out_shape=[jax.ShapeDtypeStruct((m, LANES), F32)] * 2,
        compiler_params=_cparams(("parallel",)),
        name="rope_tables",
    )(posf, inv_row)


def _in_kernel(x_ref, g_ref, w_ref, o_ref):
    h = _rms(x_ref[...], g_ref[...])
    o_ref[...] = jnp.dot(h.astype(BF16), w_ref[...], preferred_element_type=F32)


def _in_proj(x2d, g, w_ext):
    m, d = x2d.shape
    n = w_ext.shape[1]
    tm = min(512, m)
    return pl.pallas_call(
        _in_kernel,
        grid=(m // tm,),
        in_specs=[pl.BlockSpec((tm, d), lambda i: (i, 0)),
                  pl.BlockSpec((1, d), lambda i: (0, 0)),
                  pl.BlockSpec((d, n), lambda i: (0, 0))],
        out_specs=pl.BlockSpec((tm, n), lambda i: (i, 0)),
        out_shape=jax.ShapeDtypeStruct((m, n), F32),
        compiler_params=_cparams(("parallel",)),
        name="in_proj",
    )(x2d, g, w_ext)


def _pair_masks():
    i = lax.broadcasted_iota(I32, (PAIR, PAIR), 0)
    j = lax.broadcasted_iota(I32, (PAIR, PAIR), 1)
    same = (i // CHUNK) == (j // CHUNK)
    li = i % CHUNK
    lj = j % CHUNK
    return same, li, lj, i == j


def _chunk_pair(a_t, b_t, k_t, r_t, v, b_h, k_h, g_last, consts):
    n = len(a_t)
    idx = range(n)
    strict = [c[0] for c in consts]
    incl = [c[1] for c in consts]
    levels = [c[2] for c in consts]
    eye, m0, m1 = consts[0][3:6]
    reverse = [c[6] for c in consts]

    def stack(x):
        return jnp.concatenate([jnp.where(m0, x, jnp.zeros_like(x)),
                                jnp.where(m1, x, jnp.zeros_like(x))], axis=0)

    a2, b2, k2, v2, bh2, kh2, r2 = ([stack(x) for x in xs] for xs in (a_t, b_t, k_t, v, b_h, k_h, r_t))
    prod = [_dot_nt(jnp.concatenate([a2[i], r2[i].astype(BF16)], axis=0),
                    jnp.concatenate([b2[i], k2[i]], axis=0)) for i in idx]
    n_mat = [jnp.where(strict[i], prod[i][:PAIR, :PAIR], 0.0) for i in idx]
    m_ak = [jnp.where(strict[i], prod[i][:PAIR, PAIR:], 0.0).astype(BF16) for i in idx]
    m_rb = [jnp.where(incl[i], prod[i][PAIR:, :PAIR], 0.0).astype(BF16) for i in idx]
    m_rk = [jnp.where(incl[i], prod[i][PAIR:, PAIR:], 0.0).astype(BF16) for i in idx]
    mv = [_dot(jnp.concatenate([m_ak[i], m_rk[i]], axis=0), v2[i]) for i in idx]

    x = [jnp.where(eye, 1.0, 0.0) - jnp.where(levels[i][0], n_mat[i], 0.0) for i in idx]
    n_bf = [n_mat[i].astype(BF16) for i in idx]
    zero = jnp.zeros((PAIR, PAIR), BF16)
    for lv in range(1, len(levels[0])):
        s = 2 ** lv
        if s < SUBLANES:
            cx = [_dot(jnp.where(levels[i][lv], n_bf[i], zero), x[i]) for i in idx]
            x = [x[i] - _dot(x[i], cx[i]) for i in idx]
            continue
        blocks = [(r, r + s) for r in range(0, PAIR, s)]
        upd = [[((r % CHUNK) // s) % 2 == (0 if reverse[i] else 1) for r, _ in blocks] for i in idx]

        def take(mat, i):
            return jnp.concatenate([mat[r0:r1] for (r0, r1), u in zip(blocks, upd[i]) if u], axis=0)

        c_h = [take(jnp.where(levels[i][lv], n_mat[i], 0.0), i) for i in idx]
        cx_h = [_dot(c_h[i], x[i]) for i in idx]
        zrows = jnp.zeros((s, PAIR), F32)
        cx = []
        for i in idx:
            it = iter(range(CHUNK // s))
            cx.append(jnp.concatenate(
                [cx_h[i][k * s:(k + 1) * s] if u else zrows
                 for u in upd[i] for k in ([next(it)] if u else [0])], axis=0))
        du = [_dot(take(x[i], i), cx[i]) for i in idx]
        x_new = []
        for i in idx:
            it = iter(range(CHUNK // s))
            x_new.append(jnp.concatenate(
                [x[i][r0:r1] - du[i][k * s:(k + 1) * s] if u else x[i][r0:r1]
                 for (r0, r1), u in zip(blocks, upd[i]) for k in ([next(it)] if u else [0])], axis=0))
        x = x_new

    tw = [_dot(x[i], jnp.concatenate([a2[i], mv[i][:PAIR].astype(BF16)], axis=1))
          for i in idx]
    tw_bf = [t.astype(BF16) for t in tw]
    qo = [jnp.concatenate([r2[i], mv[i][PAIR:]], axis=1) - _dot(m_rb[i], tw_bf[i])
          for i in idx]
    bt = [_dot_tn(bh2[i], tw_bf[i]) for i in idx]
    kv = [_dot_tn(kh2[i], v2[i]) for i in idx]
    out = []
    for i in idx:
        g_mat = jnp.where(eye, g_last[i], 0.0) - bt[i][:, :PAIR]
        h_mat = kv[i] - bt[i][:, PAIR:]
        q = qo[i][:CHUNK] + qo[i][CHUNK:]
        out.append((q[:, :PAIR], q[:, PAIR:], g_mat, h_mat))
    return out


def _rwkv_a_kernel(z_ref, zp_ref, zn_ref, mu_ref, w0_ref, wuph_ref, wupl_ref, a0_ref, auph_ref,
                   aupl_ref, gup_ref, kk_ref, ka_ref, rk_ref, hsum_ref, trif_ref, trib_ref,
                   q_out, ol_out, g_out, h_out, bonus_out, gate_out,
                   at_s, bt_s, kt_s, rt_s, bh_s, kh_s, v_s, gl_s, *, tm, seq):
    i = pl.program_id(0)
    z = z_ref[...]
    has_prev = (i * tm) % seq != 0
    has_next = ((i + 1) * tm) % seq != 0
    prev_row = jnp.where(has_prev, zp_ref[SUBLANES - 1:SUBLANES, :], 0.0)
    next_row = jnp.where(has_next, zn_ref[0:1, :], 0.0)
    sub = lax.broadcasted_iota(I32, (SUBLANES, 1), 0)
    z_dn = pltpu.roll(z, 1, axis=0)
    z_dn = jnp.concatenate([jnp.where(sub == 0, prev_row, z_dn[:SUBLANES]), z_dn[SUBLANES:]], axis=0)
    z_up = pltpu.roll(z, tm - 1, axis=0)
    z_up = jnp.concatenate([z_up[:tm - SUBLANES],
                            jnp.where(sub == SUBLANES - 1, next_row, z_up[tm - SUBLANES:])], axis=0)
    zs = z + mu_ref[...] * (0.5 * (z_dn + z_up) - z)

    w = RW_WIDTH
    r = zs[:, :w]
    k = zs[:, w:2 * w]
    v = zs[:, 2 * w:3 * w]
    wd = zs[:, 3 * w:3 * w + 128]
    ad = zs[:, 3 * w + 128:3 * w + 256]
    gd = zs[:, 3 * w + 256:3 * w + 384]

    hsum = hsum_ref[...]
    w_logit = w0_ref[...] + _dot_3pass(jnp.tanh(wd), wuph_ref[...], wupl_ref[...])
    lw = -_sigmoid(w_logit) * jnp.exp(jnp.float32(-0.5))
    a = _sigmoid(a0_ref[...] + _dot_3pass(ad, auph_ref[...], aupl_ref[...]))
    gate_out[...] = _dot(_sigmoid(gd), gup_ref[...]).astype(BF16)
    kkr = k * kk_ref[...]
    kkn = kkr * jnp.minimum(lax.rsqrt(_dot(kkr * kkr, hsum)), 1e12)
    kd = [k * (1.0 + (a[:, d * w:(d + 1) * w] - 1.0) * ka_ref[...]) for d in range(2)]
    bonus_out[...] = (_dot(r * (0.5 * (kd[0] + kd[1])) * rk_ref[...], hsum) * v).astype(BF16)
    v_s[...] = v.astype(BF16)

    for d, tri_ref in enumerate((trif_ref, trib_ref)):
        cols = slice(d * w, (d + 1) * w)
        lw_d = lw[:, cols]
        cum = _dot_rsplit(tri_ref[...], lw_d, 3)
        ends = [c * CHUNK if d == 1 else (c + 1) * CHUNK - 1 for c in range(tm // CHUNK)]
        tot = jnp.concatenate([jnp.broadcast_to(cum[e:e + 1, :], (CHUNK, w)) for e in ends], axis=0)
        g_inv = jnp.exp(-cum)
        g_end = jnp.exp(tot - cum)
        b = kkn * a[:, cols]
        at_s[d] = (kkn * jnp.exp(cum - lw_d)).astype(BF16)
        bt_s[d] = (b * g_inv).astype(BF16)
        kt_s[d] = (kd[d] * g_inv).astype(BF16)
        rt_s[d] = r * jnp.exp(cum)
        bh_s[d] = (b * g_end).astype(BF16)
        kh_s[d] = (kd[d] * g_end).astype(BF16)
        gl_s[d] = jnp.exp(tot)

    same, li, lj, eye = _pair_masks()
    lane = lax.broadcasted_iota(I32, (1, PAIR), 1)
    m0 = lane < HEAD_DIM
    m1 = lane >= HEAD_DIM
    consts = []
    for reverse in (False, True):
        before = (lj > li) if reverse else (lj < li)
        strict = same & before
        incl = same & (before | (li == lj))
        levels = []
        s = 1
        while s < CHUNK:
            blk = same & ((li // (2 * s)) == (lj // (2 * s)))
            hi_row = (li // s) % 2 == 1
            hi_col = (lj // s) % 2 == 1
            levels.append(blk & ((~hi_row & hi_col) if reverse else (hi_row & ~hi_col)))
            s *= 2
        consts.append((strict, incl, levels, eye, m0, m1, reverse))

    def chunk_body(it, carry):
        inst = []
        for j in range(CHUNKS_PER_ITER):
            c = it * CHUNKS_PER_ITER + j
            r0 = pl.multiple_of(c * CHUNK, CHUNK)
            inst += [(c, r0, pl.ds(r0, CHUNK), d, slice(p * PAIR, (p + 1) * PAIR))
                     for d in range(2) for p in range(N_PAIRS)]
        outs = _chunk_pair(
            [at_s[d, rows, ln] for c, r0, rows, d, ln in inst],
            [bt_s[d, rows, ln] for c, r0, rows, d, ln in inst],
            [kt_s[d, rows, ln] for c, r0, rows, d, ln in inst],
            [rt_s[d, rows, ln] for c, r0, rows, d, ln in inst],
            [v_s[rows, ln] for c, r0, rows, d, ln in inst],
            [bh_s[d, rows, ln] for c, r0, rows, d, ln in inst],
            [kh_s[d, rows, ln] for c, r0, rows, d, ln in inst],
            [gl_s[d, pl.ds(r0, 1), ln] for c, r0, rows, d, ln in inst],
            [consts[d] for c, r0, rows, d, ln in inst])
        for (c, r0, rows, d, ln), (qe, ol, g_mat, h_mat) in zip(inst, outs):
            q_out[d, rows, ln] = qe.astype(BF16)
            ol_out[d, rows, ln] = ol.astype(BF16)
            g_out[d, c, :, ln] = g_mat.astype(BF16)
            h_out[d, c, :, ln] = h_mat.astype(BF16)
        return carry

    lax.fori_loop(0, tm // (CHUNK * CHUNKS_PER_ITER), chunk_body, 0)


def _rwkv_a(z2d, seq, mu, w0, wup, a0, aup, gup, k_k, k_a, r_k, hsum):
    m = z2d.shape[0]
    tm = min(256, seq)
    nc = tm // CHUNK
    w = RW_WIDTH
    full = lambda shape: pl.BlockSpec(shape, lambda i: (0,) * len(shape))
    last8 = m // 8 - 1
    ti = jnp.arange(tm)[:, None]
    tj = jnp.arange(tm)[None, :]
    same_chunk = (ti // CHUNK) == (tj // CHUNK)
    tri_f = (same_chunk & (tj <= ti)).astype(BF16)
    tri_b = (same_chunk & (tj >= ti)).astype(BF16)
    wup_hi, wup_lo = _hi_lo(wup)
    aup_hi, aup_lo = _hi_lo(aup)
    kern = functools.partial(_rwkv_a_kernel, tm=tm, seq=seq)
    return pl.pallas_call(
        kern,
        grid=(m // tm,),
        in_specs=[
            pl.BlockSpec((tm, RW_COLS), lambda i: (i, 0)),
            pl.BlockSpec((8, RW_COLS), lambda i: (jnp.maximum(i * (tm // 8) - 1, 0), 0)),
            pl.BlockSpec((8, RW_COLS), lambda i: (jnp.minimum((i + 1) * (tm // 8), last8), 0)),
            full((1, RW_COLS)), full((1, 2 * w)), full((128, 2 * w)), full((128, 2 * w)),
            full((1, 2 * w)), full((128, 2 * w)), full((128, 2 * w)), full((128, w)),
            full((1, w)), full((1, w)), full((1, w)), full((w, w)), full((tm, tm)), full((tm, tm)),
        ],
        out_specs=[
            pl.BlockSpec((2, tm, w), lambda i: (0, i, 0)),
            pl.BlockSpec((2, tm, w), lambda i: (0, i, 0)),
            pl.BlockSpec((2, nc, PAIR, w), lambda i: (0, i, 0, 0)),
            pl.BlockSpec((2, nc, PAIR, w), lambda i: (0, i, 0, 0)),
            pl.BlockSpec((tm, w), lambda i: (i, 0)),
            pl.BlockSpec((tm, w), lambda i: (i, 0)),
        ],
        out_shape=[
            jax.ShapeDtypeStruct((2, m, w), BF16),
            jax.ShapeDtypeStruct((2, m, w), BF16),
            jax.ShapeDtypeStruct((2, m // CHUNK, PAIR, w), BF16),
            jax.ShapeDtypeStruct((2, m // CHUNK, PAIR, w), BF16),
            jax.ShapeDtypeStruct((m, w), BF16),
            jax.ShapeDtypeStruct((m, w), BF16),
        ],
        scratch_shapes=[pltpu.VMEM((2, tm, w), BF16)] * 3 + [pltpu.VMEM((2, tm, w), F32)]
        + [pltpu.VMEM((2, tm, w), BF16)] * 2 + [pltpu.VMEM((tm, w), BF16), pltpu.VMEM((2, tm, w), F32)],
        compiler_params=_cparams(("parallel",)),
        name="rwkv_chunk_local",
    )(z2d, z2d, z2d, mu, w0, wup_hi, wup_lo, a0, aup_hi, aup_lo, gup.astype(BF16), k_k, k_a, r_k,
      hsum, tri_f, tri_b)


def _rwkv_b_kernel(qf_ref, olf_ref, gf_ref, hf_ref, qb_ref, olb_ref, gb_ref, hb_ref,
                   of_ref, ob_ref, s_ref, *, cb):
    @pl.when(pl.program_id(1) == 0)
    def _():
        s_ref[...] = jnp.zeros_like(s_ref)

    for step in range(cb):
        inst = []
        for d, refs in enumerate(((qf_ref, olf_ref, gf_ref, hf_ref, of_ref),
                                  (qb_ref, olb_ref, gb_ref, hb_ref, ob_ref))):
            c = cb - 1 - step if d == 1 else step
            for p in range(N_PAIRS):
                inst.append((d, c, slice(c * CHUNK, (c + 1) * CHUNK),
                             slice(p * PAIR, (p + 1) * PAIR)) + refs)
        s_bf = [s_ref[d, :, ln].astype(BF16) for d, c, rows, ln, *_ in inst]
        s_new = [jnp.dot(g_ref[c, :, ln], sb, preferred_element_type=F32) + h_ref[c, :, ln]
                 for (d, c, rows, ln, q_ref, ol_ref, g_ref, h_ref, o_ref), sb in zip(inst, s_bf)]
        o_val = [jnp.dot(q_ref[rows, ln], sb, preferred_element_type=F32) + ol_ref[rows, ln]
                 for (d, c, rows, ln, q_ref, ol_ref, g_ref, h_ref, o_ref), sb in zip(inst, s_bf)]
        for (d, c, rows, ln, q_ref, ol_ref, g_ref, h_ref, o_ref), sn, ov in zip(inst, s_new, o_val):
            s_ref[d, :, ln] = sn
            o_ref[rows, ln] = ov.astype(BF16)


def _rwkv_b(qeff, oloc, g_all, h_all, batch, seq):
    m = batch * seq
    w = RW_WIDTH
    cb = min(8, seq // CHUNK)
    tm = cb * CHUNK
    nb = seq // tm

    def fwd(b, j):
        return b * nb + j

    def bwd(b, j):
        return b * nb + nb - 1 - j

    def specs(d, blk):
        return [
            pl.BlockSpec((None, tm, w), lambda b, j: (d, blk(b, j), 0)),
            pl.BlockSpec((None, tm, w), lambda b, j: (d, blk(b, j), 0)),
            pl.BlockSpec((None, cb, PAIR, w), lambda b, j: (d, blk(b, j), 0, 0)),
            pl.BlockSpec((None, cb, PAIR, w), lambda b, j: (d, blk(b, j), 0, 0)),
        ]

    return pl.pallas_call(
        functools.partial(_rwkv_b_kernel, cb=cb),
        grid=(batch, nb),
        in_specs=specs(0, fwd) + specs(1, bwd),
        out_specs=[pl.BlockSpec((tm, w), lambda b, j: (fwd(b, j), 0)),
                   pl.BlockSpec((tm, w), lambda b, j: (bwd(b, j), 0))],
        out_shape=[jax.ShapeDtypeStruct((m, w), BF16)] * 2,
        scratch_shapes=[pltpu.VMEM((2, PAIR, w), F32)],
        compiler_params=_cparams(("parallel", "arbitrary")),
        name="rwkv_recurrence",
    )(qeff, oloc, g_all, h_all, qeff, oloc, g_all, h_all)


def _mla_prep_kernel(z_ref, c_ref, s_ref, qn_ref, kvn_ref, qa_ref, qb_ref, kk_ref, kvv_ref, hsel_ref,
                     q_out, k_out, v_out, qmax_out, kmax_out, *, scale):
    z = z_ref[...]
    cos = c_ref[...]
    sin = s_ref[...]
    qd = _rms(z[:, :Q_LORA], qn_ref[...]).astype(BF16)
    kvd = _rms(z[:, Q_LORA:Q_LORA + KV_LORA], kvn_ref[...]).astype(BF16)
    o = Q_LORA + KV_LORA
    kr = z[:, o:o + LANES] * cos + z[:, o + LANES:o + 2 * LANES] * sin
    qa = jnp.dot(qd, qa_ref[...], preferred_element_type=F32)
    qb = jnp.dot(qd, qb_ref[...], preferred_element_type=F32)
    kn = jnp.dot(kvd, kk_ref[...], preferred_element_type=F32)
    v_out[...] = lax.dot_general(kvv_ref[...], kvd, (((1,), (1,)), ((), ())),
                                 preferred_element_type=F32).astype(BF16)
    qs, ks = [], []
    for h in range(MLA_HEADS):
        lanes = slice(h * LANES, (h + 1) * LANES)
        qs.append(((qa[:, lanes] * cos + qb[:, lanes] * sin) * scale).astype(BF16))
        ks.append((kn[:, lanes] + kr).astype(BF16))
        q_out[h] = qs[h]
        k_out[h] = ks[h]
    for vals, out in ((qs, qmax_out), (ks, kmax_out)):
        full = jnp.concatenate([v.astype(F32) for v in vals], axis=1)
        n2 = jnp.dot((full * full).astype(BF16), hsel_ref[...], preferred_element_type=F32)
        out[...] = jnp.broadcast_to(jnp.max(n2, axis=0, keepdims=True), (SUBLANES, LANES))


def _mla_prep(z2d, cos, sin, batch, seq, q_norm, kv_norm, q_a, q_b, kv_k, kv_v):
    tm = min(512, seq)
    nt = seq // tm
    hw = MLA_HEADS * LANES
    scale = float((QK_NOPE + QK_ROPE) ** -0.5 * math.log2(math.e))
    full = lambda shape: pl.BlockSpec(shape, lambda b, i: (0,) * len(shape))
    col_blk = RW_COLS // MLA_IN
    assert col_blk * MLA_IN == RW_COLS
    head_sel = (jnp.arange(hw)[:, None] // LANES == jnp.arange(LANES)[None, :]).astype(BF16)
    q, k, v_t, qmax, kmax = pl.pallas_call(
        functools.partial(_mla_prep_kernel, scale=scale),
        grid=(batch, nt),
        in_specs=[
            pl.BlockSpec((tm, MLA_IN), lambda b, i: (b * nt + i, col_blk)),
            pl.BlockSpec((tm, LANES), lambda b, i: (b * nt + i, 0)),
            pl.BlockSpec((tm, LANES), lambda b, i: (b * nt + i, 0)),
            full((1, Q_LORA)), full((1, KV_LORA)), full((Q_LORA, hw)), full((Q_LORA, hw)),
            full((KV_LORA, hw)), full((MLA_WIDTH, KV_LORA)), full((hw, LANES)),
        ],
        out_specs=[
            pl.BlockSpec((None, MLA_HEADS, tm, LANES), lambda b, i: (b, 0, i, 0)),
            pl.BlockSpec((None, MLA_HEADS, tm, LANES), lambda b, i: (b, 0, i, 0)),
            pl.BlockSpec((None, MLA_WIDTH, tm), lambda b, i: (b, 0, i)),
            pl.BlockSpec((None, None, SUBLANES, LANES), lambda b, i: (b, i, 0, 0)),
            pl.BlockSpec((None, None, SUBLANES, LANES), lambda b, i: (b, i, 0, 0)),
        ],
        out_shape=[
            jax.ShapeDtypeStruct((batch, MLA_HEADS, seq, LANES), BF16),
            jax.ShapeDtypeStruct((batch, MLA_HEADS, seq, LANES), BF16),
            jax.ShapeDtypeStruct((batch, MLA_WIDTH, seq), BF16),
            jax.ShapeDtypeStruct((batch, nt, SUBLANES, LANES), F32),
            jax.ShapeDtypeStruct((batch, nt, SUBLANES, LANES), F32),
        ],
        compiler_params=_cparams(("parallel", "parallel")),
        name="mla_prep",
    )(z2d, cos, sin, q_norm, kv_norm, q_a, q_b, kv_k, kv_v, head_sel)
    bound = jnp.sqrt(jnp.max(qmax[:, :, 0, :MLA_HEADS], axis=1)
                     * jnp.max(kmax[:, :, 0, :MLA_HEADS], axis=1)) * NORM_MARGIN
    small = (bound <= FLASH_SAFE_LOG2).reshape(batch, MLA_HEADS // 2, 2).all(axis=-1)
    return q, k, v_t, small.astype(I32).reshape(-1)


def _flash_kernel(small_ref, q_ref, k_ref, vt_ref, o_ref, m_ref, l_ref, acc_ref):
    j = pl.program_id(3)
    small = small_ref[pl.program_id(0) * pl.num_programs(1) + pl.program_id(1)] != 0

    @pl.when(j == 0)
    def _():
        m_ref[...] = jnp.full_like(m_ref, -jnp.inf)
        l_ref[...] = jnp.zeros_like(l_ref)
        acc_ref[...] = jnp.zeros_like(acc_ref)

    tk = k_ref.shape[1]
    sub = min(FLASH_SUB, tk)
    inst = [(h, slice(b * sub, (b + 1) * sub)) for b in range(tk // sub) for h in range(2)]
    rows = [slice(h * V_HEAD, (h + 1) * V_HEAD) for h in range(2)]
    ones = jnp.ones((ONES_ROWS, sub), BF16)

    def scores():
        return [lax.dot_general(k_ref[h, kb, :], q_ref[h], (((1,), (1,)), ((), ())),
                                preferred_element_type=F32) for h, kb in inst]

    def weighted_values(p):
        pv = [jnp.dot(jnp.concatenate([vt_ref[rows[h], kb], ones], axis=0), x,
                      preferred_element_type=F32) for (h, kb), x in zip(inst, p)]
        return [x[:V_HEAD, :] for x in pv], [x[V_HEAD:V_HEAD + 1, :] for x in pv]

    @pl.when(small)
    def _():
        pv, l_loc = weighted_values([jnp.exp2(x).astype(BF16) for x in scores()])
        for h in range(2):
            mine = [i for i, (hh, _) in enumerate(inst) if hh == h]
            l_ref[h:h + 1, :] = l_ref[h:h + 1, :] + sum(l_loc[i] for i in mine)
            acc_ref[rows[h], :] = acc_ref[rows[h], :] + sum(pv[i] for i in mine)

    @pl.when(jnp.logical_not(small))
    def _():
        s = scores()
        m_loc = [jnp.max(x, axis=0, keepdims=True) for x in s]
        pv, l_loc = weighted_values([jnp.exp2(x - m).astype(BF16) for x, m in zip(s, m_loc)])
        for h in range(2):
            mine = [i for i, (hh, _) in enumerate(inst) if hh == h]
            m_prev = m_ref[h:h + 1, :]
            m_new = m_prev
            for i in mine:
                m_new = jnp.maximum(m_new, m_loc[i])
            alpha = jnp.exp2(m_prev - m_new)
            l_new = alpha * l_ref[h:h + 1, :]
            acc = alpha * acc_ref[rows[h], :]
            for i in mine:
                w = jnp.exp2(m_loc[i] - m_new)
                l_new = l_new + w * l_loc[i]
                acc = acc + w * pv[i]
            m_ref[h:h + 1, :] = m_new
            l_ref[h:h + 1, :] = l_new
            acc_ref[rows[h], :] = acc

    @pl.when(j == pl.num_programs(3) - 1)
    def _():
        inv = 1.0 / l_ref[...]
        o_t = jnp.concatenate([acc_ref[:V_HEAD, :] * inv[0:1, :], acc_ref[V_HEAD:, :] * inv[1:2, :]],
                              axis=0)
        o_ref[...] = o_t.T


def _flash(small, q, k, v_t):
    batch, heads, seq, _ = q.shape
    tq = min(FLASH_TQ, seq)
    tk = min(FLASH_TK, seq)
    return pl.pallas_call(
        _flash_kernel,
        grid_spec=pltpu.PrefetchScalarGridSpec(
            num_scalar_prefetch=1,
            grid=(batch, heads // 2, seq // tq, seq // tk),
            in_specs=[
                pl.BlockSpec((None, 2, tq, LANES), lambda b, p, i, j, sm: (b, p, i, 0)),
                pl.BlockSpec((None, 2, tk, LANES), lambda b, p, i, j, sm: (b, p, j, 0)),
                pl.BlockSpec((None, 2 * V_HEAD, tk), lambda b, p, i, j, sm: (b, p, j)),
            ],
            out_specs=pl.BlockSpec((None, tq, LANES), lambda b, p, i, j, sm: (b, i, p)),
            scratch_shapes=[pltpu.VMEM((2, tq), F32), pltpu.VMEM((2, tq), F32),
                            pltpu.VMEM((2 * V_HEAD, tq), F32)],
        ),
        out_shape=jax.ShapeDtypeStruct((batch, seq, MLA_WIDTH), F32),
        compiler_params=_cparams(("parallel", "parallel", "parallel", "arbitrary")),
        name="mla_flash",
    )(small, q, k, v_t)


def _out_kernel(of_ref, ob_ref, bonus_ref, gate_ref, ym_ref, x_ref, hsum_ref, lw_ref, lb_ref,
                on_ref, w_ref, o_ref, ot_ref):
    o = of_ref[...].astype(F32) + ob_ref[...].astype(F32)
    hsum = hsum_ref[...]
    inv_n = 1.0 / HEAD_DIM
    mean = _dot_lsplit(o, hsum, 2) * inv_n
    d = o - mean
    var = _dot_lsplit(d * d, hsum, 2) * inv_n
    y_rw = (d * lax.rsqrt(var + GN_EPS) * lw_ref[...] + lb_ref[...] + bonus_ref[...]) * gate_ref[...]
    y_mla = _rms(ym_ref[...], on_ref[...])
    w = RW_WIDTH
    out = x_ref[...] + _dot(y_rw, w_ref[:w, :]) + _dot(y_mla, w_ref[w:, :])
    o_ref[...] = out
    _rows_to_tiles(ot_ref, out)


def _out_proj(o_f, o_b, bonus, gate, y_mla, x2d, hsum, lnx_w, lnx_b, o_norm, w_out):
    m, d = x2d.shape
    w = RW_WIDTH
    tm = min(512, m)
    row = lambda n: pl.BlockSpec((tm, n), lambda i: (i, 0))
    full = lambda shape: pl.BlockSpec(shape, lambda i: (0,) * len(shape))
    return pl.pallas_call(
        _out_kernel,
        grid=(m // tm,),
        in_specs=[row(w), row(w), row(w), row(w), row(MLA_WIDTH), row(d), full((w, w)),
                  full((1, w)), full((1, w)), full((1, MLA_WIDTH)), full((w + MLA_WIDTH, d))],
        out_specs=[row(d), pl.BlockSpec((tm * SUBLANES, LANES), lambda i: (i, 0))],
        out_shape=[jax.ShapeDtypeStruct((m, d), F32),
                   jax.ShapeDtypeStruct((m * SUBLANES, LANES), F32)],
        compiler_params=_cparams(("parallel",)),
        name="out_proj",
    )(o_f, o_b, bonus, gate, y_mla, x2d, hsum, lnx_w, lnx_b, o_norm, w_out)


def _router_kernel(x_ref, g_ref, rth_ref, rtl_ref, a_ref):
    x_hi, x_lo = _split_bf16(_rms(x_ref[...], g_ref[...]), 2)
    nt = (((1,), (1,)), ((), ()))
    logits = (lax.dot_general(rth_ref[...], x_hi, nt, preferred_element_type=F32)
              + lax.dot_general(rth_ref[...], x_lo, nt, preferred_element_type=F32)
              + lax.dot_general(rtl_ref[...], x_hi, nt, preferred_element_type=F32))
    mx = jnp.max(logits, axis=0, keepdims=True)
    e = jnp.exp(logits - mx)
    a_ref[...] = e / jnp.sum(e, axis=0, keepdims=True)


def _router(x3d, g, router_t):
    batch, seq, d = x3d.shape
    e = router_t.shape[0]
    tm = min(512, seq)
    return pl.pallas_call(
        _router_kernel,
        grid=(batch, seq // tm),
        in_specs=[pl.BlockSpec((None, tm, d), lambda b, i: (b, i, 0)),
                  pl.BlockSpec((1, d), lambda b, i: (0, 0)),
                  pl.BlockSpec((e, d), lambda b, i: (0, 0)),
                  pl.BlockSpec((e, d), lambda b, i: (0, 0))],
        out_specs=pl.BlockSpec((None, e, tm), lambda b, i: (b, 0, i)),
        out_shape=jax.ShapeDtypeStruct((batch, e, seq), F32),
        compiler_params=_cparams(("parallel", "parallel")),
        name="moe_router",
    )(x3d, g, *_hi_lo(router_t))


def _threshold_kernel(a_ref, thr_ref, *, cap):
    bits = pltpu.bitcast(a_ref[...], I32)
    n_e = bits.shape[0]

    def search(i, cur):
        cand = cur | jnp.left_shift(jnp.int32(1), 30 - i)
        cnt = jnp.sum(jnp.where(bits >= cand, 1, 0), axis=1, keepdims=True)
        return jnp.where(cnt >= cap, cand, cur)

    thr_ref[...] = lax.fori_loop(0, 31, search, jnp.zeros((n_e, 1), I32))


def _compact_kernel(a_ref, thr_ref, idx_ref, gate_ref, *, cap):
    a = a_ref[...]
    nb = a.shape[0]
    bits = pltpu.bitcast(a, I32)
    thr = thr_ref[...]
    ri = lax.broadcasted_iota(I32, (LANES, LANES), 0)
    ci = lax.broadcasted_iota(I32, (LANES, LANES), 1)
    upper = jnp.where(ri <= ci, 1.0, 0.0).astype(BF16)
    bi = lax.broadcasted_iota(I32, (nb, nb), 0)
    bj = lax.broadcasted_iota(I32, (nb, nb), 1)
    before = jnp.where(bj < bi, 1.0, 0.0).astype(BF16)

    def total(x):
        return jnp.sum(jnp.sum(x, axis=1, keepdims=True), axis=0, keepdims=True)

    def running(mask):
        within = jnp.dot(mask, upper, preferred_element_type=F32)
        tot = jnp.broadcast_to(within[:, LANES - 1:LANES], (nb, LANES))
        return within, jnp.dot(before, tot.astype(BF16), preferred_element_type=F32)

    gt = bits > thr
    eq = bits == thr
    need = cap - total(jnp.where(gt, 1.0, 0.0))
    w_eq, b_eq = running(jnp.where(eq, 1.0, 0.0).astype(BF16))
    sel = gt | (eq & (w_eq + b_eq <= need))
    sel_b = jnp.where(sel, 1.0, 0.0).astype(BF16)
    within, base = running(sel_b)

    tot_row = lax.dot_general(jnp.ones((SUBLANES, LANES), BF16), sel_b, (((1,), (1,)), ((), ())),
                              preferred_element_type=F32)
    base_row = jnp.dot(tot_row.astype(BF16), jnp.where(bi < bj, 1.0, 0.0).astype(BF16),
                       preferred_element_type=F32)
    c_col = lax.broadcasted_iota(I32, (cap, 1), 0).astype(F32)
    in_block = (base_row[0:1, :] <= c_col) & (c_col < base_row[0:1, :] + tot_row[0:1, :])
    onehot = jnp.where(in_block, 1.0, 0.0).astype(BF16)

    a_parts = _split_bf16(a, 3)
    base_hi = jnp.floor(base * (1.0 / 32.0))
    lane = lax.broadcasted_iota(I32, (1, LANES), 1).astype(F32)
    block_id = lax.broadcasted_iota(I32, (nb, LANES), 0).astype(F32)
    table = jnp.concatenate(
        [within.astype(BF16), sel_b] + a_parts
        + [base_hi.astype(BF16), (base - 32.0 * base_hi).astype(BF16), block_id.astype(BF16)], axis=1)
    g = jnp.dot(onehot, table, preferred_element_type=F32)
    part = lambda k: g[:, k * LANES:(k + 1) * LANES]
    a_c = part(2) + part(3) + part(4)
    slot = lax.broadcasted_iota(I32, (cap, LANES), 0).astype(F32)
    target = slot - (32.0 * part(5) + part(6)) + 1.0
    match = (part(0) == target) & (part(1) > 0.5)
    pos = jnp.dot(jnp.where(match, lane, 0.0).astype(BF16), jnp.ones((LANES, LANES), BF16),
                  preferred_element_type=F32)
    gate_ref[...] = jnp.sum(jnp.where(match, a_c, 0.0), axis=1, keepdims=True)
    idx_ref[...] = (LANES * part(7) + pos)[:, 0:1].astype(I32)


def _select(aff_t, cap):
    batch, e, seq = aff_t.shape
    nb = seq // LANES
    thr = pl.pallas_call(
        functools.partial(_threshold_kernel, cap=cap),
        grid=(batch,),
        in_specs=[pl.BlockSpec((None, e, seq), lambda b: (b, 0, 0))],
        out_specs=pl.BlockSpec((None, e, 1), lambda b: (b, 0, 0)),
        out_shape=jax.ShapeDtypeStruct((batch, e, 1), I32),
        compiler_params=_cparams(("parallel",)),
        name="moe_threshold",
    )(aff_t)
    idx, gate = pl.pallas_call(
        functools.partial(_compact_kernel, cap=cap),
        grid=(batch, e),
        in_specs=[pl.BlockSpec((None, None, nb, LANES), lambda b, j: (b, j, 0, 0)),
                  pl.BlockSpec((None, None, 1, 1), lambda b, j: (b, j, 0, 0))],
        out_specs=[pl.BlockSpec((None, None, cap, 1), lambda b, j: (b, j, 0, 0))] * 2,
        out_shape=[jax.ShapeDtypeStruct((batch, e, cap, 1), I32),
                   jax.ShapeDtypeStruct((batch, e, cap, 1), F32)],
        compiler_params=_cparams(("parallel", "parallel")),
        name="moe_compact",
    )(aff_t.reshape(batch, e, nb, LANES), thr.reshape(batch, e, 1, 1))
    return idx.reshape(-1), gate


def _rows_from_tiles(ref):
    n = ref.shape[0] // SUBLANES
    return jnp.concatenate([ref[pl.ds(s, n, stride=SUBLANES), :] for s in range(SUBLANES)], axis=-1)


def _rows_to_tiles(ref, val):
    n = val.shape[0]
    for s in range(SUBLANES):
        ref[pl.ds(s, n, stride=SUBLANES), :] = val[:, s * LANES:(s + 1) * LANES]


def _tile(r):
    return pl.ds(pl.multiple_of(r * SUBLANES, SUBLANES), SUBLANES)


def _gather_kernel(idx_ref, x_ref, o_ref, *, cap, n_e):
    base = (pl.program_id(0) * n_e + pl.program_id(1)) * cap

    def body(c, carry):
        o_ref[_tile(c), :] = x_ref[_tile(idx_ref[base + c]), :]
        return carry

    lax.fori_loop(0, cap, body, 0, unroll=8)


def _gather(idx_flat, x_tiles, n_e, cap):
    batch, rows, _ = x_tiles.shape
    return pl.pallas_call(
        functools.partial(_gather_kernel, cap=cap, n_e=n_e),
        grid_spec=pltpu.PrefetchScalarGridSpec(
            num_scalar_prefetch=1,
            grid=(batch, n_e),
            in_specs=[pl.BlockSpec((None, rows, LANES), lambda b, e, idx: (b, 0, 0),
                                   pipeline_mode=pl.Buffered(1))],
            out_specs=pl.BlockSpec((None, None, cap * SUBLANES, LANES),
                                   lambda b, e, idx: (b, e, 0, 0)),
        ),
        out_shape=jax.ShapeDtypeStruct((batch, n_e, cap * SUBLANES, LANES), F32),
        compiler_params=_cparams(("parallel", "arbitrary")),
        name="moe_gather",
    )(idx_flat, x_tiles)


def _ffn_kernel(x_ref, gate_ref, g_ref, wg_ref, wu_ref, wd_ref, o_ref, wg_s, wu_s, wd_s):
    @pl.when((pl.program_id(1) == 0) & (pl.program_id(2) == 0))
    def _():
        wg_s[...] = wg_ref[...].astype(BF16)
        wu_s[...] = wu_ref[...].astype(BF16)
        wd_s[...] = wd_ref[...].astype(BF16)

    xn = _rms(_rows_from_tiles(x_ref), g_ref[...]).astype(BF16)
    h1 = jnp.dot(xn, wg_s[...], preferred_element_type=F32)
    h2 = jnp.dot(xn, wu_s[...], preferred_element_type=F32)
    hid = (h1 * _sigmoid(h1) * h2).astype(BF16)
    _rows_to_tiles(o_ref, jnp.dot(hid, wd_s[...], preferred_element_type=F32) * gate_ref[...])


def _expert_ffn(xe, gate_col, g, w_gate, w_up, w_down, layer):
    batch, n_e, rows, _ = xe.shape
    cap = rows // SUBLANES
    d, f = w_gate.shape[2:]
    tc = min(512, cap)
    return pl.pallas_call(
        _ffn_kernel,
        grid=(n_e, batch, cap // tc),
        in_specs=[
            pl.BlockSpec((None, None, tc * SUBLANES, LANES), lambda e, b, c: (b, e, c, 0)),
            pl.BlockSpec((None, None, tc, 1), lambda e, b, c: (b, e, c, 0)),
            pl.BlockSpec((1, d), lambda e, b, c: (0, 0)),
            pl.BlockSpec((None, None, d, f), lambda e, b, c: (layer, e, 0, 0)),
            pl.BlockSpec((None, None, d, f), lambda e, b, c: (layer, e, 0, 0)),
            pl.BlockSpec((None, None, f, d), lambda e, b, c: (layer, e, 0, 0)),
        ],
        out_specs=pl.BlockSpec((None, None, tc * SUBLANES, LANES), lambda e, b, c: (b, e, c, 0)),
        out_shape=jax.ShapeDtypeStruct(xe.shape, F32),
        scratch_shapes=[pltpu.VMEM((d, f), BF16), pltpu.VMEM((d, f), BF16), pltpu.VMEM((f, d), BF16)],
        compiler_params=_cparams(("parallel", "arbitrary", "arbitrary")),
        name="moe_ffn",
    )(xe, gate_col, g, w_gate, w_up, w_down)


SCATTER_GROUP = 8


def _scatter_kernel(idx_ref, y_ref, o_ref, *, cap, n_e):
    e = pl.program_id(1)
    base = (pl.program_id(0) * n_e + e) * cap

    @pl.when(e == 0)
    def _():
        o_ref[...] = jnp.zeros_like(o_ref)

    def body(g, carry):
        c0 = g * SCATTER_GROUP
        rows = [idx_ref[base + c0 + i] for i in range(SCATTER_GROUP)]
        new = [o_ref[_tile(r), :] + y_ref[_tile(c0 + i), :] for i, r in enumerate(rows)]
        for r, v in zip(rows, new):
            o_ref[_tile(r), :] = v
        return carry

    lax.fori_loop(0, cap // SCATTER_GROUP, body, 0)


def _scatter_add(idx_flat, ye, seq):
    batch, n_e, rows, _ = ye.shape
    cap = rows // SUBLANES
    return pl.pallas_call(
        functools.partial(_scatter_kernel, cap=cap, n_e=n_e),
        grid_spec=pltpu.PrefetchScalarGridSpec(
            num_scalar_prefetch=1,
            grid=(batch, n_e),
            in_specs=[pl.BlockSpec((None, None, rows, LANES), lambda b, e, idx: (b, e, 0, 0))],
            out_specs=pl.BlockSpec((None, seq * SUBLANES, LANES), lambda b, e, idx: (b, 0, 0),
                                   pipeline_mode=pl.Buffered(1)),
        ),
        out_shape=jax.ShapeDtypeStruct((batch, seq * SUBLANES, LANES), F32),
        compiler_params=_cparams(("parallel", "arbitrary")),
        name="moe_scatter_add",
    )(idx_flat, ye)


def _ple_kernel(x_ref, d_ref, p_ref, g_ref, wp_ref, wg_ref, fg_ref, o_ref, *, final):
    x = x_ref[...] + _rows_from_tiles(d_ref)
    gate = _sigmoid(_dot(_rms(x, g_ref[...]), wg_ref[...]))
    out = x + _dot(p_ref[...], wp_ref[...]) * gate
    if final:
        out = _rms(out, fg_ref[...])
    o_ref[...] = out


def _ple(x2d, delta_tiles, p_all, layer, g, w_proj, w_gate, final_g, final):
    m, d = x2d.shape
    dp = p_all.shape[1]
    tm = min(512, m)
    first = layer * (m // tm)
    full = lambda shape: pl.BlockSpec(shape, lambda i: (0,) * len(shape))
    return pl.pallas_call(
        functools.partial(_ple_kernel, final=final),
        grid=(m // tm,),
        in_specs=[pl.BlockSpec((tm, d), lambda i: (i, 0)),
                  pl.BlockSpec((tm * SUBLANES, LANES), lambda i: (i, 0)),
                  pl.BlockSpec((tm, dp), lambda i: (first + i, 0)),
                  full((1, d)), full((dp, d)), full((d, d)), full((1, d))],
        out_specs=pl.BlockSpec((tm, d), lambda i: (i, 0)),
        out_shape=jax.ShapeDtypeStruct((m, d), F32),
        compiler_params=_cparams(("parallel",)),
        name="ple_final" if final else "ple",
    )(x2d, delta_tiles, p_all, g, w_proj, w_gate, final_g)


def _hi_lo(w):
    hi = w.astype(BF16)
    return hi, (w - hi.astype(F32)).astype(BF16)


def _rot_cols(w):
    half = QK_ROPE // 2
    return jnp.concatenate([-w[..., half:], w[..., :half]], axis=-1)


def _pad_head(nope, rope):
    lead = (nope if nope is not None else rope).shape[:-1]
    n = nope if nope is not None else jnp.zeros(lead + (QK_NOPE,), F32)
    r = rope if rope is not None else jnp.zeros(lead + (QK_ROPE,), F32)
    return jnp.concatenate([n, r, jnp.zeros(lead + (LANES - QK_NOPE - QK_ROPE,), F32)], axis=-1)


def _block_rows(w_pair):
    z = jnp.zeros_like(w_pair[0])
    return jnp.concatenate([jnp.concatenate([w_pair[0], z], axis=1),
                            jnp.concatenate([z, w_pair[1]], axis=1)], axis=0)


def kernel(x, p, positions, attn_norm, w_in, rw_mu, rw_w0, rw_w_up, rw_a0, rw_a_up, rw_g_up, rw_k_k,
           rw_k_a, rw_r_k, rw_lnx_w, rw_lnx_b, mla_q_norm, mla_q_up, mla_kv_norm, mla_kv_up,
           mla_o_norm, w_out, ffn_norm, router, exp_w_gate, exp_w_up, exp_w_down, ple_norm,
           ple_proj, ple_gate, final_norm):
    batch, seq, d = x.shape
    depth = w_in.shape[0]
    m = batch * seq
    cap = EC_FACTOR * seq // N_EXPERTS
    w = RW_WIDTH

    cos, sin = _rope_tables(positions)
    hsum = (jnp.arange(w)[:, None] // HEAD_DIM == jnp.arange(w)[None, :] // HEAD_DIM).astype(BF16)
    x2d = x.reshape(m, d)

    for i in range(depth):
        w_mla = w_in[i][:, RW_COLS:]
        w_kr = w_mla[:, Q_LORA + KV_LORA:]
        w_ext = jnp.concatenate([w_in[i][:, :RW_COLS], w_mla[:, :Q_LORA + KV_LORA],
                                 _pad_head(None, w_kr), _pad_head(None, _rot_cols(w_kr))],
                                axis=1).astype(BF16)
        q_up = mla_q_up[i].reshape(Q_LORA, MLA_HEADS, QK_NOPE + QK_ROPE)
        q_a = _pad_head(q_up[..., :QK_NOPE], q_up[..., QK_NOPE:]).reshape(Q_LORA, -1).astype(BF16)
        q_b = _pad_head(None, _rot_cols(q_up[..., QK_NOPE:])).reshape(Q_LORA, -1).astype(BF16)
        kv_up = mla_kv_up[i].reshape(KV_LORA, MLA_HEADS, QK_NOPE + V_HEAD)
        kv_k = _pad_head(kv_up[..., :QK_NOPE], None).reshape(KV_LORA, -1).astype(BF16)
        kv_v = kv_up[..., QK_NOPE:].reshape(KV_LORA, MLA_WIDTH).T.astype(BF16)

        z2d = _in_proj(x2d, attn_norm[i][None, :], w_ext)
        qeff, oloc, g_all, h_all, bonus, gate = _rwkv_a(
            z2d, seq, rw_mu[i][None, :], rw_w0[i].reshape(1, 2 * w), _block_rows(rw_w_up[i]),
            rw_a0[i].reshape(1, 2 * w), _block_rows(rw_a_up[i]), rw_g_up[i], rw_k_k[i][None, :],
            rw_k_a[i][None, :], rw_r_k[i].reshape(1, w), hsum)
        o_f, o_b = _rwkv_b(qeff, oloc, g_all, h_all, batch, seq)
        q, k, v_t, small = _mla_prep(z2d, cos, sin, batch, seq, mla_q_norm[i][None, :],
                                     mla_kv_norm[i][None, :], q_a, q_b, kv_k, kv_v)
        y_mla = _flash(small, q, k, v_t).reshape(m, MLA_WIDTH)
        x2d, x_tiles = _out_proj(o_f, o_b, bonus, gate, y_mla, x2d, hsum, rw_lnx_w[i][None, :],
                                 rw_lnx_b[i][None, :], mla_o_norm[i][None, :],
                                 w_out[i].astype(BF16))

        aff_t = _router(x2d.reshape(batch, seq, d), ffn_norm[i][None, :], router[i].T)
        idx_flat, gates = _select(aff_t, cap)
        xe = _gather(idx_flat, x_tiles.reshape(batch, seq * SUBLANES, LANES), N_EXPERTS, cap)
        ye = _expert_ffn(xe, gates, ffn_norm[i][None, :], exp_w_gate, exp_w_up, exp_w_down, i)
        delta = _scatter_add(idx_flat, ye, seq).reshape(m * SUBLANES, LANES)

        x2d = _ple(x2d, delta, p.reshape(depth * m, -1), i, ple_norm[i][None, :], ple_proj[i].astype(BF16),
                   ple_gate[i].astype(BF16), final_norm[None, :], final=(i == depth - 1))

    return x2d.reshape(batch, seq, d)
```

```python
import functools
import math

import jax
import jax.numpy as jnp
from jax import lax
from jax.experimental import pallas as pl
from jax.experimental.pallas import tpu as pltpu

F32 = jnp.float32
BF16 = jnp.bfloat16
I32 = jnp.int32

RW_HEADS = 8
HEAD_DIM = 64
RW_WIDTH = RW_HEADS * HEAD_DIM
RW_COLS = 3 * RW_WIDTH + 2 * 64 + 2 * 64 + 128
MLA_HEADS = 8
QK_NOPE = 64
QK_ROPE = 32
V_HEAD = 64
Q_LORA = 256
KV_LORA = 128
MLA_WIDTH = MLA_HEADS * V_HEAD
MLA_IN = Q_LORA + KV_LORA + 2 * 128
ROPE_THETA = 10000.0
N_EXPERTS = 16
EC_FACTOR = 2
NORM_EPS = 1e-6
GN_EPS = 64e-5

LANES = 128
SUBLANES = 8
CHUNK = 64
CHUNKS_PER_ITER = 4
PAIR = 2 * HEAD_DIM
N_PAIRS = RW_WIDTH // PAIR
VMEM_LIMIT = 56 * 1024 * 1024
FLASH_TQ = 2048
FLASH_TK = 2048
FLASH_SUB = 256
ONES_ROWS = 16
FLASH_SAFE_LOG2 = 40.0
NORM_MARGIN = 1.05


def _cparams(sem):
    return pltpu.CompilerParams(dimension_semantics=sem, vmem_limit_bytes=VMEM_LIMIT)


def _rms(x, g):
    return x * lax.rsqrt(jnp.mean(x * x, axis=-1, keepdims=True) + NORM_EPS) * g


def _sigmoid(x):
    return 1.0 / (1.0 + jnp.exp(-x))


def _dot(a, b):
    return jnp.dot(a.astype(BF16), b.astype(BF16), preferred_element_type=F32)


def _split_bf16(x, parts):
    out = []
    rest = x
    for _ in range(parts):
        hi = rest.astype(BF16)
        out.append(hi)
        rest = rest - hi.astype(F32)
    return out


def _dot_lsplit(a, b_exact, parts):
    acc = None
    for term in _split_bf16(a, parts):
        d = jnp.dot(term, b_exact, preferred_element_type=F32)
        acc = d if acc is None else acc + d
    return acc


def _dot_rsplit(a_exact, b, parts):
    acc = None
    for term in _split_bf16(b, parts):
        d = jnp.dot(a_exact, term, preferred_element_type=F32)
        acc = d if acc is None else acc + d
    return acc


def _dot_tn(a, b):
    return lax.dot_general(a.astype(BF16), b.astype(BF16), (((0,), (0,)), ((), ())),
                           preferred_element_type=F32)


def _dot_nt(a, b):
    return lax.dot_general(a.astype(BF16), b.astype(BF16), (((1,), (1,)), ((), ())),
                           preferred_element_type=F32)


def _rope_kernel(pos_ref, inv_ref, c_ref, s_ref):
    ang = pos_ref[...] * inv_ref[...]
    c_ref[...] = jnp.cos(ang)
    s_ref[...] = jnp.sin(ang)


def _rope_tables(positions):
    b, t = positions.shape
    m = b * t
    inv = ROPE_THETA ** (-jnp.arange(0, QK_ROPE, 2, dtype=F32) / QK_ROPE)
    inv_row = jnp.concatenate([jnp.zeros((QK_NOPE,), F32), inv, inv,
                               jnp.zeros((LANES - QK_NOPE - QK_ROPE,), F32)])[None, :]
    posf = jnp.broadcast_to(positions.astype(F32).reshape(m, 1), (m, LANES))
    tm = min(1024, m)
    return pl.pallas_call(
        _rope_kernel,
        grid=(m // tm,),
        in_specs=[pl.BlockSpec((tm, LANES), lambda i: (i, 0)),
                  pl.BlockSpec((1, LANES), lambda i: (0, 0))],
        out_specs=[pl.BlockSpec((tm, LANES), lambda i: (i, 0))] * 2,
        out_shape=[jax.ShapeDtypeStruct((m, LANES), F32)] * 2,
        compiler_params=_cparams(("parallel",)),
        name="rope_tables",
    )(posf, inv_row)


def _in_kernel(x_ref, g_ref, w_ref, o_ref):
    h = _rms(x_ref[...], g_ref[...])
    o_ref[...] = jnp.dot(h.astype(BF16), w_ref[...], preferred_element_type=F32)


def _in_proj(x2d, g, w_ext):
    m, d = x2d.shape
    n = w_ext.shape[1]
    tm = min(512, m)
    return pl.pallas_call(
        _in_kernel,
        grid=(m // tm,),
        in_specs=[pl.BlockSpec((tm, d), lambda i: (i, 0)),
                  pl.BlockSpec((1, d), lambda i: (0, 0)),
                  pl.BlockSpec((d, n), lambda i: (0, 0))],
        out_specs=pl.BlockSpec((tm, n), lambda i: (i, 0)),
        out_shape=jax.ShapeDtypeStruct((m, n), F32),
        compiler_params=_cparams(("parallel",)),
        name="in_proj",
    )(x2d, g, w_ext)


def _pair_masks():
    i = lax.broadcasted_iota(I32, (PAIR, PAIR), 0)
    j = lax.broadcasted_iota(I32, (PAIR, PAIR), 1)
    same = (i // CHUNK) == (j // CHUNK)
    li = i % CHUNK
    lj = j % CHUNK
    return same, li, lj, i == j


def _chunk_pair(a_t, b_t, k_t, r_t, v, b_h, k_h, g_last, consts):
    n = len(a_t)
    idx = range(n)
    strict = [c[0] for c in consts]
    incl = [c[1] for c in consts]
    levels = [c[2] for c in consts]
    eye, m0, m1 = consts[0][3:6]
    reverse = [c[6] for c in consts]

    def stack(x):
        return jnp.concatenate([jnp.where(m0, x, jnp.zeros_like(x)),
                                jnp.where(m1, x, jnp.zeros_like(x))], axis=0)

    a2, b2, k2, v2, bh2, kh2, r2 = ([stack(x) for x in xs] for xs in (a_t, b_t, k_t, v, b_h, k_h, r_t))
    prod = [_dot_nt(jnp.concatenate([a2[i], r2[i].astype(BF16)], axis=0),
                    jnp.concatenate([b2[i], k2[i]], axis=0)) for i in idx]
    n_mat = [jnp.where(strict[i], prod[i][:PAIR, :PAIR], 0.0) for i in idx]
    m_ak = [jnp.where(strict[i], prod[i][:PAIR, PAIR:], 0.0).astype(BF16) for i in idx]
    m_rb = [jnp.where(incl[i], prod[i][PAIR:, :PAIR], 0.0).astype(BF16) for i in idx]
    m_rk = [jnp.where(incl[i], prod[i][PAIR:, PAIR:], 0.0).astype(BF16) for i in idx]
    mv = [_dot(jnp.concatenate([m_ak[i], m_rk[i]], axis=0), v2[i]) for i in idx]

    x = [jnp.where(eye, 1.0, 0.0) - jnp.where(levels[i][0], n_mat[i], 0.0) for i in idx]
    n_bf = [n_mat[i].astype(BF16) for i in idx]
    zero = jnp.zeros((PAIR, PAIR), BF16)
    for lv in range(1, len(levels[0])):
        s = 2 ** lv
        if s < SUBLANES:
            cx = [_dot(jnp.where(levels[i][lv], n_bf[i], zero), x[i]) for i in idx]
            x = [x[i] - _dot(x[i], cx[i]) for i in idx]
            continue
        blocks = [(r, r + s) for r in range(0, PAIR, s)]
        upd = [[((r % CHUNK) // s) % 2 == (0 if reverse[i] else 1) for r, _ in blocks] for i in idx]

        def take(mat, i):
            return jnp.concatenate([mat[r0:r1] for (r0, r1), u in zip(blocks, upd[i]) if u], axis=0)

        c_h = [take(jnp.where(levels[i][lv], n_mat[i], 0.0), i) for i in idx]
        cx_h = [_dot(c_h[i], x[i]) for i in idx]
        zrows = jnp.zeros((s, PAIR), F32)
        cx = []
        for i in idx:
            it = iter(range(CHUNK // s))
            cx.append(jnp.concatenate(
                [cx_h[i][k * s:(k + 1) * s] if u else zrows
                 for u in upd[i] for k in ([next(it)] if u else [0])], axis=0))
        du = [_dot(take(x[i], i), cx[i]) for i in idx]
        x_new = []
        for i in idx:
            it = iter(range(CHUNK // s))
            x_new.append(jnp.concatenate(
                [x[i][r0:r1] - du[i][k * s:(k + 1) * s] if u else x[i][r0:r1]
                 for (r0, r1), u in zip(blocks, upd[i]) for k in ([next(it)] if u else [0])], axis=0))
        x = x_new

    tw = [_dot(x[i], jnp.concatenate([a2[i], mv[i][:PAIR].astype(BF16)], axis=1))
          for i in idx]
    tw_bf = [t.astype(BF16) for t in tw]
    qo = [jnp.concatenate([r2[i], mv[i][PAIR:]], axis=1) - _dot(m_rb[i], tw_bf[i])
          for i in idx]
    bt = [_dot_tn(bh2[i], tw_bf[i]) for i in idx]
    kv = [_dot_tn(kh2[i], v2[i]) for i in idx]
    out = []
    for i in idx:
        g_mat = jnp.where(eye, g_last[i], 0.0) - bt[i][:, :PAIR]
        h_mat = kv[i] - bt[i][:, PAIR:]
        q = qo[i][:CHUNK] + qo[i][CHUNK:]
        out.append((q[:, :PAIR], q[:, PAIR:], g_mat, h_mat))
    return out


def _rwkv_a_kernel(z_ref, zp_ref, zn_ref, mu_ref, w0_ref, wup_ref, a0_ref, aup_ref,
                   gup_ref, kk_ref, ka_ref, rk_ref, hsum_ref, trif_ref, trib_ref,
                   q_out, ol_out, g_out, h_out, bonus_out, gate_out,
                   at_s, bt_s, kt_s, rt_s, bh_s, kh_s, v_s, gl_s, *, tm, seq):
    i = pl.program_id(0)
    z = z_ref[...]
    has_prev = (i * tm) % seq != 0
    has_next = ((i + 1) * tm) % seq != 0
    prev_row = jnp.where(has_prev, zp_ref[SUBLANES - 1:SUBLANES, :], 0.0)
    next_row = jnp.where(has_next, zn_ref[0:1, :], 0.0)
    sub = lax.broadcasted_iota(I32, (SUBLANES, 1), 0)
    z_dn = pltpu.roll(z, 1, axis=0)
    z_dn = jnp.concatenate([jnp.where(sub == 0, prev_row, z_dn[:SUBLANES]), z_dn[SUBLANES:]], axis=0)
    z_up = pltpu.roll(z, tm - 1, axis=0)
    z_up = jnp.concatenate([z_up[:tm - SUBLANES],
                            jnp.where(sub == SUBLANES - 1, next_row, z_up[tm - SUBLANES:])], axis=0)
    zs = z + mu_ref[...] * (0.5 * (z_dn + z_up) - z)

    w = RW_WIDTH
    r = zs[:, :w]
    k = zs[:, w:2 * w]
    v = zs[:, 2 * w:3 * w]
    wd = zs[:, 3 * w:3 * w + 128]
    ad = zs[:, 3 * w + 128:3 * w + 256]
    gd = zs[:, 3 * w + 256:3 * w + 384]

    hsum = hsum_ref[...]
    w_logit = w0_ref[...] + _dot(jnp.tanh(wd), wup_ref[...])
    lw = -_sigmoid(w_logit) * jnp.exp(jnp.float32(-0.5))
    a = _sigmoid(a0_ref[...] + _dot(ad, aup_ref[...]))
    gate_out[...] = _dot(_sigmoid(gd), gup_ref[...]).astype(BF16)
    kkr = k * kk_ref[...]
    kkn = kkr * jnp.minimum(lax.rsqrt(_dot(kkr * kkr, hsum)), 1e12)
    kd = [k * (1.0 + (a[:, d * w:(d + 1) * w] - 1.0) * ka_ref[...]) for d in range(2)]
    bonus_out[...] = (_dot(r * (0.5 * (kd[0] + kd[1])) * rk_ref[...], hsum) * v).astype(BF16)
    v_s[...] = v.astype(BF16)

    for d, tri_ref in enumerate((trif_ref, trib_ref)):
        cols = slice(d * w, (d + 1) * w)
        lw_d = lw[:, cols]
        cum = _dot_rsplit(tri_ref[...], lw_d, 2)
        ends = [c * CHUNK if d == 1 else (c + 1) * CHUNK - 1 for c in range(tm // CHUNK)]
        tot = jnp.concatenate([jnp.broadcast_to(cum[e:e + 1, :], (CHUNK, w)) for e in ends], axis=0)
        g_inv = jnp.exp(-cum)
        g_end = jnp.exp(tot - cum)
        b = kkn * a[:, cols]
        at_s[d] = (kkn * jnp.exp(cum - lw_d)).astype(BF16)
        bt_s[d] = (b * g_inv).astype(BF16)
        kt_s[d] = (kd[d] * g_inv).astype(BF16)
        rt_s[d] = r * jnp.exp(cum)
        bh_s[d] = (b * g_end).astype(BF16)
        kh_s[d] = (kd[d] * g_end).astype(BF16)
        gl_s[d] = jnp.exp(tot)

    same, li, lj, eye = _pair_masks()
    lane = lax.broadcasted_iota(I32, (1, PAIR), 1)
    m0 = lane < HEAD_DIM
    m1 = lane >= HEAD_DIM
    consts = []
    for reverse in (False, True):
        before = (lj > li) if reverse else (lj < li)
        strict = same & before
        incl = same & (before | (li == lj))
        levels = []
        s = 1
        while s < CHUNK:
            blk = same & ((li // (2 * s)) == (lj // (2 * s)))
            hi_row = (li // s) % 2 == 1
            hi_col = (lj // s) % 2 == 1
            levels.append(blk & ((~hi_row & hi_col) if reverse else (hi_row & ~hi_col)))
            s *= 2
        consts.append((strict, incl, levels, eye, m0, m1, reverse))

    def chunk_body(it, carry):
        inst = []
        for j in range(CHUNKS_PER_ITER):
            c = it * CHUNKS_PER_ITER + j
            r0 = pl.multiple_of(c * CHUNK, CHUNK)
            inst += [(c, r0, pl.ds(r0, CHUNK), d, slice(p * PAIR, (p + 1) * PAIR))
                     for d in range(2) for p in range(N_PAIRS)]
        outs = _chunk_pair(
            [at_s[d, rows, ln] for c, r0, rows, d, ln in inst],
            [bt_s[d, rows, ln] for c, r0, rows, d, ln in inst],
            [kt_s[d, rows, ln] for c, r0, rows, d, ln in inst],
            [rt_s[d, rows, ln] for c, r0, rows, d, ln in inst],
            [v_s[rows, ln] for c, r0, rows, d, ln in inst],
            [bh_s[d, rows, ln] for c, r0, rows, d, ln in inst],
            [kh_s[d, rows, ln] for c, r0, rows, d, ln in inst],
            [gl_s[d, pl.ds(r0, 1), ln] for c, r0, rows, d, ln in inst],
            [consts[d] for c, r0, rows, d, ln in inst])
        for (c, r0, rows, d, ln), (qe, ol, g_mat, h_mat) in zip(inst, outs):
            q_out[d, rows, ln] = qe.astype(BF16)
            ol_out[d, rows, ln] = ol.astype(BF16)
            g_out[d, c, :, ln] = g_mat.astype(BF16)
            h_out[d, c, :, ln] = h_mat.astype(BF16)
        return carry

    lax.fori_loop(0, tm // (CHUNK * CHUNKS_PER_ITER), chunk_body, 0)


def _rwkv_a(z2d, seq, mu, w0, wup, a0, aup, gup, k_k, k_a, r_k, hsum):
    m = z2d.shape[0]
    tm = min(256, seq)
    nc = tm // CHUNK
    w = RW_WIDTH
    full = lambda shape: pl.BlockSpec(shape, lambda i: (0,) * len(shape))
    last8 = m // 8 - 1
    ti = jnp.arange(tm)[:, None]
    tj = jnp.arange(tm)[None, :]
    same_chunk = (ti // CHUNK) == (tj // CHUNK)
    tri_f = (same_chunk & (tj <= ti)).astype(BF16)
    tri_b = (same_chunk & (tj >= ti)).astype(BF16)
    kern = functools.partial(_rwkv_a_kernel, tm=tm, seq=seq)
    return pl.pallas_call(
        kern,
        grid=(m // tm,),
        in_specs=[
            pl.BlockSpec((tm, RW_COLS), lambda i: (i, 0)),
            pl.BlockSpec((8, RW_COLS), lambda i: (jnp.maximum(i * (tm // 8) - 1, 0), 0)),
            pl.BlockSpec((8, RW_COLS), lambda i: (jnp.minimum((i + 1) * (tm // 8), last8), 0)),
            full((1, RW_COLS)), full((1, 2 * w)), full((128, 2 * w)),
            full((1, 2 * w)), full((128, 2 * w)), full((128, w)),
            full((1, w)), full((1, w)), full((1, w)), full((w, w)), full((tm, tm)), full((tm, tm)),
        ],
        out_specs=[
            pl.BlockSpec((2, tm, w), lambda i: (0, i, 0)),
            pl.BlockSpec((2, tm, w), lambda i: (0, i, 0)),
            pl.BlockSpec((2, nc, PAIR, w), lambda i: (0, i, 0, 0)),
            pl.BlockSpec((2, nc, PAIR, w), lambda i: (0, i, 0, 0)),
            pl.BlockSpec((tm, w), lambda i: (i, 0)),
            pl.BlockSpec((tm, w), lambda i: (i, 0)),
        ],
        out_shape=[
            jax.ShapeDtypeStruct((2, m, w), BF16),
            jax.ShapeDtypeStruct((2, m, w), BF16),
            jax.ShapeDtypeStruct((2, m // CHUNK, PAIR, w), BF16),
            jax.ShapeDtypeStruct((2, m // CHUNK, PAIR, w), BF16),
            jax.ShapeDtypeStruct((m, w), BF16),
            jax.ShapeDtypeStruct((m, w), BF16),
        ],
        scratch_shapes=[pltpu.VMEM((2, tm, w), BF16)] * 3 + [pltpu.VMEM((2, tm, w), F32)]
        + [pltpu.VMEM((2, tm, w), BF16)] * 2 + [pltpu.VMEM((tm, w), BF16), pltpu.VMEM((2, tm, w), F32)],
        compiler_params=_cparams(("parallel",)),
        name="rwkv_chunk_local",
    )(z2d, z2d, z2d, mu, w0, wup.astype(BF16), a0, aup.astype(BF16), gup.astype(BF16), k_k, k_a, r_k,
      hsum, tri_f, tri_b)


def _rwkv_b_kernel(qf_ref, olf_ref, gf_ref, hf_ref, qb_ref, olb_ref, gb_ref, hb_ref,
                   of_ref, ob_ref, s_ref, *, cb):
    @pl.when(pl.program_id(1) == 0)
    def _():
        s_ref[...] = jnp.zeros_like(s_ref)

    for step in range(cb):
        inst = []
        for d, refs in enumerate(((qf_ref, olf_ref, gf_ref, hf_ref, of_ref),
                                  (qb_ref, olb_ref, gb_ref, hb_ref, ob_ref))):
            c = cb - 1 - step if d == 1 else step
            for p in range(N_PAIRS):
                inst.append((d, c, slice(c * CHUNK, (c + 1) * CHUNK),
                             slice(p * PAIR, (p + 1) * PAIR)) + refs)
        s_bf = [s_ref[d, :, ln].astype(BF16) for d, c, rows, ln, *_ in inst]
        s_new = [jnp.dot(g_ref[c, :, ln], sb, preferred_element_type=F32) + h_ref[c, :, ln]
                 for (d, c, rows, ln, q_ref, ol_ref, g_ref, h_ref, o_ref), sb in zip(inst, s_bf)]
        o_val = [jnp.dot(q_ref[rows, ln], sb, preferred_element_type=F32) + ol_ref[rows, ln]
                 for (d, c, rows, ln, q_ref, ol_ref, g_ref, h_ref, o_ref), sb in zip(inst, s_bf)]
        for (d, c, rows, ln, q_ref, ol_ref, g_ref, h_ref, o_ref), sn, ov in zip(inst, s_new, o_val):
            s_ref[d, :, ln] = sn
            o_ref[rows, ln] = ov.astype(BF16)


def _rwkv_b(qeff, oloc, g_all, h_all, batch, seq):
    m = batch * seq
    w = RW_WIDTH
    cb = min(8, seq // CHUNK)
    tm = cb * CHUNK
    nb = seq // tm

    def fwd(b, j):
        return b * nb + j

    def bwd(b, j):
        return b * nb + nb - 1 - j

    def specs(d, blk):
        return [
            pl.BlockSpec((None, tm, w), lambda b, j: (d, blk(b, j), 0)),
            pl.BlockSpec((None, tm, w), lambda b, j: (d, blk(b, j), 0)),
            pl.BlockSpec((None, cb, PAIR, w), lambda b, j: (d, blk(b, j), 0, 0)),
            pl.BlockSpec((None, cb, PAIR, w), lambda b, j: (d, blk(b, j), 0, 0)),
        ]

    return pl.pallas_call(
        functools.partial(_rwkv_b_kernel, cb=cb),
        grid=(batch, nb),
        in_specs=specs(0, fwd) + specs(1, bwd),
        out_specs=[pl.BlockSpec((tm, w), lambda b, j: (fwd(b, j), 0)),
                   pl.BlockSpec((tm, w), lambda b, j: (bwd(b, j), 0))],
        out_shape=[jax.ShapeDtypeStruct((m, w), BF16)] * 2,
        scratch_shapes=[pltpu.VMEM((2, PAIR, w), F32)],
        compiler_params=_cparams(("parallel", "arbitrary")),
        name="rwkv_recurrence",
    )(qeff, oloc, g_all, h_all, qeff, oloc, g_all, h_all)


def _mla_prep_kernel(z_ref, c_ref, s_ref, qn_ref, kvn_ref, qa_ref, qb_ref, kk_ref, kvv_ref, hsel_ref,
                     q_out, k_out, v_out, qmax_out, kmax_out, *, scale):
    z = z_ref[...]
    cos = c_ref[...]
    sin = s_ref[...]
    qd = _rms(z[:, :Q_LORA], qn_ref[...]).astype(BF16)
    kvd = _rms(z[:, Q_LORA:Q_LORA + KV_LORA], kvn_ref[...]).astype(BF16)
    o = Q_LORA + KV_LORA
    kr = z[:, o:o + LANES] * cos + z[:, o + LANES:o + 2 * LANES] * sin
    qa = jnp.dot(qd, qa_ref[...], preferred_element_type=F32)
    qb = jnp.dot(qd, qb_ref[...], preferred_element_type=F32)
    kn = jnp.dot(kvd, kk_ref[...], preferred_element_type=F32)
    v_out[...] = lax.dot_general(kvv_ref[...], kvd, (((1,), (1,)), ((), ())),
                                 preferred_element_type=F32).astype(BF16)
    qs, ks = [], []
    for h in range(MLA_HEADS):
        lanes = slice(h * LANES, (h + 1) * LANES)
        qs.append(((qa[:, lanes] * cos + qb[:, lanes] * sin) * scale).astype(BF16))
        ks.append((kn[:, lanes] + kr).astype(BF16))
        q_out[h] = qs[h]
        k_out[h] = ks[h]
    for vals, out in ((qs, qmax_out), (ks, kmax_out)):
        full = jnp.concatenate([v.astype(F32) for v in vals], axis=1)
        n2 = jnp.dot((full * full).astype(BF16), hsel_ref[...], preferred_element_type=F32)
        out[...] = jnp.broadcast_to(jnp.max(n2, axis=0, keepdims=True), (SUBLANES, LANES))


def _mla_prep(z2d, cos, sin, batch, seq, q_norm, kv_norm, q_a, q_b, kv_k, kv_v):
    tm = min(512, seq)
    nt = seq // tm
    hw = MLA_HEADS * LANES
    scale = float((QK_NOPE + QK_ROPE) ** -0.5 * math.log2(math.e))
    full = lambda shape: pl.BlockSpec(shape, lambda b, i: (0,) * len(shape))
    col_blk = RW_COLS // MLA_IN
    assert col_blk * MLA_IN == RW_COLS
    head_sel = (jnp.arange(hw)[:, None] // LANES == jnp.arange(LANES)[None, :]).astype(BF16)
    q, k, v_t, qmax, kmax = pl.pallas_call(
        functools.partial(_mla_prep_kernel, scale=scale),
        grid=(batch, nt),
        in_specs=[
            pl.BlockSpec((tm, MLA_IN), lambda b, i: (b * nt + i, col_blk)),
            pl.BlockSpec((tm, LANES), lambda b, i: (b * nt + i, 0)),
            pl.BlockSpec((tm, LANES), lambda b, i: (b * nt + i, 0)),
            full((1, Q_LORA)), full((1, KV_LORA)), full((Q_LORA, hw)), full((Q_LORA, hw)),
            full((KV_LORA, hw)), full((MLA_WIDTH, KV_LORA)), full((hw, LANES)),
        ],
        out_specs=[
            pl.BlockSpec((None, MLA_HEADS, tm, LANES), lambda b, i: (b, 0, i, 0)),
            pl.BlockSpec((None, MLA_HEADS, tm, LANES), lambda b, i: (b, 0, i, 0)),
            pl.BlockSpec((None, MLA_WIDTH, tm), lambda b, i: (b, 0, i)),
            pl.BlockSpec((None, None, SUBLANES, LANES), lambda b, i: (b, i, 0, 0)),
            pl.BlockSpec((None, None, SUBLANES, LANES), lambda b, i: (b, i, 0, 0)),
        ],
        out_shape=[
            jax.ShapeDtypeStruct((batch, MLA_HEADS, seq, LANES), BF16),
            jax.ShapeDtypeStruct((batch, MLA_HEADS, seq, LANES), BF16),
            jax.ShapeDtypeStruct((batch, MLA_WIDTH, seq), BF16),
            jax.ShapeDtypeStruct((batch, nt, SUBLANES, LANES), F32),
            jax.ShapeDtypeStruct((batch, nt, SUBLANES, LANES), F32),
        ],
        compiler_params=_cparams(("parallel", "parallel")),
        name="mla_prep",
    )(z2d, cos, sin, q_norm, kv_norm, q_a, q_b, kv_k, kv_v, head_sel)
    bound = jnp.sqrt(jnp.max(qmax[:, :, 0, :MLA_HEADS], axis=1)
                     * jnp.max(kmax[:, :, 0, :MLA_HEADS], axis=1)) * NORM_MARGIN
    small = (bound <= FLASH_SAFE_LOG2).reshape(batch, MLA_HEADS // 2, 2).all(axis=-1)
    return q, k, v_t, small.astype(I32).reshape(-1)


def _flash_kernel(small_ref, q_ref, k_ref, vt_ref, o_ref, m_ref, l_ref, acc_ref):
    j = pl.program_id(3)
    small = small_ref[pl.program_id(0) * pl.num_programs(1) + pl.program_id(1)] != 0

    @pl.when(j == 0)
    def _():
        m_ref[...] = jnp.full_like(m_ref, -jnp.inf)
        l_ref[...] = jnp.zeros_like(l_ref)
        acc_ref[...] = jnp.zeros_like(acc_ref)

    tk = k_ref.shape[1]
    sub = min(FLASH_SUB, tk)
    inst = [(h, slice(b * sub, (b + 1) * sub)) for b in range(tk // sub) for h in range(2)]
    rows = [slice(h * V_HEAD, (h + 1) * V_HEAD) for h in range(2)]
    ones = jnp.ones((ONES_ROWS, sub), BF16)

    def scores():
        return [lax.dot_general(k_ref[h, kb, :], q_ref[h], (((1,), (1,)), ((), ())),
                                preferred_element_type=F32) for h, kb in inst]

    def weighted_values(p):
        pv = [jnp.dot(jnp.concatenate([vt_ref[rows[h], kb], ones], axis=0), x,
                      preferred_element_type=F32) for (h, kb), x in zip(inst, p)]
        return [x[:V_HEAD, :] for x in pv], [x[V_HEAD:V_HEAD + 1, :] for x in pv]

    @pl.when(small)
    def _():
        pv, l_loc = weighted_values([jnp.exp2(x).astype(BF16) for x in scores()])
        for h in range(2):
            mine = [i for i, (hh, _) in enumerate(inst) if hh == h]
            l_ref[h:h + 1, :] = l_ref[h:h + 1, :] + sum(l_loc[i] for i in mine)
            acc_ref[rows[h], :] = acc_ref[rows[h], :] + sum(pv[i] for i in mine)

    @pl.when(jnp.logical_not(small))
    def _():
        s = scores()
        m_loc = [jnp.max(x, axis=0, keepdims=True) for x in s]
        pv, l_loc = weighted_values([jnp.exp2(x - m).astype(BF16) for x, m in zip(s, m_loc)])
        for h in range(2):
            mine = [i for i, (hh, _) in enumerate(inst) if hh == h]
            m_prev = m_ref[h:h + 1, :]
            m_new = m_prev
            for i in mine:
                m_new = jnp.maximum(m_new, m_loc[i])
            alpha = jnp.exp2(m_prev - m_new)
            l_new = alpha * l_ref[h:h + 1, :]
            acc = alpha * acc_ref[rows[h], :]
            for i in mine:
                w = jnp.exp2(m_loc[i] - m_new)
                l_new = l_new + w * l_loc[i]
                acc = acc + w * pv[i]
            m_ref[h:h + 1, :] = m_new
            l_ref[h:h + 1, :] = l_new
            acc_ref[rows[h], :] = acc

    @pl.when(j == pl.num_programs(3) - 1)
    def _():
        inv = 1.0 / l_ref[...]
        o_t = jnp.concatenate([acc_ref[:V_HEAD, :] * inv[0:1, :], acc_ref[V_HEAD:, :] * inv[1:2, :]],
                              axis=0)
        o_ref[...] = o_t.T


def _flash(small, q, k, v_t):
    batch, heads, seq, _ = q.shape
    tq = min(FLASH_TQ, seq)
    tk = min(FLASH_TK, seq)
    return pl.pallas_call(
        _flash_kernel,
        grid_spec=pltpu.PrefetchScalarGridSpec(
            num_scalar_prefetch=1,
            grid=(batch, heads // 2, seq // tq, seq // tk),
            in_specs=[
                pl.BlockSpec((None, 2, tq, LANES), lambda b, p, i, j, sm: (b, p, i, 0)),
                pl.BlockSpec((None, 2, tk, LANES), lambda b, p, i, j, sm: (b, p, j, 0)),
                pl.BlockSpec((None, 2 * V_HEAD, tk), lambda b, p, i, j, sm: (b, p, j)),
            ],
            out_specs=pl.BlockSpec((None, tq, LANES), lambda b, p, i, j, sm: (b, i, p)),
            scratch_shapes=[pltpu.VMEM((2, tq), F32), pltpu.VMEM((2, tq), F32),
                            pltpu.VMEM((2 * V_HEAD, tq), F32)],
        ),
        out_shape=jax.ShapeDtypeStruct((batch, seq, MLA_WIDTH), F32),
        compiler_params=_cparams(("parallel", "parallel", "parallel", "arbitrary")),
        name="mla_flash",
    )(small, q, k, v_t)


def _out_kernel(of_ref, ob_ref, bonus_ref, gate_ref, ym_ref, x_ref, hsum_ref, lw_ref, lb_ref,
                on_ref, w_ref, o_ref, ot_ref):
    o = of_ref[...].astype(F32) + ob_ref[...].astype(F32)
    hsum = hsum_ref[...]
    inv_n = 1.0 / HEAD_DIM
    mean = _dot_lsplit(o, hsum, 2) * inv_n
    d = o - mean
    var = _dot_lsplit(d * d, hsum, 2) * inv_n
    y_rw = (d * lax.rsqrt(var + GN_EPS) * lw_ref[...] + lb_ref[...] + bonus_ref[...]) * gate_ref[...]
    y_mla = _rms(ym_ref[...], on_ref[...])
    w = RW_WIDTH
    out = x_ref[...] + _dot(y_rw, w_ref[:w, :]) + _dot(y_mla, w_ref[w:, :])
    o_ref[...] = out
    _rows_to_tiles(ot_ref, out)


def _out_proj(o_f, o_b, bonus, gate, y_mla, x2d, hsum, lnx_w, lnx_b, o_norm, w_out):
    m, d = x2d.shape
    w = RW_WIDTH
    tm = min(512, m)
    row = lambda n: pl.BlockSpec((tm, n), lambda i: (i, 0))
    full = lambda shape: pl.BlockSpec(shape, lambda i: (0,) * len(shape))
    return pl.pallas_call(
        _out_kernel,
        grid=(m // tm,),
        in_specs=[row(w), row(w), row(w), row(w), row(MLA_WIDTH), row(d), full((w, w)),
                  full((1, w)), full((1, w)), full((1, MLA_WIDTH)), full((w + MLA_WIDTH, d))],
        out_specs=[row(d), pl.BlockSpec((tm * SUBLANES, LANES), lambda i: (i, 0))],
        out_shape=[jax.ShapeDtypeStruct((m, d), F32),
                   jax.ShapeDtypeStruct((m * SUBLANES, LANES), F32)],
        compiler_params=_cparams(("parallel",)),
        name="out_proj",
    )(o_f, o_b, bonus, gate, y_mla, x2d, hsum, lnx_w, lnx_b, o_norm, w_out)


def _router_kernel(x_ref, g_ref, rth_ref, rtl_ref, a_ref):
    x_hi, x_lo = _split_bf16(_rms(x_ref[...], g_ref[...]), 2)
    nt = (((1,), (1,)), ((), ()))
    logits = (lax.dot_general(rth_ref[...], x_hi, nt, preferred_element_type=F32)
              + lax.dot_general(rth_ref[...], x_lo, nt, preferred_element_type=F32)
              + lax.dot_general(rtl_ref[...], x_hi, nt, preferred_element_type=F32))
    mx = jnp.max(logits, axis=0, keepdims=True)
    e = jnp.exp(logits - mx)
    a_ref[...] = e / jnp.sum(e, axis=0, keepdims=True)


def _router(x3d, g, router_t):
    batch, seq, d = x3d.shape
    e = router_t.shape[0]
    tm = min(512, seq)
    return pl.pallas_call(
        _router_kernel,
        grid=(batch, seq // tm),
        in_specs=[pl.BlockSpec((None, tm, d), lambda b, i: (b, i, 0)),
                  pl.BlockSpec((1, d), lambda b, i: (0, 0)),
                  pl.BlockSpec((e, d), lambda b, i: (0, 0)),
                  pl.BlockSpec((e, d), lambda b, i: (0, 0))],
        out_specs=pl.BlockSpec((None, e, tm), lambda b, i: (b, 0, i)),
        out_shape=jax.ShapeDtypeStruct((batch, e, seq), F32),
        compiler_params=_cparams(("parallel", "parallel")),
        name="moe_router",
    )(x3d, g, *_hi_lo(router_t))


def _threshold_kernel(a_ref, thr_ref, *, cap):
    bits = pltpu.bitcast(a_ref[...], I32)
    n_e = bits.shape[0]

    def search(i, cur):
        cand = cur | jnp.left_shift(jnp.int32(1), 30 - i)
        cnt = jnp.sum(jnp.where(bits >= cand, 1, 0), axis=1, keepdims=True)
        return jnp.where(cnt >= cap, cand, cur)

    thr_ref[...] = lax.fori_loop(0, 31, search, jnp.zeros((n_e, 1), I32))


def _compact_kernel(a_ref, thr_ref, idx_ref, gate_ref, *, cap):
    a = a_ref[...]
    nb = a.shape[0]
    bits = pltpu.bitcast(a, I32)
    thr = thr_ref[...]
    ri = lax.broadcasted_iota(I32, (LANES, LANES), 0)
    ci = lax.broadcasted_iota(I32, (LANES, LANES), 1)
    upper = jnp.where(ri <= ci, 1.0, 0.0).astype(BF16)
    bi = lax.broadcasted_iota(I32, (nb, nb), 0)
    bj = lax.broadcasted_iota(I32, (nb, nb), 1)
    before = jnp.where(bj < bi, 1.0, 0.0).astype(BF16)

    def total(x):
        return jnp.sum(jnp.sum(x, axis=1, keepdims=True), axis=0, keepdims=True)

    def running(mask):
        within = jnp.dot(mask, upper, preferred_element_type=F32)
        tot = jnp.broadcast_to(within[:, LANES - 1:LANES], (nb, LANES))
        return within, jnp.dot(before, tot.astype(BF16), preferred_element_type=F32)

    gt = bits > thr
    eq = bits == thr
    need = cap - total(jnp.where(gt, 1.0, 0.0))
    w_eq, b_eq = running(jnp.where(eq, 1.0, 0.0).astype(BF16))
    sel = gt | (eq & (w_eq + b_eq <= need))
    sel_b = jnp.where(sel, 1.0, 0.0).astype(BF16)
    within, base = running(sel_b)

    tot_row = lax.dot_general(jnp.ones((SUBLANES, LANES), BF16), sel_b, (((1,), (1,)), ((), ())),
                              preferred_element_type=F32)
    base_row = jnp.dot(tot_row.astype(BF16), jnp.where(bi < bj, 1.0, 0.0).astype(BF16),
                       preferred_element_type=F32)
    c_col = lax.broadcasted_iota(I32, (cap, 1), 0).astype(F32)
    in_block = (base_row[0:1, :] <= c_col) & (c_col < base_row[0:1, :] + tot_row[0:1, :])
    onehot = jnp.where(in_block, 1.0, 0.0).astype(BF16)

    a_parts = _split_bf16(a, 3)
    base_hi = jnp.floor(base * (1.0 / 32.0))
    lane = lax.broadcasted_iota(I32, (1, LANES), 1).astype(F32)
    block_id = lax.broadcasted_iota(I32, (nb, LANES), 0).astype(F32)
    table = jnp.concatenate(
        [within.astype(BF16), sel_b] + a_parts
        + [base_hi.astype(BF16), (base - 32.0 * base_hi).astype(BF16), block_id.astype(BF16)], axis=1)
    g = jnp.dot(onehot, table, preferred_element_type=F32)
    part = lambda k: g[:, k * LANES:(k + 1) * LANES]
    a_c = part(2) + part(3) + part(4)
    slot = lax.broadcasted_iota(I32, (cap, LANES), 0).astype(F32)
    target = slot - (32.0 * part(5) + part(6)) + 1.0
    match = (part(0) == target) & (part(1) > 0.5)
    pos = jnp.dot(jnp.where(match, lane, 0.0).astype(BF16), jnp.ones((LANES, LANES), BF16),
                  preferred_element_type=F32)
    gate_ref[...] = jnp.sum(jnp.where(match, a_c, 0.0), axis=1, keepdims=True)
    idx_ref[...] = (LANES * part(7) + pos)[:, 0:1].astype(I32)


def _select(aff_t, cap):
    batch, e, seq = aff_t.shape
    nb = seq // LANES
    thr = pl.pallas_call(
        functools.partial(_threshold_kernel, cap=cap),
        grid=(batch,),
        in_specs=[pl.BlockSpec((None, e, seq), lambda b: (b, 0, 0))],
        out_specs=pl.BlockSpec((None, e, 1), lambda b: (b, 0, 0)),
        out_shape=jax.ShapeDtypeStruct((batch, e, 1), I32),
        compiler_params=_cparams(("parallel",)),
        name="moe_threshold",
    )(aff_t)
    idx, gate = pl.pallas_call(
        functools.partial(_compact_kernel, cap=cap),
        grid=(batch, e),
        in_specs=[pl.BlockSpec((None, None, nb, LANES), lambda b, j: (b, j, 0, 0)),
                  pl.BlockSpec((None, None, 1, 1), lambda b, j: (b, j, 0, 0))],
        out_specs=[pl.BlockSpec((None, None, cap, 1), lambda b, j: (b, j, 0, 0))] * 2,
        out_shape=[jax.ShapeDtypeStruct((batch, e, cap, 1), I32),
                   jax.ShapeDtypeStruct((batch, e, cap, 1), F32)],
        compiler_params=_cparams(("parallel", "parallel")),
        name="moe_compact",
    )(aff_t.reshape(batch, e, nb, LANES), thr.reshape(batch, e, 1, 1))
    return idx.reshape(-1), gate


def _rows_from_tiles(ref):
    n = ref.shape[0] // SUBLANES
    return jnp.concatenate([ref[pl.ds(s, n, stride=SUBLANES), :] for s in range(SUBLANES)], axis=-1)


def _rows_to_tiles(ref, val):
    n = val.shape[0]
    for s in range(SUBLANES):
        ref[pl.ds(s, n, stride=SUBLANES), :] = val[:, s * LANES:(s + 1) * LANES]


def _tile(r):
    return pl.ds(pl.multiple_of(r * SUBLANES, SUBLANES), SUBLANES)


def _gather_kernel(idx_ref, x_ref, o_ref, *, cap, n_e):
    base = (pl.program_id(0) * n_e + pl.program_id(1)) * cap

    def body(c, carry):
        o_ref[_tile(c), :] = x_ref[_tile(idx_ref[base + c]), :]
        return carry

    lax.fori_loop(0, cap, body, 0, unroll=8)


def _gather(idx_flat, x_tiles, n_e, cap):
    batch, rows, _ = x_tiles.shape
    return pl.pallas_call(
        functools.partial(_gather_kernel, cap=cap, n_e=n_e),
        grid_spec=pltpu.PrefetchScalarGridSpec(
            num_scalar_prefetch=1,
            grid=(batch, n_e),
            in_specs=[pl.BlockSpec((None, rows, LANES), lambda b, e, idx: (b, 0, 0),
                                   pipeline_mode=pl.Buffered(1))],
            out_specs=pl.BlockSpec((None, None, cap * SUBLANES, LANES),
                                   lambda b, e, idx: (b, e, 0, 0)),
        ),
        out_shape=jax.ShapeDtypeStruct((batch, n_e, cap * SUBLANES, LANES), F32),
        compiler_params=_cparams(("parallel", "arbitrary")),
        name="moe_gather",
    )(idx_flat, x_tiles)


def _ffn_kernel(x_ref, gate_ref, g_ref, wg_ref, wu_ref, wd_ref, o_ref, wg_s, wu_s, wd_s):
    @pl.when((pl.program_id(1) == 0) & (pl.program_id(2) == 0))
    def _():
        wg_s[...] = wg_ref[...].astype(BF16)
        wu_s[...] = wu_ref[...].astype(BF16)
        wd_s[...] = wd_ref[...].astype(BF16)

    xn = _rms(_rows_from_tiles(x_ref), g_ref[...]).astype(BF16)
    h1 = jnp.dot(xn, wg_s[...], preferred_element_type=F32)
    h2 = jnp.dot(xn, wu_s[...], preferred_element_type=F32)
    hid = (h1 * _sigmoid(h1) * h2).astype(BF16)
    _rows_to_tiles(o_ref, jnp.dot(hid, wd_s[...], preferred_element_type=F32) * gate_ref[...])


def _expert_ffn(xe, gate_col, g, w_gate, w_up, w_down, layer):
    batch, n_e, rows, _ = xe.shape
    cap = rows // SUBLANES
    d, f = w_gate.shape[2:]
    tc = min(512, cap)
    return pl.pallas_call(
        _ffn_kernel,
        grid=(n_e, batch, cap // tc),
        in_specs=[
            pl.BlockSpec((None, None, tc * SUBLANES, LANES), lambda e, b, c: (b, e, c, 0)),
            pl.BlockSpec((None, None, tc, 1), lambda e, b, c: (b, e, c, 0)),
            pl.BlockSpec((1, d), lambda e, b, c: (0, 0)),
            pl.BlockSpec((None, None, d, f), lambda e, b, c: (layer, e, 0, 0)),
            pl.BlockSpec((None, None, d, f), lambda e, b, c: (layer, e, 0, 0)),
            pl.BlockSpec((None, None, f, d), lambda e, b, c: (layer, e, 0, 0)),
        ],
        out_specs=pl.BlockSpec((None, None, tc * SUBLANES, LANES), lambda e, b, c: (b, e, c, 0)),
        out_shape=jax.ShapeDtypeStruct(xe.shape, F32),
        scratch_shapes=[pltpu.VMEM((d, f), BF16), pltpu.VMEM((d, f), BF16), pltpu.VMEM((f, d), BF16)],
        compiler_params=_cparams(("parallel", "arbitrary", "arbitrary")),
        name="moe_ffn",
    )(xe, gate_col, g, w_gate, w_up, w_down)


SCATTER_GROUP = 8


def _scatter_kernel(idx_ref, y_ref, o_ref, *, cap, n_e):
    e = pl.program_id(1)
    base = (pl.program_id(0) * n_e + e) * cap

    @pl.when(e == 0)
    def _():
        o_ref[...] = jnp.zeros_like(o_ref)

    def body(g, carry):
        c0 = g * SCATTER_GROUP
        rows = [idx_ref[base + c0 + i] for i in range(SCATTER_GROUP)]
        new = [o_ref[_tile(r), :] + y_ref[_tile(c0 + i), :] for i, r in enumerate(rows)]
        for r, v in zip(rows, new):
            o_ref[_tile(r), :] = v
        return carry

    lax.fori_loop(0, cap // SCATTER_GROUP, body, 0)


def _scatter_add(idx_flat, ye, seq):
    batch, n_e, rows, _ = ye.shape
    cap = rows // SUBLANES
    return pl.pallas_call(
        functools.partial(_scatter_kernel, cap=cap, n_e=n_e),
        grid_spec=pltpu.PrefetchScalarGridSpec(
            num_scalar_prefetch=1,
            grid=(batch, n_e),
            in_specs=[pl.BlockSpec((None, None, rows, LANES), lambda b, e, idx: (b, e, 0, 0))],
            out_specs=pl.BlockSpec((None, seq * SUBLANES, LANES), lambda b, e, idx: (b, 0, 0),
                                   pipeline_mode=pl.Buffered(1)),
        ),
        out_shape=jax.ShapeDtypeStruct((batch, seq * SUBLANES, LANES), F32),
        compiler_params=_cparams(("parallel", "arbitrary")),
        name="moe_scatter_add",
    )(idx_flat, ye)


def _ple_kernel(x_ref, d_ref, p_ref, g_ref, wp_ref, wg_ref, fg_ref, o_ref, *, final):
    x = x_ref[...] + _rows_from_tiles(d_ref)
    gate = _sigmoid(_dot(_rms(x, g_ref[...]), wg_ref[...]))
    out = x + _dot(p_ref[...], wp_ref[...]) * gate
    if final:
        out = _rms(out, fg_ref[...])
    o_ref[...] = out


def _ple(x2d, delta_tiles, p_all, layer, g, w_proj, w_gate, final_g, final):
    m, d = x2d.shape
    dp = p_all.shape[1]
    tm = min(512, m)
    first = layer * (m // tm)
    full = lambda shape: pl.BlockSpec(shape, lambda i: (0,) * len(shape))
    return pl.pallas_call(
        functools.partial(_ple_kernel, final=final),
        grid=(m // tm,),
        in_specs=[pl.BlockSpec((tm, d), lambda i: (i, 0)),
                  pl.BlockSpec((tm * SUBLANES, LANES), lambda i: (i, 0)),
                  pl.BlockSpec((tm, dp), lambda i: (first + i, 0)),
                  full((1, d)), full((dp, d)), full((d, d)), full((1, d))],
        out_specs=pl.BlockSpec((tm, d), lambda i: (i, 0)),
        out_shape=jax.ShapeDtypeStruct((m, d), F32),
        compiler_params=_cparams(("parallel",)),
        name="ple_final" if final else "ple",
    )(x2d, delta_tiles, p_all, g, w_proj, w_gate, final_g)


def _hi_lo(w):
    hi = w.astype(BF16)
    return hi, (w - hi.astype(F32)).astype(BF16)


def _rot_cols(w):
    half = QK_ROPE // 2
    return jnp.concatenate([-w[..., half:], w[..., :half]], axis=-1)


def _pad_head(nope, rope):
    lead = (nope if nope is not None else rope).shape[:-1]
    n = nope if nope is not None else jnp.zeros(lead + (QK_NOPE,), F32)
    r = rope if rope is not None else jnp.zeros(lead + (QK_ROPE,), F32)
    return jnp.concatenate([n, r, jnp.zeros(lead + (LANES - QK_NOPE - QK_ROPE,), F32)], axis=-1)


def _block_rows(w_pair):
    z = jnp.zeros_like(w_pair[0])
    return jnp.concatenate([jnp.concatenate([w_pair[0], z], axis=1),
                            jnp.concatenate([z, w_pair[1]], axis=1)], axis=0)


def kernel(x, p, positions, attn_norm, w_in, rw_mu, rw_w0, rw_w_up, rw_a0, rw_a_up, rw_g_up, rw_k_k,
           rw_k_a, rw_r_k, rw_lnx_w, rw_lnx_b, mla_q_norm, mla_q_up, mla_kv_norm, mla_kv_up,
           mla_o_norm, w_out, ffn_norm, router, exp_w_gate, exp_w_up, exp_w_down, ple_norm,
           ple_proj, ple_gate, final_norm):
    batch, seq, d = x.shape
    depth = w_in.shape[0]
    m = batch * seq
    cap = EC_FACTOR * seq // N_EXPERTS
    w = RW_WIDTH

    cos, sin = _rope_tables(positions)
    hsum = (jnp.arange(w)[:, None] // HEAD_DIM == jnp.arange(w)[None, :] // HEAD_DIM).astype(BF16)
    x2d = x.reshape(m, d)

    for i in range(depth):
        w_mla = w_in[i][:, RW_COLS:]
        w_kr = w_mla[:, Q_LORA + KV_LORA:]
        w_ext = jnp.concatenate([w_in[i][:, :RW_COLS], w_mla[:, :Q_LORA + KV_LORA],
                                 _pad_head(None, w_kr), _pad_head(None, _rot_cols(w_kr))],
                                axis=1).astype(BF16)
        q_up = mla_q_up[i].reshape(Q_LORA, MLA_HEADS, QK_NOPE + QK_ROPE)
        q_a = _pad_head(q_up[..., :QK_NOPE], q_up[..., QK_NOPE:]).reshape(Q_LORA, -1).astype(BF16)
        q_b = _pad_head(None, _rot_cols(q_up[..., QK_NOPE:])).reshape(Q_LORA, -1).astype(BF16)
        kv_up = mla_kv_up[i].reshape(KV_LORA, MLA_HEADS, QK_NOPE + V_HEAD)
        kv_k = _pad_head(kv_up[..., :QK_NOPE], None).reshape(KV_LORA, -1).astype(BF16)
        kv_v = kv_up[..., QK_NOPE:].reshape(KV_LORA, MLA_WIDTH).T.astype(BF16)

        z2d = _in_proj(x2d, attn_norm[i][None, :], w_ext)
        qeff, oloc, g_all, h_all, bonus, gate = _rwkv_a(
            z2d, seq, rw_mu[i][None, :], rw_w0[i].reshape(1, 2 * w), _block_rows(rw_w_up[i]),
            rw_a0[i].reshape(1, 2 * w), _block_rows(rw_a_up[i]), rw_g_up[i], rw_k_k[i][None, :],
            rw_k_a[i][None, :], rw_r_k[i].reshape(1, w), hsum)
        o_f, o_b = _rwkv_b(qeff, oloc, g_all, h_all, batch, seq)
        q, k, v_t, small = _mla_prep(z2d, cos, sin, batch, seq, mla_q_norm[i][None, :],
                                     mla_kv_norm[i][None, :], q_a, q_b, kv_k, kv_v)
        y_mla = _flash(small, q, k, v_t).reshape(m, MLA_WIDTH)
        x2d, x_tiles = _out_proj(o_f, o_b, bonus, gate, y_mla, x2d, hsum, rw_lnx_w[i][None, :],
                                 rw_lnx_b[i][None, :], mla_o_norm[i][None, :],
                                 w_out[i].astype(BF16))

        aff_t = _router(x2d.reshape(batch, seq, d), ffn_norm[i][None, :], router[i].T)
        idx_flat, gates = _select(aff_t, cap)
        xe = _gather(idx_flat, x_tiles.reshape(batch, seq * SUBLANES, LANES), N_EXPERTS, cap)
        ye = _expert_ffn(xe, gates, ffn_norm[i][None, :], exp_w_gate, exp_w_up, exp_w_down, i)
        delta = _scatter_add(idx_flat, ye, seq).reshape(m * SUBLANES, LANES)

        x2d = _ple(x2d, delta, p.reshape(depth * m, -1), i, ple_norm[i][None, :], ple_proj[i].astype(BF16),
                   ple_gate[i].astype(BF16), final_norm[None, :], final=(i == depth - 1))

    return x2d.reshape(batch, seq, d)
```

```python
import functools
import math

import jax
import jax.numpy as jnp
from jax import lax
from jax.experimental import pallas as pl
from jax.experimental.pallas import tpu as pltpu

F32 = jnp.float32
BF16 = jnp.bfloat16
I32 = jnp.int32

RW_HEADS = 8
HEAD_DIM = 64
RW_WIDTH = RW_HEADS * HEAD_DIM
RW_COLS = 3 * RW_WIDTH + 2 * 64 + 2 * 64 + 128
MLA_HEADS = 8
QK_NOPE = 64
QK_ROPE = 32
V_HEAD = 64
Q_LORA = 256
KV_LORA = 128
MLA_WIDTH = MLA_HEADS * V_HEAD
MLA_IN = Q_LORA + KV_LORA + 2 * 128
ROPE_THETA = 10000.0
N_EXPERTS = 16
EC_FACTOR = 2
NORM_EPS = 1e-6
GN_EPS = 64e-5

LANES = 128
SUBLANES = 8
CHUNK = 64
CHUNKS_PER_ITER = 4
PAIR = 2 * HEAD_DIM
N_PAIRS = RW_WIDTH // PAIR
VMEM_LIMIT = 56 * 1024 * 1024
FLASH_TQ = 2048
FLASH_TK = 2048
FLASH_SUB = 256
ONES_ROWS = 16
FLASH_SAFE_LOG2 = 40.0
NORM_MARGIN = 1.05


def _cparams(sem):
    return pltpu.CompilerParams(dimension_semantics=sem, vmem_limit_bytes=VMEM_LIMIT)


def _rms(x, g):
    return x * lax.rsqrt(jnp.mean(x * x, axis=-1, keepdims=True) + NORM_EPS) * g


def _sigmoid(x):
    return 1.0 / (1.0 + jnp.exp(-x))


def _dot(a, b):
    return jnp.dot(a.astype(BF16), b.astype(BF16), preferred_element_type=F32)


def _split_bf16(x, parts):
    out = []
    rest = x
    for _ in range(parts):
        hi = rest.astype(BF16)
        out.append(hi)
        rest = rest - hi.astype(F32)
    return out


def _dot_lsplit(a, b_exact, parts):
    acc = None
    for term in _split_bf16(a, parts):
        d = jnp.dot(term, b_exact, preferred_element_type=F32)
        acc = d if acc is None else acc + d
    return acc


def _dot_rsplit(a_exact, b, parts):
    acc = None
    for term in _split_bf16(b, parts):
        d = jnp.dot(a_exact, term, preferred_element_type=F32)
        acc = d if acc is None else acc + d
    return acc


def _dot_tn(a, b):
    return lax.dot_general(a.astype(BF16), b.astype(BF16), (((0,), (0,)), ((), ())),
                           preferred_element_type=F32)


def _dot_nt(a, b):
    return lax.dot_general(a.astype(BF16), b.astype(BF16), (((1,), (1,)), ((), ())),
                           preferred_element_type=F32)


def _rope_kernel(pos_ref, inv_ref, c_ref, s_ref):
    ang = pos_ref[...] * inv_ref[...]
    c_ref[...] = jnp.cos(ang)
    s_ref[...] = jnp.sin(ang)


def _rope_tables(positions):
    b, t = positions.shape
    m = b * t
    inv = ROPE_THETA ** (-jnp.arange(0, QK_ROPE, 2, dtype=F32) / QK_ROPE)
    inv_row = jnp.concatenate([jnp.zeros((QK_NOPE,), F32), inv, inv,
                               jnp.zeros((LANES - QK_NOPE - QK_ROPE,), F32)])[None, :]
    posf = jnp.broadcast_to(positions.astype(F32).reshape(m, 1), (m, LANES))
    tm = min(1024, m)
    return pl.pallas_call(
        _rope_kernel,
        grid=(m // tm,),
        in_specs=[pl.BlockSpec((tm, LANES), lambda i: (i, 0)),
                  pl.BlockSpec((1, LANES), lambda i: (0, 0))],
        out_specs=[pl.BlockSpec((tm, LANES), lambda i: (i, 0))] * 2,
        out_shape=[jax.ShapeDtypeStruct((m, LANES), F32)] * 2,
        compiler_params=_cparams(("parallel",)),
        name="rope_tables",
    )(posf, inv_row)


def _in_kernel(x_ref, g_ref, w_ref, o_ref):
    h = _rms(x_ref[...], g_ref[...])
    o_ref[...] = jnp.dot(h.astype(BF16), w_ref[...], preferred_element_type=F32)


def _in_proj(x2d, g, w_ext):
    m, d = x2d.shape
    n = w_ext.shape[1]
    tm = min(512, m)
    return pl.pallas_call(
        _in_kernel,
        grid=(m // tm,),
        in_specs=[pl.BlockSpec((tm, d), lambda i: (i, 0)),
                  pl.BlockSpec((1, d), lambda i: (0, 0)),
                  pl.BlockSpec((d, n), lambda i: (0, 0))],
        out_specs=pl.BlockSpec((tm, n), lambda i: (i, 0)),
        out_shape=jax.ShapeDtypeStruct((m, n), F32),
        compiler_params=_cparams(("parallel",)),
        name="in_proj",
    )(x2d, g, w_ext)


def _pair_masks():
    i = lax.broadcasted_iota(I32, (PAIR, PAIR), 0)
    j = lax.broadcasted_iota(I32, (PAIR, PAIR), 1)
    same = (i // CHUNK) == (j // CHUNK)
    li = i % CHUNK
    lj = j % CHUNK
    return same, li, lj, i == j


def _chunk_pair(a_t, b_t, k_t, r_t, v, b_h, k_h, g_last, consts):
    n = len(a_t)
    idx = range(n)
    strict = [c[0] for c in consts]
    incl = [c[1] for c in consts]
    levels = [c[2] for c in consts]
    eye, m0, m1 = consts[0][3:6]
    reverse = [c[6] for c in consts]

    def stack(x):
        return jnp.concatenate([jnp.where(m0, x, jnp.zeros_like(x)),
                                jnp.where(m1, x, jnp.zeros_like(x))], axis=0)

    a2, b2, k2, v2, bh2, kh2, r2 = ([stack(x) for x in xs] for xs in (a_t, b_t, k_t, v, b_h, k_h, r_t))
    prod = [_dot_nt(jnp.concatenate([a2[i], r2[i].astype(BF16)], axis=0),
                    jnp.concatenate([b2[i], k2[i]], axis=0)) for i in idx]
    n_mat = [jnp.where(strict[i], prod[i][:PAIR, :PAIR], 0.0) for i in idx]
    m_ak = [jnp.where(strict[i], prod[i][:PAIR, PAIR:], 0.0).astype(BF16) for i in idx]
    m_rb = [jnp.where(incl[i], prod[i][PAIR:, :PAIR], 0.0).astype(BF16) for i in idx]
    m_rk = [jnp.where(incl[i], prod[i][PAIR:, PAIR:], 0.0).astype(BF16) for i in idx]
    mv = [_dot(jnp.concatenate([m_ak[i], m_rk[i]], axis=0), v2[i]) for i in idx]

    x = [jnp.where(eye, 1.0, 0.0) - jnp.where(levels[i][0], n_mat[i], 0.0) for i in idx]
    n_bf = [n_mat[i].astype(BF16) for i in idx]
    zero = jnp.zeros((PAIR, PAIR), BF16)
    for lv in range(1, len(levels[0])):
        s = 2 ** lv
        if s < SUBLANES:
            cx = [_dot(jnp.where(levels[i][lv], n_bf[i], zero), x[i]) for i in idx]
            x = [x[i] - _dot(x[i], cx[i]) for i in idx]
            continue
        blocks = [(r, r + s) for r in range(0, PAIR, s)]
        upd = [[((r % CHUNK) // s) % 2 == (0 if reverse[i] else 1) for r, _ in blocks] for i in idx]

        def take(mat, i):
            return jnp.concatenate([mat[r0:r1] for (r0, r1), u in zip(blocks, upd[i]) if u], axis=0)

        c_h = [take(jnp.where(levels[i][lv], n_mat[i], 0.0), i) for i in idx]
        cx_h = [_dot(c_h[i], x[i]) for i in idx]
        zrows = jnp.zeros((s, PAIR), F32)
        cx = []
        for i in idx:
            it = iter(range(CHUNK // s))
            cx.append(jnp.concatenate(
                [cx_h[i][k * s:(k + 1) * s] if u else zrows
                 for u in upd[i] for k in ([next(it)] if u else [0])], axis=0))
        du = [_dot(take(x[i], i), cx[i]) for i in idx]
        x_new = []
        for i in idx:
            it = iter(range(CHUNK // s))
            x_new.append(jnp.concatenate(
                [x[i][r0:r1] - du[i][k * s:(k + 1) * s] if u else x[i][r0:r1]
                 for (r0, r1), u in zip(blocks, upd[i]) for k in ([next(it)] if u else [0])], axis=0))
        x = x_new

    tw = [_dot(x[i], jnp.concatenate([a2[i], mv[i][:PAIR].astype(BF16)], axis=1))
          for i in idx]
    tw_bf = [t.astype(BF16) for t in tw]
    qo = [jnp.concatenate([r2[i], mv[i][PAIR:]], axis=1) - _dot(m_rb[i], tw_bf[i])
          for i in idx]
    bt = [_dot_tn(bh2[i], tw_bf[i]) for i in idx]
    kv = [_dot_tn(kh2[i], v2[i]) for i in idx]
    out = []
    for i in idx:
        g_mat = jnp.where(eye, g_last[i], 0.0) - bt[i][:, :PAIR]
        h_mat = kv[i] - bt[i][:, PAIR:]
        q = qo[i][:CHUNK] + qo[i][CHUNK:]
        out.append((q[:, :PAIR], q[:, PAIR:], g_mat, h_mat))
    return out


def _rwkv_a_kernel(z_ref, zp_ref, zn_ref, mu_ref, w0_ref, wup_ref, a0_ref, aup_ref,
                   gup_ref, kk_ref, ka_ref, rk_ref, hsum_ref, trif_ref, trib_ref,
                   q_out, ol_out, g_out, h_out, bonus_out, gate_out,
                   at_s, bt_s, kt_s, rt_s, bh_s, kh_s, v_s, gl_s, *, tm, seq):
    i = pl.program_id(0)
    z = z_ref[...]
    has_prev = (i * tm) % seq != 0
    has_next = ((i + 1) * tm) % seq != 0
    prev_row = jnp.where(has_prev, zp_ref[SUBLANES - 1:SUBLANES, :], 0.0)
    next_row = jnp.where(has_next, zn_ref[0:1, :], 0.0)
    sub = lax.broadcasted_iota(I32, (SUBLANES, 1), 0)
    z_dn = pltpu.roll(z, 1, axis=0)
    z_dn = jnp.concatenate([jnp.where(sub == 0, prev_row, z_dn[:SUBLANES]), z_dn[SUBLANES:]], axis=0)
    z_up = pltpu.roll(z, tm - 1, axis=0)
    z_up = jnp.concatenate([z_up[:tm - SUBLANES],
                            jnp.where(sub == SUBLANES - 1, next_row, z_up[tm - SUBLANES:])], axis=0)
    zs = z + mu_ref[...] * (0.5 * (z_dn + z_up) - z)

    w = RW_WIDTH
    r = zs[:, :w]
    k = zs[:, w:2 * w]
    v = zs[:, 2 * w:3 * w]
    wd = zs[:, 3 * w:3 * w + 128]
    ad = zs[:, 3 * w + 128:3 * w + 256]
    gd = zs[:, 3 * w + 256:3 * w + 384]

    hsum = hsum_ref[...]
    w_logit = w0_ref[...] + _dot(jnp.tanh(wd), wup_ref[...])
    lw = -_sigmoid(w_logit) * jnp.exp(jnp.float32(-0.5))
    a = _sigmoid(a0_ref[...] + _dot(ad, aup_ref[...]))
    gate_out[...] = _dot(_sigmoid(gd), gup_ref[...]).astype(BF16)
    kkr = k * kk_ref[...]
    kkn = kkr * jnp.minimum(lax.rsqrt(_dot(kkr * kkr, hsum)), 1e12)
    kd = [k * (1.0 + (a[:, d * w:(d + 1) * w] - 1.0) * ka_ref[...]) for d in range(2)]
    bonus_out[...] = (_dot(r * (0.5 * (kd[0] + kd[1])) * rk_ref[...], hsum) * v).astype(BF16)
    v_s[...] = v.astype(BF16)

    for d, tri_ref in enumerate((trif_ref, trib_ref)):
        cols = slice(d * w, (d + 1) * w)
        lw_d = lw[:, cols]
        cum = _dot_rsplit(tri_ref[...], lw_d, 2)
        ends = [c * CHUNK if d == 1 else (c + 1) * CHUNK - 1 for c in range(tm // CHUNK)]
        tot = jnp.concatenate([jnp.broadcast_to(cum[e:e + 1, :], (CHUNK, w)) for e in ends], axis=0)
        g_inv = jnp.exp(-cum)
        g_end = jnp.exp(tot - cum)
        b = kkn * a[:, cols]
        at_s[d] = (kkn * jnp.exp(cum - lw_d)).astype(BF16)
        bt_s[d] = (b * g_inv).astype(BF16)
        kt_s[d] = (kd[d] * g_inv).astype(BF16)
        rt_s[d] = r * jnp.exp(cum)
        bh_s[d] = (b * g_end).astype(BF16)
        kh_s[d] = (kd[d] * g_end).astype(BF16)
        gl_s[d] = jnp.exp(tot)

    same, li, lj, eye = _pair_masks()
    lane = lax.broadcasted_iota(I32, (1, PAIR), 1)
    m0 = lane < HEAD_DIM
    m1 = lane >= HEAD_DIM
    consts = []
    for reverse in (False, True):
        before = (lj > li) if reverse else (lj < li)
        strict = same & before
        incl = same & (before | (li == lj))
        levels = []
        s = 1
        while s < CHUNK:
            blk = same & ((li // (2 * s)) == (lj // (2 * s)))
            hi_row = (li // s) % 2 == 1
            hi_col = (lj // s) % 2 == 1
            levels.append(blk & ((~hi_row & hi_col) if reverse else (hi_row & ~hi_col)))
            s *= 2
        consts.append((strict, incl, levels, eye, m0, m1, reverse))

    def chunk_body(it, carry):
        inst = []
        for j in range(CHUNKS_PER_ITER):
            c = it * CHUNKS_PER_ITER + j
            r0 = pl.multiple_of(c * CHUNK, CHUNK)
            inst += [(c, r0, pl.ds(r0, CHUNK), d, slice(p * PAIR, (p + 1) * PAIR))
                     for d in range(2) for p in range(N_PAIRS)]
        outs = _chunk_pair(
            [at_s[d, rows, ln] for c, r0, rows, d, ln in inst],
            [bt_s[d, rows, ln] for c, r0, rows, d, ln in inst],
            [kt_s[d, rows, ln] for c, r0, rows, d, ln in inst],
            [rt_s[d, rows, ln] for c, r0, rows, d, ln in inst],
            [v_s[rows, ln] for c, r0, rows, d, ln in inst],
            [bh_s[d, rows, ln] for c, r0, rows, d, ln in inst],
            [kh_s[d, rows, ln] for c, r0, rows, d, ln in inst],
            [gl_s[d, pl.ds(r0, 1), ln] for c, r0, rows, d, ln in inst],
            [consts[d] for c, r0, rows, d, ln in inst])
        for (c, r0, rows, d, ln), (qe, ol, g_mat, h_mat) in zip(inst, outs):
            q_out[d, rows, ln] = qe.astype(BF16)
            ol_out[d, rows, ln] = ol.astype(BF16)
            g_out[d, c, :, ln] = (g_mat[:HEAD_DIM] + g_mat[HEAD_DIM:]).astype(BF16)
            h_out[d, c, :, ln] = (h_mat[:HEAD_DIM] + h_mat[HEAD_DIM:]).astype(BF16)
        return carry

    lax.fori_loop(0, tm // (CHUNK * CHUNKS_PER_ITER), chunk_body, 0)


def _rwkv_a(z2d, seq, mu, w0, wup, a0, aup, gup, k_k, k_a, r_k, hsum):
    m = z2d.shape[0]
    tm = min(256, seq)
    nc = tm // CHUNK
    w = RW_WIDTH
    full = lambda shape: pl.BlockSpec(shape, lambda i: (0,) * len(shape))
    last8 = m // 8 - 1
    ti = jnp.arange(tm)[:, None]
    tj = jnp.arange(tm)[None, :]
    same_chunk = (ti // CHUNK) == (tj // CHUNK)
    tri_f = (same_chunk & (tj <= ti)).astype(BF16)
    tri_b = (same_chunk & (tj >= ti)).astype(BF16)
    kern = functools.partial(_rwkv_a_kernel, tm=tm, seq=seq)
    return pl.pallas_call(
        kern,
        grid=(m // tm,),
        in_specs=[
            pl.BlockSpec((tm, RW_COLS), lambda i: (i, 0)),
            pl.BlockSpec((8, RW_COLS), lambda i: (jnp.maximum(i * (tm // 8) - 1, 0), 0)),
            pl.BlockSpec((8, RW_COLS), lambda i: (jnp.minimum((i + 1) * (tm // 8), last8), 0)),
            full((1, RW_COLS)), full((1, 2 * w)), full((128, 2 * w)),
            full((1, 2 * w)), full((128, 2 * w)), full((128, w)),
            full((1, w)), full((1, w)), full((1, w)), full((w, w)), full((tm, tm)), full((tm, tm)),
        ],
        out_specs=[
            pl.BlockSpec((2, tm, w), lambda i: (0, i, 0)),
            pl.BlockSpec((2, tm, w), lambda i: (0, i, 0)),
            pl.BlockSpec((2, nc, HEAD_DIM, w), lambda i: (0, i, 0, 0)),
            pl.BlockSpec((2, nc, HEAD_DIM, w), lambda i: (0, i, 0, 0)),
            pl.BlockSpec((tm, w), lambda i: (i, 0)),
            pl.BlockSpec((tm, w), lambda i: (i, 0)),
        ],
        out_shape=[
            jax.ShapeDtypeStruct((2, m, w), BF16),
            jax.ShapeDtypeStruct((2, m, w), BF16),
            jax.ShapeDtypeStruct((2, m // CHUNK, HEAD_DIM, w), BF16),
            jax.ShapeDtypeStruct((2, m // CHUNK, HEAD_DIM, w), BF16),
            jax.ShapeDtypeStruct((m, w), BF16),
            jax.ShapeDtypeStruct((m, w), BF16),
        ],
        scratch_shapes=[pltpu.VMEM((2, tm, w), BF16)] * 3 + [pltpu.VMEM((2, tm, w), F32)]
        + [pltpu.VMEM((2, tm, w), BF16)] * 2 + [pltpu.VMEM((tm, w), BF16), pltpu.VMEM((2, tm, w), F32)],
        compiler_params=_cparams(("parallel",)),
        name="rwkv_chunk_local",
    )(z2d, z2d, z2d, mu, w0, wup.astype(BF16), a0, aup.astype(BF16), gup.astype(BF16), k_k, k_a, r_k,
      hsum, tri_f, tri_b)


def _rwkv_b_kernel(qf_ref, olf_ref, gf_ref, hf_ref, qb_ref, olb_ref, gb_ref, hb_ref,
                   of_ref, ob_ref, s_ref, *, cb):
    @pl.when(pl.program_id(1) == 0)
    def _():
        s_ref[...] = jnp.zeros_like(s_ref)

    lane = lax.broadcasted_iota(I32, (1, PAIR), 1)

    def block_diag(packed):
        zero = jnp.zeros_like(packed)
        return jnp.concatenate([jnp.where(lane < HEAD_DIM, packed, zero),
                                jnp.where(lane >= HEAD_DIM, packed, zero)], axis=0)

    for step in range(cb):
        inst = []
        for d, refs in enumerate(((qf_ref, olf_ref, gf_ref, hf_ref, of_ref),
                                  (qb_ref, olb_ref, gb_ref, hb_ref, ob_ref))):
            c = cb - 1 - step if d == 1 else step
            for p in range(N_PAIRS):
                inst.append((d, c, slice(c * CHUNK, (c + 1) * CHUNK),
                             slice(p * PAIR, (p + 1) * PAIR)) + refs)
        s_bf = [s_ref[d, :, ln].astype(BF16) for d, c, rows, ln, *_ in inst]
        s_new = [jnp.dot(block_diag(g_ref[c, :, ln]), sb, preferred_element_type=F32)
                 + block_diag(h_ref[c, :, ln])
                 for (d, c, rows, ln, q_ref, ol_ref, g_ref, h_ref, o_ref), sb in zip(inst, s_bf)]
        o_val = [jnp.dot(q_ref[rows, ln], sb, preferred_element_type=F32) + ol_ref[rows, ln]
                 for (d, c, rows, ln, q_ref, ol_ref, g_ref, h_ref, o_ref), sb in zip(inst, s_bf)]
        for (d, c, rows, ln, q_ref, ol_ref, g_ref, h_ref, o_ref), sn, ov in zip(inst, s_new, o_val):
            s_ref[d, :, ln] = sn
            o_ref[rows, ln] = ov.astype(BF16)


def _rwkv_b(qeff, oloc, g_all, h_all, batch, seq):
    m = batch * seq
    w = RW_WIDTH
    cb = min(8, seq // CHUNK)
    tm = cb * CHUNK
    nb = seq // tm

    def fwd(b, j):
        return b * nb + j

    def bwd(b, j):
        return b * nb + nb - 1 - j

    def specs(d, blk):
        return [
            pl.BlockSpec((None, tm, w), lambda b, j: (d, blk(b, j), 0)),
            pl.BlockSpec((None, tm, w), lambda b, j: (d, blk(b, j), 0)),
            pl.BlockSpec((None, cb, HEAD_DIM, w), lambda b, j: (d, blk(b, j), 0, 0)),
            pl.BlockSpec((None, cb, HEAD_DIM, w), lambda b, j: (d, blk(b, j), 0, 0)),
        ]

    return pl.pallas_call(
        functools.partial(_rwkv_b_kernel, cb=cb),
        grid=(batch, nb),
        in_specs=specs(0, fwd) + specs(1, bwd),
        out_specs=[pl.BlockSpec((tm, w), lambda b, j: (fwd(b, j), 0)),
                   pl.BlockSpec((tm, w), lambda b, j: (bwd(b, j), 0))],
        out_shape=[jax.ShapeDtypeStruct((m, w), BF16)] * 2,
        scratch_shapes=[pltpu.VMEM((2, PAIR, w), F32)],
        compiler_params=_cparams(("parallel", "arbitrary")),
        name="rwkv_recurrence",
    )(qeff, oloc, g_all, h_all, qeff, oloc, g_all, h_all)


def _mla_prep_kernel(z_ref, c_ref, s_ref, qn_ref, kvn_ref, qa_ref, qb_ref, kk_ref, kvv_ref, hsel_ref,
                     q_out, k_out, v_out, qmax_out, kmax_out, *, scale):
    z = z_ref[...]
    cos = c_ref[...]
    sin = s_ref[...]
    qd = _rms(z[:, :Q_LORA], qn_ref[...]).astype(BF16)
    kvd = _rms(z[:, Q_LORA:Q_LORA + KV_LORA], kvn_ref[...]).astype(BF16)
    o = Q_LORA + KV_LORA
    kr = z[:, o:o + LANES] * cos + z[:, o + LANES:o + 2 * LANES] * sin
    qa = jnp.dot(qd, qa_ref[...], preferred_element_type=F32)
    qb = jnp.dot(qd, qb_ref[...], preferred_element_type=F32)
    kn = jnp.dot(kvd, kk_ref[...], preferred_element_type=F32)
    v_out[...] = lax.dot_general(kvv_ref[...], kvd, (((1,), (1,)), ((), ())),
                                 preferred_element_type=F32).astype(BF16)
    qs, ks = [], []
    for h in range(MLA_HEADS):
        lanes = slice(h * LANES, (h + 1) * LANES)
        qs.append(((qa[:, lanes] * cos + qb[:, lanes] * sin) * scale).astype(BF16))
        ks.append((kn[:, lanes] + kr).astype(BF16))
        q_out[h] = qs[h]
        k_out[h] = ks[h]
    for vals, out in ((qs, qmax_out), (ks, kmax_out)):
        full = jnp.concatenate([v.astype(F32) for v in vals], axis=1)
        n2 = jnp.dot((full * full).astype(BF16), hsel_ref[...], preferred_element_type=F32)
        out[...] = jnp.broadcast_to(jnp.max(n2, axis=0, keepdims=True), (SUBLANES, LANES))


def _mla_prep(z2d, cos, sin, batch, seq, q_norm, kv_norm, q_a, q_b, kv_k, kv_v):
    tm = min(512, seq)
    nt = seq // tm
    hw = MLA_HEADS * LANES
    scale = float((QK_NOPE + QK_ROPE) ** -0.5 * math.log2(math.e))
    full = lambda shape: pl.BlockSpec(shape, lambda b, i: (0,) * len(shape))
    col_blk = RW_COLS // MLA_IN
    assert col_blk * MLA_IN == RW_COLS
    head_sel = (jnp.arange(hw)[:, None] // LANES == jnp.arange(LANES)[None, :]).astype(BF16)
    q, k, v_t, qmax, kmax = pl.pallas_call(
        functools.partial(_mla_prep_kernel, scale=scale),
        grid=(batch, nt),
        in_specs=[
            pl.BlockSpec((tm, MLA_IN), lambda b, i: (b * nt + i, col_blk)),
            pl.BlockSpec((tm, LANES), lambda b, i: (b * nt + i, 0)),
            pl.BlockSpec((tm, LANES), lambda b, i: (b * nt + i, 0)),
            full((1, Q_LORA)), full((1, KV_LORA)), full((Q_LORA, hw)), full((Q_LORA, hw)),
            full((KV_LORA, hw)), full((MLA_WIDTH, KV_LORA)), full((hw, LANES)),
        ],
        out_specs=[
            pl.BlockSpec((None, MLA_HEADS, tm, LANES), lambda b, i: (b, 0, i, 0)),
            pl.BlockSpec((None, MLA_HEADS, tm, LANES), lambda b, i: (b, 0, i, 0)),
            pl.BlockSpec((None, MLA_WIDTH, tm), lambda b, i: (b, 0, i)),
            pl.BlockSpec((None, None, SUBLANES, LANES), lambda b, i: (b, i, 0, 0)),
            pl.BlockSpec((None, None, SUBLANES, LANES), lambda b, i: (b, i, 0, 0)),
        ],
        out_shape=[
            jax.ShapeDtypeStruct((batch, MLA_HEADS, seq, LANES), BF16),
            jax.ShapeDtypeStruct((batch, MLA_HEADS, seq, LANES), BF16),
            jax.ShapeDtypeStruct((batch, MLA_WIDTH, seq), BF16),
            jax.ShapeDtypeStruct((batch, nt, SUBLANES, LANES), F32),
            jax.ShapeDtypeStruct((batch, nt, SUBLANES, LANES), F32),
        ],
        compiler_params=_cparams(("parallel", "parallel")),
        name="mla_prep",
    )(z2d, cos, sin, q_norm, kv_norm, q_a, q_b, kv_k, kv_v, head_sel)
    bound = jnp.sqrt(jnp.max(qmax[:, :, 0, :MLA_HEADS], axis=1)
                     * jnp.max(kmax[:, :, 0, :MLA_HEADS], axis=1)) * NORM_MARGIN
    small = (bound <= FLASH_SAFE_LOG2).reshape(batch, MLA_HEADS // 2, 2).all(axis=-1)
    return q, k, v_t, small.astype(I32).reshape(-1)


def _flash_kernel(small_ref, q_ref, k_ref, vt_ref, o_ref, m_ref, l_ref, acc_ref):
    j = pl.program_id(3)
    small = small_ref[pl.program_id(0) * pl.num_programs(1) + pl.program_id(1)] != 0

    @pl.when(j == 0)
    def _():
        m_ref[...] = jnp.full_like(m_ref, -jnp.inf)
        l_ref[...] = jnp.zeros_like(l_ref)
        acc_ref[...] = jnp.zeros_like(acc_ref)

    tk = k_ref.shape[1]
    sub = min(FLASH_SUB, tk)
    inst = [(h, slice(b * sub, (b + 1) * sub)) for b in range(tk // sub) for h in range(2)]
    rows = [slice(h * V_HEAD, (h + 1) * V_HEAD) for h in range(2)]
    ones = jnp.ones((ONES_ROWS, sub), BF16)

    def scores():
        return [lax.dot_general(k_ref[h, kb, :], q_ref[h], (((1,), (1,)), ((), ())),
                                preferred_element_type=F32) for h, kb in inst]

    def weighted_values(p):
        pv = [jnp.dot(jnp.concatenate([vt_ref[rows[h], kb], ones], axis=0), x,
                      preferred_element_type=F32) for (h, kb), x in zip(inst, p)]
        return [x[:V_HEAD, :] for x in pv], [x[V_HEAD:V_HEAD + 1, :] for x in pv]

    @pl.when(small)
    def _():
        pv, l_loc = weighted_values([jnp.exp2(x).astype(BF16) for x in scores()])
        for h in range(2):
            mine = [i for i, (hh, _) in enumerate(inst) if hh == h]
            l_ref[h:h + 1, :] = l_ref[h:h + 1, :] + sum(l_loc[i] for i in mine)
            acc_ref[rows[h], :] = acc_ref[rows[h], :] + sum(pv[i] for i in mine)

    @pl.when(jnp.logical_not(small))
    def _():
        s = scores()
        m_loc = [jnp.max(x, axis=0, keepdims=True) for x in s]
        pv, l_loc = weighted_values([jnp.exp2(x - m).astype(BF16) for x, m in zip(s, m_loc)])
        for h in range(2):
            mine = [i for i, (hh, _) in enumerate(inst) if hh == h]
            m_prev = m_ref[h:h + 1, :]
            m_new = m_prev
            for i in mine:
                m_new = jnp.maximum(m_new, m_loc[i])
            alpha = jnp.exp2(m_prev - m_new)
            l_new = alpha * l_ref[h:h + 1, :]
            acc = alpha * acc_ref[rows[h], :]
            for i in mine:
                w = jnp.exp2(m_loc[i] - m_new)
                l_new = l_new + w * l_loc[i]
                acc = acc + w * pv[i]
            m_ref[h:h + 1, :] = m_new
            l_ref[h:h + 1, :] = l_new
            acc_ref[rows[h], :] = acc

    @pl.when(j == pl.num_programs(3) - 1)
    def _():
        inv = 1.0 / l_ref[...]
        o_t = jnp.concatenate([acc_ref[:V_HEAD, :] * inv[0:1, :], acc_ref[V_HEAD:, :] * inv[1:2, :]],
                              axis=0)
        o_ref[...] = o_t.T


def _flash(small, q, k, v_t):
    batch, heads, seq, _ = q.shape
    tq = min(FLASH_TQ, seq)
    tk = min(FLASH_TK, seq)
    return pl.pallas_call(
        _flash_kernel,
        grid_spec=pltpu.PrefetchScalarGridSpec(
            num_scalar_prefetch=1,
            grid=(batch, heads // 2, seq // tq, seq // tk),
            in_specs=[
                pl.BlockSpec((None, 2, tq, LANES), lambda b, p, i, j, sm: (b, p, i, 0)),
                pl.BlockSpec((None, 2, tk, LANES), lambda b, p, i, j, sm: (b, p, j, 0)),
                pl.BlockSpec((None, 2 * V_HEAD, tk), lambda b, p, i, j, sm: (b, p, j)),
            ],
            out_specs=pl.BlockSpec((None, tq, LANES), lambda b, p, i, j, sm: (b, i, p)),
            scratch_shapes=[pltpu.VMEM((2, tq), F32), pltpu.VMEM((2, tq), F32),
                            pltpu.VMEM((2 * V_HEAD, tq), F32)],
        ),
        out_shape=jax.ShapeDtypeStruct((batch, seq, MLA_WIDTH), F32),
        compiler_params=_cparams(("parallel", "parallel", "parallel", "arbitrary")),
        name="mla_flash",
    )(small, q, k, v_t)


def _out_kernel(of_ref, ob_ref, bonus_ref, gate_ref, ym_ref, x_ref, hsum_ref, lw_ref, lb_ref,
                on_ref, w_ref, o_ref, ot_ref):
    o = of_ref[...].astype(F32) + ob_ref[...].astype(F32)
    hsum = hsum_ref[...]
    inv_n = 1.0 / HEAD_DIM
    mean = _dot_lsplit(o, hsum, 2) * inv_n
    d = o - mean
    var = _dot_lsplit(d * d, hsum, 2) * inv_n
    y_rw = (d * lax.rsqrt(var + GN_EPS) * lw_ref[...] + lb_ref[...] + bonus_ref[...]) * gate_ref[...]
    y_mla = _rms(ym_ref[...], on_ref[...])
    w = RW_WIDTH
    out = x_ref[...] + _dot(y_rw, w_ref[:w, :]) + _dot(y_mla, w_ref[w:, :])
    o_ref[...] = out
    _rows_to_tiles(ot_ref, out)


def _out_proj(o_f, o_b, bonus, gate, y_mla, x2d, hsum, lnx_w, lnx_b, o_norm, w_out):
    m, d = x2d.shape
    w = RW_WIDTH
    tm = min(512, m)
    row = lambda n: pl.BlockSpec((tm, n), lambda i: (i, 0))
    full = lambda shape: pl.BlockSpec(shape, lambda i: (0,) * len(shape))
    return pl.pallas_call(
        _out_kernel,
        grid=(m // tm,),
        in_specs=[row(w), row(w), row(w), row(w), row(MLA_WIDTH), row(d), full((w, w)),
                  full((1, w)), full((1, w)), full((1, MLA_WIDTH)), full((w + MLA_WIDTH, d))],
        out_specs=[row(d), pl.BlockSpec((tm * SUBLANES, LANES), lambda i: (i, 0))],
        out_shape=[jax.ShapeDtypeStruct((m, d), F32),
                   jax.ShapeDtypeStruct((m * SUBLANES, LANES), F32)],
        compiler_params=_cparams(("parallel",)),
        name="out_proj",
    )(o_f, o_b, bonus, gate, y_mla, x2d, hsum, lnx_w, lnx_b, o_norm, w_out)


def _router_kernel(x_ref, g_ref, rth_ref, rtl_ref, a_ref):
    x_hi, x_lo = _split_bf16(_rms(x_ref[...], g_ref[...]), 2)
    nt = (((1,), (1,)), ((), ()))
    logits = (lax.dot_general(rth_ref[...], x_hi, nt, preferred_element_type=F32)
              + lax.dot_general(rth_ref[...], x_lo, nt, preferred_element_type=F32)
              + lax.dot_general(rtl_ref[...], x_hi, nt, preferred_element_type=F32))
    mx = jnp.max(logits, axis=0, keepdims=True)
    e = jnp.exp(logits - mx)
    a_ref[...] = e / jnp.sum(e, axis=0, keepdims=True)


def _router(x3d, g, router_t):
    batch, seq, d = x3d.shape
    e = router_t.shape[0]
    tm = min(512, seq)
    return pl.pallas_call(
        _router_kernel,
        grid=(batch, seq // tm),
        in_specs=[pl.BlockSpec((None, tm, d), lambda b, i: (b, i, 0)),
                  pl.BlockSpec((1, d), lambda b, i: (0, 0)),
                  pl.BlockSpec((e, d), lambda b, i: (0, 0)),
                  pl.BlockSpec((e, d), lambda b, i: (0, 0))],
        out_specs=pl.BlockSpec((None, e, tm), lambda b, i: (b, 0, i)),
        out_shape=jax.ShapeDtypeStruct((batch, e, seq), F32),
        compiler_params=_cparams(("parallel", "parallel")),
        name="moe_router",
    )(x3d, g, *_hi_lo(router_t))


def _threshold_kernel(a_ref, thr_ref, *, cap):
    bits = pltpu.bitcast(a_ref[...], I32)
    n_e = bits.shape[0]

    def search(i, cur):
        cand = cur | jnp.left_shift(jnp.int32(1), 30 - i)
        cnt = jnp.sum(jnp.where(bits >= cand, 1, 0), axis=1, keepdims=True)
        return jnp.where(cnt >= cap, cand, cur)

    thr_ref[...] = lax.fori_loop(0, 31, search, jnp.zeros((n_e, 1), I32))


def _compact_kernel(a_ref, thr_ref, idx_ref, gate_ref, *, cap):
    a = a_ref[...]
    nb = a.shape[0]
    bits = pltpu.bitcast(a, I32)
    thr = thr_ref[...]
    ri = lax.broadcasted_iota(I32, (LANES, LANES), 0)
    ci = lax.broadcasted_iota(I32, (LANES, LANES), 1)
    upper = jnp.where(ri <= ci, 1.0, 0.0).astype(BF16)
    bi = lax.broadcasted_iota(I32, (nb, nb), 0)
    bj = lax.broadcasted_iota(I32, (nb, nb), 1)
    before = jnp.where(bj < bi, 1.0, 0.0).astype(BF16)

    def total(x):
        return jnp.sum(jnp.sum(x, axis=1, keepdims=True), axis=0, keepdims=True)

    def running(mask):
        within = jnp.dot(mask, upper, preferred_element_type=F32)
        tot = jnp.broadcast_to(within[:, LANES - 1:LANES], (nb, LANES))
        return within, jnp.dot(before, tot.astype(BF16), preferred_element_type=F32)

    gt = bits > thr
    eq = bits == thr
    need = cap - total(jnp.where(gt, 1.0, 0.0))
    w_eq, b_eq = running(jnp.where(eq, 1.0, 0.0).astype(BF16))
    sel = gt | (eq & (w_eq + b_eq <= need))
    sel_b = jnp.where(sel, 1.0, 0.0).astype(BF16)
    within, base = running(sel_b)

    tot_row = lax.dot_general(jnp.ones((SUBLANES, LANES), BF16), sel_b, (((1,), (1,)), ((), ())),
                              preferred_element_type=F32)
    base_row = jnp.dot(tot_row.astype(BF16), jnp.where(bi < bj, 1.0, 0.0).astype(BF16),
                       preferred_element_type=F32)
    c_col = lax.broadcasted_iota(I32, (cap, 1), 0).astype(F32)
    in_block = (base_row[0:1, :] <= c_col) & (c_col < base_row[0:1, :] + tot_row[0:1, :])
    onehot = jnp.where(in_block, 1.0, 0.0).astype(BF16)

    a_parts = _split_bf16(a, 3)
    base_hi = jnp.floor(base * (1.0 / 32.0))
    lane = lax.broadcasted_iota(I32, (1, LANES), 1).astype(F32)
    block_id = lax.broadcasted_iota(I32, (nb, LANES), 0).astype(F32)
    table = jnp.concatenate(
        [within.astype(BF16), sel_b] + a_parts
        + [base_hi.astype(BF16), (base - 32.0 * base_hi).astype(BF16), block_id.astype(BF16)], axis=1)
    g = jnp.dot(onehot, table, preferred_element_type=F32)
    part = lambda k: g[:, k * LANES:(k + 1) * LANES]
    a_c = part(2) + part(3) + part(4)
    slot = lax.broadcasted_iota(I32, (cap, LANES), 0).astype(F32)
    target = slot - (32.0 * part(5) + part(6)) + 1.0
    match = (part(0) == target) & (part(1) > 0.5)
    pos = jnp.dot(jnp.where(match, lane, 0.0).astype(BF16), jnp.ones((LANES, LANES), BF16),
                  preferred_element_type=F32)
    gate_ref[...] = jnp.sum(jnp.where(match, a_c, 0.0), axis=1, keepdims=True)
    idx_ref[...] = (LANES * part(7) + pos)[:, 0:1].astype(I32)


def _select(aff_t, cap):
    batch, e, seq = aff_t.shape
    nb = seq // LANES
    thr = pl.pallas_call(
        functools.partial(_threshold_kernel, cap=cap),
        grid=(batch,),
        in_specs=[pl.BlockSpec((None, e, seq), lambda b: (b, 0, 0))],
        out_specs=pl.BlockSpec((None, e, 1), lambda b: (b, 0, 0)),
        out_shape=jax.ShapeDtypeStruct((batch, e, 1), I32),
        compiler_params=_cparams(("parallel",)),
        name="moe_threshold",
    )(aff_t)
    idx, gate = pl.pallas_call(
        functools.partial(_compact_kernel, cap=cap),
        grid=(batch, e),
        in_specs=[pl.BlockSpec((None, None, nb, LANES), lambda b, j: (b, j, 0, 0)),
                  pl.BlockSpec((None, None, 1, 1), lambda b, j: (b, j, 0, 0))],
        out_specs=[pl.BlockSpec((None, None, cap, 1), lambda b, j: (b, j, 0, 0))] * 2,
        out_shape=[jax.ShapeDtypeStruct((batch, e, cap, 1), I32),
                   jax.ShapeDtypeStruct((batch, e, cap, 1), F32)],
        compiler_params=_cparams(("parallel", "parallel")),
        name="moe_compact",
    )(aff_t.reshape(batch, e, nb, LANES), thr.reshape(batch, e, 1, 1))
    return idx.reshape(-1), gate


def _rows_from_tiles(ref):
    n = ref.shape[0] // SUBLANES
    return jnp.concatenate([ref[pl.ds(s, n, stride=SUBLANES), :] for s in range(SUBLANES)], axis=-1)


def _rows_to_tiles(ref, val):
    n = val.shape[0]
    for s in range(SUBLANES):
        ref[pl.ds(s, n, stride=SUBLANES), :] = val[:, s * LANES:(s + 1) * LANES]


def _tile(r):
    return pl.ds(pl.multiple_of(r * SUBLANES, SUBLANES), SUBLANES)


def _gather_kernel(idx_ref, x_ref, o_ref, *, cap, n_e):
    base = (pl.program_id(0) * n_e + pl.program_id(1)) * cap

    def body(c, carry):
        o_ref[_tile(c), :] = x_ref[_tile(idx_ref[base + c]), :]
        return carry

    lax.fori_loop(0, cap, body, 0, unroll=8)


def _gather(idx_flat, x_tiles, n_e, cap):
    batch, rows, _ = x_tiles.shape
    return pl.pallas_call(
        functools.partial(_gather_kernel, cap=cap, n_e=n_e),
        grid_spec=pltpu.PrefetchScalarGridSpec(
            num_scalar_prefetch=1,
            grid=(batch, n_e),
            in_specs=[pl.BlockSpec((None, rows, LANES), lambda b, e, idx: (b, 0, 0),
                                   pipeline_mode=pl.Buffered(1))],
            out_specs=pl.BlockSpec((None, None, cap * SUBLANES, LANES),
                                   lambda b, e, idx: (b, e, 0, 0)),
        ),
        out_shape=jax.ShapeDtypeStruct((batch, n_e, cap * SUBLANES, LANES), F32),
        compiler_params=_cparams(("parallel", "arbitrary")),
        name="moe_gather",
    )(idx_flat, x_tiles)


def _ffn_kernel(x_ref, gate_ref, g_ref, wg_ref, wu_ref, wd_ref, o_ref, wg_s, wu_s, wd_s):
    @pl.when((pl.program_id(1) == 0) & (pl.program_id(2) == 0))
    def _():
        wg_s[...] = wg_ref[...].astype(BF16)
        wu_s[...] = wu_ref[...].astype(BF16)
        wd_s[...] = wd_ref[...].astype(BF16)

    xn = _rms(_rows_from_tiles(x_ref), g_ref[...]).astype(BF16)
    h1 = jnp.dot(xn, wg_s[...], preferred_element_type=F32)
    h2 = jnp.dot(xn, wu_s[...], preferred_element_type=F32)
    hid = (h1 * _sigmoid(h1) * h2).astype(BF16)
    _rows_to_tiles(o_ref, jnp.dot(hid, wd_s[...], preferred_element_type=F32) * gate_ref[...])


def _expert_ffn(xe, gate_col, g, w_gate, w_up, w_down, layer):
    batch, n_e, rows, _ = xe.shape
    cap = rows // SUBLANES
    d, f = w_gate.shape[2:]
    tc = min(512, cap)
    n_c = cap // tc
    steps = batch * n_c

    def weight_spec(shape, k):
        ahead = min(k + 1, steps - 1)

        def index(e, b, c):
            nxt = (b * n_c + c >= ahead) if ahead > 0 else False
            return (layer, jnp.minimum(e + jnp.where(nxt, 1, 0), n_e - 1), 0, 0)

        return pl.BlockSpec((None, None) + shape, index)

    return pl.pallas_call(
        _ffn_kernel,
        grid=(n_e, batch, n_c),
        in_specs=[
            pl.BlockSpec((None, None, tc * SUBLANES, LANES), lambda e, b, c: (b, e, c, 0)),
            pl.BlockSpec((None, None, tc, 1), lambda e, b, c: (b, e, c, 0)),
            pl.BlockSpec((1, d), lambda e, b, c: (0, 0)),
            weight_spec((d, f), 0), weight_spec((d, f), 1), weight_spec((f, d), 2),
        ],
        out_specs=pl.BlockSpec((None, None, tc * SUBLANES, LANES), lambda e, b, c: (b, e, c, 0)),
        out_shape=jax.ShapeDtypeStruct(xe.shape, F32),
        scratch_shapes=[pltpu.VMEM((d, f), BF16), pltpu.VMEM((d, f), BF16), pltpu.VMEM((f, d), BF16)],
        compiler_params=_cparams(("parallel", "arbitrary", "arbitrary")),
        name="moe_ffn",
    )(xe, gate_col, g, w_gate, w_up, w_down)


SCATTER_GROUP = 8


def _scatter_kernel(idx_ref, y_ref, o_ref, *, cap, n_e):
    e = pl.program_id(1)
    base = (pl.program_id(0) * n_e + e) * cap

    @pl.when(e == 0)
    def _():
        o_ref[...] = jnp.zeros_like(o_ref)

    def body(g, carry):
        c0 = g * SCATTER_GROUP
        rows = [idx_ref[base + c0 + i] for i in range(SCATTER_GROUP)]
        new = [o_ref[_tile(r), :] + y_ref[_tile(c0 + i), :] for i, r in enumerate(rows)]
        for r, v in zip(rows, new):
            o_ref[_tile(r), :] = v
        return carry

    lax.fori_loop(0, cap // SCATTER_GROUP, body, 0)


def _scatter_add(idx_flat, ye, seq):
    batch, n_e, rows, _ = ye.shape
    cap = rows // SUBLANES
    return pl.pallas_call(
        functools.partial(_scatter_kernel, cap=cap, n_e=n_e),
        grid_spec=pltpu.PrefetchScalarGridSpec(
            num_scalar_prefetch=1,
            grid=(batch, n_e),
            in_specs=[pl.BlockSpec((None, None, rows, LANES), lambda b, e, idx: (b, e, 0, 0))],
            out_specs=pl.BlockSpec((None, seq * SUBLANES, LANES), lambda b, e, idx: (b, 0, 0),
                                   pipeline_mode=pl.Buffered(1)),
        ),
        out_shape=jax.ShapeDtypeStruct((batch, seq * SUBLANES, LANES), F32),
        compiler_params=_cparams(("parallel", "arbitrary")),
        name="moe_scatter_add",
    )(idx_flat, ye)


def _ple_kernel(x_ref, d_ref, p_ref, g_ref, wp_ref, wg_ref, fg_ref, o_ref, *, final):
    x = x_ref[...] + _rows_from_tiles(d_ref)
    gate = _sigmoid(_dot(_rms(x, g_ref[...]), wg_ref[...]))
    out = x + _dot(p_ref[...], wp_ref[...]) * gate
    if final:
        out = _rms(out, fg_ref[...])
    o_ref[...] = out


def _ple(x2d, delta_tiles, p_all, layer, g, w_proj, w_gate, final_g, final):
    m, d = x2d.shape
    dp = p_all.shape[1]
    tm = min(512, m)
    first = layer * (m // tm)
    full = lambda shape: pl.BlockSpec(shape, lambda i: (0,) * len(shape))
    return pl.pallas_call(
        functools.partial(_ple_kernel, final=final),
        grid=(m // tm,),
        in_specs=[pl.BlockSpec((tm, d), lambda i: (i, 0)),
                  pl.BlockSpec((tm * SUBLANES, LANES), lambda i: (i, 0)),
                  pl.BlockSpec((tm, dp), lambda i: (first + i, 0)),
                  full((1, d)), full((dp, d)), full((d, d)), full((1, d))],
        out_specs=pl.BlockSpec((tm, d), lambda i: (i, 0)),
        out_shape=jax.ShapeDtypeStruct((m, d), F32),
        compiler_params=_cparams(("parallel",)),
        name="ple_final" if final else "ple",
    )(x2d, delta_tiles, p_all, g, w_proj, w_gate, final_g)


def _hi_lo(w):
    hi = w.astype(BF16)
    return hi, (w - hi.astype(F32)).astype(BF16)


def _rot_cols(w):
    half = QK_ROPE // 2
    return jnp.concatenate([-w[..., half:], w[..., :half]], axis=-1)


def _pad_head(nope, rope):
    lead = (nope if nope is not None else rope).shape[:-1]
    n = nope if nope is not None else jnp.zeros(lead + (QK_NOPE,), F32)
    r = rope if rope is not None else jnp.zeros(lead + (QK_ROPE,), F32)
    return jnp.concatenate([n, r, jnp.zeros(lead + (LANES - QK_NOPE - QK_ROPE,), F32)], axis=-1)


def _block_rows(w_pair):
    z = jnp.zeros_like(w_pair[0])
    return jnp.concatenate([jnp.concatenate([w_pair[0], z], axis=1),
                            jnp.concatenate([z, w_pair[1]], axis=1)], axis=0)


def kernel(x, p, positions, attn_norm, w_in, rw_mu, rw_w0, rw_w_up, rw_a0, rw_a_up, rw_g_up, rw_k_k,
           rw_k_a, rw_r_k, rw_lnx_w, rw_lnx_b, mla_q_norm, mla_q_up, mla_kv_norm, mla_kv_up,
           mla_o_norm, w_out, ffn_norm, router, exp_w_gate, exp_w_up, exp_w_down, ple_norm,
           ple_proj, ple_gate, final_norm):
    batch, seq, d = x.shape
    depth = w_in.shape[0]
    m = batch * seq
    cap = EC_FACTOR * seq // N_EXPERTS
    w = RW_WIDTH

    cos, sin = _rope_tables(positions)
    hsum = (jnp.arange(w)[:, None] // HEAD_DIM == jnp.arange(w)[None, :] // HEAD_DIM).astype(BF16)
    x2d = x.reshape(m, d)

    for i in range(depth):
        w_mla = w_in[i][:, RW_COLS:]
        w_kr = w_mla[:, Q_LORA + KV_LORA:]
        w_ext = jnp.concatenate([w_in[i][:, :RW_COLS], w_mla[:, :Q_LORA + KV_LORA],
                                 _pad_head(None, w_kr), _pad_head(None, _rot_cols(w_kr))],
                                axis=1).astype(BF16)
        q_up = mla_q_up[i].reshape(Q_LORA, MLA_HEADS, QK_NOPE + QK_ROPE)
        q_a = _pad_head(q_up[..., :QK_NOPE], q_up[..., QK_NOPE:]).reshape(Q_LORA, -1).astype(BF16)
        q_b = _pad_head(None, _rot_cols(q_up[..., QK_NOPE:])).reshape(Q_LORA, -1).astype(BF16)
        kv_up = mla_kv_up[i].reshape(KV_LORA, MLA_HEADS, QK_NOPE + V_HEAD)
        kv_k = _pad_head(kv_up[..., :QK_NOPE], None).reshape(KV_LORA, -1).astype(BF16)
        kv_v = kv_up[..., QK_NOPE:].reshape(KV_LORA, MLA_WIDTH).T.astype(BF16)

        z2d = _in_proj(x2d, attn_norm[i][None, :], w_ext)
        qeff, oloc, g_all, h_all, bonus, gate = _rwkv_a(
            z2d, seq, rw_mu[i][None, :], rw_w0[i].reshape(1, 2 * w), _block_rows(rw_w_up[i]),
            rw_a0[i].reshape(1, 2 * w), _block_rows(rw_a_up[i]), rw_g_up[i], rw_k_k[i][None, :],
            rw_k_a[i][None, :], rw_r_k[i].reshape(1, w), hsum)
        o_f, o_b = _rwkv_b(qeff, oloc, g_all, h_all, batch, seq)
        q, k, v_t, small = _mla_prep(z2d, cos, sin, batch, seq, mla_q_norm[i][None, :],
                                     mla_kv_norm[i][None, :], q_a, q_b, kv_k, kv_v)
        y_mla = _flash(small, q, k, v_t).reshape(m, MLA_WIDTH)
        x2d, x_tiles = _out_proj(o_f, o_b, bonus, gate, y_mla, x2d, hsum, rw_lnx_w[i][None, :],
                                 rw_lnx_b[i][None, :], mla_o_norm[i][None, :],
                                 w_out[i].astype(BF16))

        aff_t = _router(x2d.reshape(batch, seq, d), ffn_norm[i][None, :], router[i].T)
        idx_flat, gates = _select(aff_t, cap)
        xe = _gather(idx_flat, x_tiles.reshape(batch, seq * SUBLANES, LANES), N_EXPERTS, cap)
        ye = _expert_ffn(xe, gates, ffn_norm[i][None, :], exp_w_gate, exp_w_up, exp_w_down, i)
        delta = _scatter_add(idx_flat, ye, seq).reshape(m * SUBLANES, LANES)

        x2d = _ple(x2d, delta, p.reshape(depth * m, -1), i, ple_norm[i][None, :], ple_proj[i].astype(BF16),
                   ple_gate[i].astype(BF16), final_norm[None, :], final=(i == depth - 1))

    return x2d.reshape(batch, seq, d)
```

```python
import functools
import math

import jax
import jax.numpy as jnp
from jax import lax
from jax.experimental import pallas as pl
from jax.experimental.pallas import tpu as pltpu

F32 = jnp.float32
BF16 = jnp.bfloat16
I32 = jnp.int32

RW_HEADS = 8
HEAD_DIM = 64
RW_WIDTH = RW_HEADS * HEAD_DIM
RW_COLS = 3 * RW_WIDTH + 2 * 64 + 2 * 64 + 128
MLA_HEADS = 8
QK_NOPE = 64
QK_ROPE = 32
V_HEAD = 64
Q_LORA = 256
KV_LORA = 128
MLA_WIDTH = MLA_HEADS * V_HEAD
MLA_IN = Q_LORA + KV_LORA + 2 * 128
ROPE_THETA = 10000.0
N_EXPERTS = 16
EC_FACTOR = 2
NORM_EPS = 1e-6
GN_EPS = 64e-5

LANES = 128
SUBLANES = 8
CHUNK = 64
CHUNKS_PER_ITER = 4
PAIR = 2 * HEAD_DIM
N_PAIRS = RW_WIDTH // PAIR
VMEM_LIMIT = 56 * 1024 * 1024
FLASH_TQ = 2048
FLASH_TK = 2048
FLASH_SUB = 256
ONES_ROWS = 16
FLASH_SAFE_LOG2 = 40.0
NORM_MARGIN = 1.05


def _cparams(sem):
    return pltpu.CompilerParams(dimension_semantics=sem, vmem_limit_bytes=VMEM_LIMIT)


def _rms(x, g):
    return x * lax.rsqrt(jnp.mean(x * x, axis=-1, keepdims=True) + NORM_EPS) * g


def _sigmoid(x):
    return 1.0 / (1.0 + jnp.exp(-x))


def _dot(a, b):
    return jnp.dot(a.astype(BF16), b.astype(BF16), preferred_element_type=F32)


def _split_bf16(x, parts):
    out = []
    rest = x
    for _ in range(parts):
        hi = rest.astype(BF16)
        out.append(hi)
        rest = rest - hi.astype(F32)
    return out


def _dot_lsplit(a, b_exact, parts):
    acc = None
    for term in _split_bf16(a, parts):
        d = jnp.dot(term, b_exact, preferred_element_type=F32)
        acc = d if acc is None else acc + d
    return acc


def _dot_rsplit(a_exact, b, parts):
    acc = None
    for term in _split_bf16(b, parts):
        d = jnp.dot(a_exact, term, preferred_element_type=F32)
        acc = d if acc is None else acc + d
    return acc


def _dot_tn(a, b):
    return lax.dot_general(a.astype(BF16), b.astype(BF16), (((0,), (0,)), ((), ())),
                           preferred_element_type=F32)


def _dot_nt(a, b):
    return lax.dot_general(a.astype(BF16), b.astype(BF16), (((1,), (1,)), ((), ())),
                           preferred_element_type=F32)


def _rope_kernel(pos_ref, inv_ref, c_ref, s_ref):
    ang = pos_ref[...] * inv_ref[...]
    c_ref[...] = jnp.cos(ang)
    s_ref[...] = jnp.sin(ang)


def _rope_tables(positions):
    b, t = positions.shape
    m = b * t
    inv = ROPE_THETA ** (-jnp.arange(0, QK_ROPE, 2, dtype=F32) / QK_ROPE)
    inv_row = jnp.concatenate([jnp.zeros((QK_NOPE,), F32), inv, inv,
                               jnp.zeros((LANES - QK_NOPE - QK_ROPE,), F32)])[None, :]
    posf = jnp.broadcast_to(positions.astype(F32).reshape(m, 1), (m, LANES))
    tm = min(1024, m)
    return pl.pallas_call(
        _rope_kernel,
        grid=(m // tm,),
        in_specs=[pl.BlockSpec((tm, LANES), lambda i: (i, 0)),
                  pl.BlockSpec((1, LANES), lambda i: (0, 0))],
        out_specs=[pl.BlockSpec((tm, LANES), lambda i: (i, 0))] * 2,
        out_shape=[jax.ShapeDtypeStruct((m, LANES), F32)] * 2,
        compiler_params=_cparams(("parallel",)),
        name="rope_tables",
    )(posf, inv_row)


def _in_kernel(x_ref, g_ref, w_ref, o_ref):
    h = _rms(x_ref[...], g_ref[...])
    o_ref[...] = jnp.dot(h.astype(BF16), w_ref[...], preferred_element_type=F32)


def _in_proj(x2d, g, w_ext):
    m, d = x2d.shape
    n = w_ext.shape[1]
    tm = min(512, m)
    return pl.pallas_call(
        _in_kernel,
        grid=(m // tm,),
        in_specs=[pl.BlockSpec((tm, d), lambda i: (i, 0)),
                  pl.BlockSpec((1, d), lambda i: (0, 0)),
                  pl.BlockSpec((d, n), lambda i: (0, 0))],
        out_specs=pl.BlockSpec((tm, n), lambda i: (i, 0)),
        out_shape=jax.ShapeDtypeStruct((m, n), F32),
        compiler_params=_cparams(("parallel",)),
        name="in_proj",
    )(x2d, g, w_ext)


def _pair_masks():
    i = lax.broadcasted_iota(I32, (PAIR, PAIR), 0)
    j = lax.broadcasted_iota(I32, (PAIR, PAIR), 1)
    same = (i // CHUNK) == (j // CHUNK)
    li = i % CHUNK
    lj = j % CHUNK
    return same, li, lj, i == j


def _chunk_pair(a_t, b_t, k_t, r_t, v, b_h, k_h, g_last, consts):
    n = len(a_t)
    idx = range(n)
    strict = [c[0] for c in consts]
    incl = [c[1] for c in consts]
    levels = [c[2] for c in consts]
    eye, m0, m1 = consts[0][3:6]
    reverse = [c[6] for c in consts]

    def stack(x):
        return jnp.concatenate([jnp.where(m0, x, jnp.zeros_like(x)),
                                jnp.where(m1, x, jnp.zeros_like(x))], axis=0)

    a2, b2, k2, v2, bh2, kh2, r2 = ([stack(x) for x in xs] for xs in (a_t, b_t, k_t, v, b_h, k_h, r_t))
    prod = [_dot_nt(jnp.concatenate([a2[i], r2[i].astype(BF16)], axis=0),
                    jnp.concatenate([b2[i], k2[i]], axis=0)) for i in idx]
    n_mat = [jnp.where(strict[i], prod[i][:PAIR, :PAIR], 0.0) for i in idx]
    m_ak = [jnp.where(strict[i], prod[i][:PAIR, PAIR:], 0.0).astype(BF16) for i in idx]
    m_rb = [jnp.where(incl[i], prod[i][PAIR:, :PAIR], 0.0).astype(BF16) for i in idx]
    m_rk = [jnp.where(incl[i], prod[i][PAIR:, PAIR:], 0.0).astype(BF16) for i in idx]
    mv = [_dot(jnp.concatenate([m_ak[i], m_rk[i]], axis=0), v2[i]) for i in idx]

    x = [jnp.where(eye, 1.0, 0.0) - jnp.where(levels[i][0], n_mat[i], 0.0) for i in idx]
    n_bf = [n_mat[i].astype(BF16) for i in idx]
    zero = jnp.zeros((PAIR, PAIR), BF16)
    for lv in range(1, len(levels[0])):
        s = 2 ** lv
        if s < SUBLANES:
            cx = [_dot(jnp.where(levels[i][lv], n_bf[i], zero), x[i]) for i in idx]
            x = [x[i] - _dot(x[i], cx[i]) for i in idx]
            continue
        blocks = [(r, r + s) for r in range(0, PAIR, s)]
        upd = [[((r % CHUNK) // s) % 2 == (0 if reverse[i] else 1) for r, _ in blocks] for i in idx]

        def take(mat, i):
            return jnp.concatenate([mat[r0:r1] for (r0, r1), u in zip(blocks, upd[i]) if u], axis=0)

        c_h = [take(jnp.where(levels[i][lv], n_mat[i], 0.0), i) for i in idx]
        cx_h = [_dot(c_h[i], x[i]) for i in idx]
        zrows = jnp.zeros((s, PAIR), F32)
        cx = []
        for i in idx:
            it = iter(range(CHUNK // s))
            cx.append(jnp.concatenate(
                [cx_h[i][k * s:(k + 1) * s] if u else zrows
                 for u in upd[i] for k in ([next(it)] if u else [0])], axis=0))
        du = [_dot(take(x[i], i), cx[i]) for i in idx]
        x_new = []
        for i in idx:
            it = iter(range(CHUNK // s))
            x_new.append(jnp.concatenate(
                [x[i][r0:r1] - du[i][k * s:(k + 1) * s] if u else x[i][r0:r1]
                 for (r0, r1), u in zip(blocks, upd[i]) for k in ([next(it)] if u else [0])], axis=0))
        x = x_new

    tw = [_dot(x[i], jnp.concatenate([a2[i], mv[i][:PAIR].astype(BF16)], axis=1))
          for i in idx]
    tw_bf = [t.astype(BF16) for t in tw]
    qo = [jnp.concatenate([r2[i], mv[i][PAIR:]], axis=1) - _dot(m_rb[i], tw_bf[i])
          for i in idx]
    bt = [_dot_tn(bh2[i], tw_bf[i]) for i in idx]
    kv = [_dot_tn(kh2[i], v2[i]) for i in idx]
    out = []
    for i in idx:
        g_mat = jnp.where(eye, g_last[i], 0.0) - bt[i][:, :PAIR]
        h_mat = kv[i] - bt[i][:, PAIR:]
        q = qo[i][:CHUNK] + qo[i][CHUNK:]
        out.append((q[:, :PAIR], q[:, PAIR:], g_mat, h_mat))
    return out


def _rwkv_a_kernel(z_ref, zp_ref, zn_ref, mu_ref, w0_ref, wup_ref, a0_ref, aup_ref,
                   gup_ref, kk_ref, ka_ref, rk_ref, hsum_ref, trif_ref, trib_ref,
                   q_out, ol_out, g_out, h_out, bonus_out, gate_out,
                   at_s, bt_s, kt_s, rt_s, bh_s, kh_s, v_s, gl_s, *, tm, seq):
    i = pl.program_id(0)
    z = z_ref[...]
    has_prev = (i * tm) % seq != 0
    has_next = ((i + 1) * tm) % seq != 0
    prev_row = jnp.where(has_prev, zp_ref[SUBLANES - 1:SUBLANES, :], 0.0)
    next_row = jnp.where(has_next, zn_ref[0:1, :], 0.0)
    sub = lax.broadcasted_iota(I32, (SUBLANES, 1), 0)
    z_dn = pltpu.roll(z, 1, axis=0)
    z_dn = jnp.concatenate([jnp.where(sub == 0, prev_row, z_dn[:SUBLANES]), z_dn[SUBLANES:]], axis=0)
    z_up = pltpu.roll(z, tm - 1, axis=0)
    z_up = jnp.concatenate([z_up[:tm - SUBLANES],
                            jnp.where(sub == SUBLANES - 1, next_row, z_up[tm - SUBLANES:])], axis=0)
    zs = z + mu_ref[...] * (0.5 * (z_dn + z_up) - z)

    w = RW_WIDTH
    r = zs[:, :w]
    k = zs[:, w:2 * w]
    v = zs[:, 2 * w:3 * w]
    wd = zs[:, 3 * w:3 * w + 128]
    ad = zs[:, 3 * w + 128:3 * w + 256]
    gd = zs[:, 3 * w + 256:3 * w + 384]

    hsum = hsum_ref[...]
    w_logit = w0_ref[...] + _dot(jnp.tanh(wd), wup_ref[...])
    lw = -_sigmoid(w_logit) * jnp.exp(jnp.float32(-0.5))
    a = _sigmoid(a0_ref[...] + _dot(ad, aup_ref[...]))
    gate_out[...] = _dot(_sigmoid(gd), gup_ref[...]).astype(BF16)
    kkr = k * kk_ref[...]
    kkn = kkr * jnp.minimum(lax.rsqrt(_dot(kkr * kkr, hsum)), 1e12)
    kd = [k * (1.0 + (a[:, d * w:(d + 1) * w] - 1.0) * ka_ref[...]) for d in range(2)]
    bonus_out[...] = (_dot(r * (0.5 * (kd[0] + kd[1])) * rk_ref[...], hsum) * v).astype(BF16)
    v_s[...] = v.astype(BF16)

    for d, tri_ref in enumerate((trif_ref, trib_ref)):
        cols = slice(d * w, (d + 1) * w)
        lw_d = lw[:, cols]
        cum = _dot_rsplit(tri_ref[...], lw_d, 2)
        ends = [c * CHUNK if d == 1 else (c + 1) * CHUNK - 1 for c in range(tm // CHUNK)]
        tot = jnp.concatenate([jnp.broadcast_to(cum[e:e + 1, :], (CHUNK, w)) for e in ends], axis=0)
        g_inv = jnp.exp(-cum)
        g_end = jnp.exp(tot - cum)
        b = kkn * a[:, cols]
        at_s[d] = (kkn * jnp.exp(cum - lw_d)).astype(BF16)
        bt_s[d] = (b * g_inv).astype(BF16)
        kt_s[d] = (kd[d] * g_inv).astype(BF16)
        rt_s[d] = r * jnp.exp(cum)
        bh_s[d] = (b * g_end).astype(BF16)
        kh_s[d] = (kd[d] * g_end).astype(BF16)
        gl_s[d] = jnp.exp(tot)

    same, li, lj, eye = _pair_masks()
    lane = lax.broadcasted_iota(I32, (1, PAIR), 1)
    m0 = lane < HEAD_DIM
    m1 = lane >= HEAD_DIM
    consts = []
    for reverse in (False, True):
        before = (lj > li) if reverse else (lj < li)
        strict = same & before
        incl = same & (before | (li == lj))
        levels = []
        s = 1
        while s < CHUNK:
            blk = same & ((li // (2 * s)) == (lj // (2 * s)))
            hi_row = (li // s) % 2 == 1
            hi_col = (lj // s) % 2 == 1
            levels.append(blk & ((~hi_row & hi_col) if reverse else (hi_row & ~hi_col)))
            s *= 2
        consts.append((strict, incl, levels, eye, m0, m1, reverse))

    def chunk_body(it, carry):
        inst = []
        for j in range(CHUNKS_PER_ITER):
            c = it * CHUNKS_PER_ITER + j
            r0 = pl.multiple_of(c * CHUNK, CHUNK)
            inst += [(c, r0, pl.ds(r0, CHUNK), d, slice(p * PAIR, (p + 1) * PAIR))
                     for d in range(2) for p in range(N_PAIRS)]
        outs = _chunk_pair(
            [at_s[d, rows, ln] for c, r0, rows, d, ln in inst],
            [bt_s[d, rows, ln] for c, r0, rows, d, ln in inst],
            [kt_s[d, rows, ln] for c, r0, rows, d, ln in inst],
            [rt_s[d, rows, ln] for c, r0, rows, d, ln in inst],
            [v_s[rows, ln] for c, r0, rows, d, ln in inst],
            [bh_s[d, rows, ln] for c, r0, rows, d, ln in inst],
            [kh_s[d, rows, ln] for c, r0, rows, d, ln in inst],
            [gl_s[d, pl.ds(r0, 1), ln] for c, r0, rows, d, ln in inst],
            [consts[d] for c, r0, rows, d, ln in inst])
        for (c, r0, rows, d, ln), (qe, ol, g_mat, h_mat) in zip(inst, outs):
            q_out[d, rows, ln] = qe.astype(BF16)
            ol_out[d, rows, ln] = ol.astype(BF16)
            g_out[d, c, :, ln] = (g_mat[:HEAD_DIM] + g_mat[HEAD_DIM:]).astype(BF16)
            h_out[d, c, :, ln] = (h_mat[:HEAD_DIM] + h_mat[HEAD_DIM:]).astype(BF16)
        return carry

    lax.fori_loop(0, tm // (CHUNK * CHUNKS_PER_ITER), chunk_body, 0)


def _rwkv_a(z2d, seq, mu, w0, wup, a0, aup, gup, k_k, k_a, r_k, hsum):
    m = z2d.shape[0]
    tm = min(256, seq)
    nc = tm // CHUNK
    w = RW_WIDTH
    full = lambda shape: pl.BlockSpec(shape, lambda i: (0,) * len(shape))
    last8 = m // 8 - 1
    ti = jnp.arange(tm)[:, None]
    tj = jnp.arange(tm)[None, :]
    same_chunk = (ti // CHUNK) == (tj // CHUNK)
    tri_f = (same_chunk & (tj <= ti)).astype(BF16)
    tri_b = (same_chunk & (tj >= ti)).astype(BF16)
    kern = functools.partial(_rwkv_a_kernel, tm=tm, seq=seq)
    return pl.pallas_call(
        kern,
        grid=(m // tm,),
        in_specs=[
            pl.BlockSpec((tm, RW_COLS), lambda i: (i, 0)),
            pl.BlockSpec((8, RW_COLS), lambda i: (jnp.maximum(i * (tm // 8) - 1, 0), 0)),
            pl.BlockSpec((8, RW_COLS), lambda i: (jnp.minimum((i + 1) * (tm // 8), last8), 0)),
            full((1, RW_COLS)), full((1, 2 * w)), full((128, 2 * w)),
            full((1, 2 * w)), full((128, 2 * w)), full((128, w)),
            full((1, w)), full((1, w)), full((1, w)), full((w, w)), full((tm, tm)), full((tm, tm)),
        ],
        out_specs=[
            pl.BlockSpec((2, tm, w), lambda i: (0, i, 0)),
            pl.BlockSpec((2, tm, w), lambda i: (0, i, 0)),
            pl.BlockSpec((2, nc, HEAD_DIM, w), lambda i: (0, i, 0, 0)),
            pl.BlockSpec((2, nc, HEAD_DIM, w), lambda i: (0, i, 0, 0)),
            pl.BlockSpec((tm, w), lambda i: (i, 0)),
            pl.BlockSpec((tm, w), lambda i: (i, 0)),
        ],
        out_shape=[
            jax.ShapeDtypeStruct((2, m, w), BF16),
            jax.ShapeDtypeStruct((2, m, w), BF16),
            jax.ShapeDtypeStruct((2, m // CHUNK, HEAD_DIM, w), BF16),
            jax.ShapeDtypeStruct((2, m // CHUNK, HEAD_DIM, w), BF16),
            jax.ShapeDtypeStruct((m, w), BF16),
            jax.ShapeDtypeStruct((m, w), BF16),
        ],
        scratch_shapes=[pltpu.VMEM((2, tm, w), BF16)] * 3 + [pltpu.VMEM((2, tm, w), F32)]
        + [pltpu.VMEM((2, tm, w), BF16)] * 2 + [pltpu.VMEM((tm, w), BF16), pltpu.VMEM((2, tm, w), F32)],
        compiler_params=_cparams(("parallel",)),
        name="rwkv_chunk_local",
    )(z2d, z2d, z2d, mu, w0, wup.astype(BF16), a0, aup.astype(BF16), gup.astype(BF16), k_k, k_a, r_k,
      hsum, tri_f, tri_b)


def _rwkv_b_kernel(qf_ref, olf_ref, gf_ref, hf_ref, qb_ref, olb_ref, gb_ref, hb_ref,
                   of_ref, ob_ref, s_ref, *, cb):
    @pl.when(pl.program_id(1) == 0)
    def _():
        s_ref[...] = jnp.zeros_like(s_ref)

    lane = lax.broadcasted_iota(I32, (1, PAIR), 1)

    def block_diag(packed):
        zero = jnp.zeros_like(packed)
        return jnp.concatenate([jnp.where(lane < HEAD_DIM, packed, zero),
                                jnp.where(lane >= HEAD_DIM, packed, zero)], axis=0)

    for step in range(cb):
        inst = []
        for d, refs in enumerate(((qf_ref, olf_ref, gf_ref, hf_ref, of_ref),
                                  (qb_ref, olb_ref, gb_ref, hb_ref, ob_ref))):
            c = cb - 1 - step if d == 1 else step
            for p in range(N_PAIRS):
                inst.append((d, c, slice(c * CHUNK, (c + 1) * CHUNK),
                             slice(p * PAIR, (p + 1) * PAIR)) + refs)
        s_bf = [s_ref[d, :, ln].astype(BF16) for d, c, rows, ln, *_ in inst]
        s_new = [jnp.dot(block_diag(g_ref[c, :, ln]), sb, preferred_element_type=F32)
                 + block_diag(h_ref[c, :, ln])
                 for (d, c, rows, ln, q_ref, ol_ref, g_ref, h_ref, o_ref), sb in zip(inst, s_bf)]
        o_val = [jnp.dot(q_ref[rows, ln], sb, preferred_element_type=F32) + ol_ref[rows, ln]
                 for (d, c, rows, ln, q_ref, ol_ref, g_ref, h_ref, o_ref), sb in zip(inst, s_bf)]
        for (d, c, rows, ln, q_ref, ol_ref, g_ref, h_ref, o_ref), sn, ov in zip(inst, s_new, o_val):
            s_ref[d, :, ln] = sn
            o_ref[rows, ln] = ov.astype(BF16)


def _rwkv_b(qeff, oloc, g_all, h_all, batch, seq):
    m = batch * seq
    w = RW_WIDTH
    cb = min(8, seq // CHUNK)
    tm = cb * CHUNK
    nb = seq // tm

    def fwd(b, j):
        return b * nb + j

    def bwd(b, j):
        return b * nb + nb - 1 - j

    def specs(d, blk):
        return [
            pl.BlockSpec((None, tm, w), lambda b, j: (d, blk(b, j), 0)),
            pl.BlockSpec((None, tm, w), lambda b, j: (d, blk(b, j), 0)),
            pl.BlockSpec((None, cb, HEAD_DIM, w), lambda b, j: (d, blk(b, j), 0, 0)),
            pl.BlockSpec((None, cb, HEAD_DIM, w), lambda b, j: (d, blk(b, j), 0, 0)),
        ]

    return pl.pallas_call(
        functools.partial(_rwkv_b_kernel, cb=cb),
        grid=(batch, nb),
        in_specs=specs(0, fwd) + specs(1, bwd),
        out_specs=[pl.BlockSpec((tm, w), lambda b, j: (fwd(b, j), 0)),
                   pl.BlockSpec((tm, w), lambda b, j: (bwd(b, j), 0))],
        out_shape=[jax.ShapeDtypeStruct((m, w), BF16)] * 2,
        scratch_shapes=[pltpu.VMEM((2, PAIR, w), F32)],
        compiler_params=_cparams(("parallel", "arbitrary")),
        name="rwkv_recurrence",
    )(qeff, oloc, g_all, h_all, qeff, oloc, g_all, h_all)


def _mla_prep_kernel(z_ref, c_ref, s_ref, qn_ref, kvn_ref, qa_ref, qb_ref, kk_ref, kvv_ref, hsel_ref,
                     q_out, k_out, v_out, qmax_out, kmax_out, *, scale):
    z = z_ref[...]
    cos = c_ref[...]
    sin = s_ref[...]
    qd = _rms(z[:, :Q_LORA], qn_ref[...]).astype(BF16)
    kvd = _rms(z[:, Q_LORA:Q_LORA + KV_LORA], kvn_ref[...]).astype(BF16)
    o = Q_LORA + KV_LORA
    kr = z[:, o:o + LANES] * cos + z[:, o + LANES:o + 2 * LANES] * sin
    qa = jnp.dot(qd, qa_ref[...], preferred_element_type=F32)
    qb = jnp.dot(qd, qb_ref[...], preferred_element_type=F32)
    kn = jnp.dot(kvd, kk_ref[...], preferred_element_type=F32)
    v_out[...] = lax.dot_general(kvv_ref[...], kvd, (((1,), (1,)), ((), ())),
                                 preferred_element_type=F32).astype(BF16)
    qs, ks = [], []
    for h in range(MLA_HEADS):
        lanes = slice(h * LANES, (h + 1) * LANES)
        qs.append(((qa[:, lanes] * cos + qb[:, lanes] * sin) * scale).astype(BF16))
        ks.append((kn[:, lanes] + kr).astype(BF16))
        q_out[h] = qs[h]
        k_out[h] = ks[h]
    for vals, out in ((qs, qmax_out), (ks, kmax_out)):
        full = jnp.concatenate([v.astype(F32) for v in vals], axis=1)
        n2 = jnp.dot((full * full).astype(BF16), hsel_ref[...], preferred_element_type=F32)
        out[...] = jnp.broadcast_to(jnp.max(n2, axis=0, keepdims=True), (SUBLANES, LANES))


def _mla_prep(z2d, cos, sin, batch, seq, q_norm, kv_norm, q_a, q_b, kv_k, kv_v):
    tm = min(512, seq)
    nt = seq // tm
    hw = MLA_HEADS * LANES
    scale = float((QK_NOPE + QK_ROPE) ** -0.5 * math.log2(math.e))
    full = lambda shape: pl.BlockSpec(shape, lambda b, i: (0,) * len(shape))
    col_blk = RW_COLS // MLA_IN
    assert col_blk * MLA_IN == RW_COLS
    head_sel = (jnp.arange(hw)[:, None] // LANES == jnp.arange(LANES)[None, :]).astype(BF16)
    q, k, v_t, qmax, kmax = pl.pallas_call(
        functools.partial(_mla_prep_kernel, scale=scale),
        grid=(batch, nt),
        in_specs=[
            pl.BlockSpec((tm, MLA_IN), lambda b, i: (b * nt + i, col_blk)),
            pl.BlockSpec((tm, LANES), lambda b, i: (b * nt + i, 0)),
            pl.BlockSpec((tm, LANES), lambda b, i: (b * nt + i, 0)),
            full((1, Q_LORA)), full((1, KV_LORA)), full((Q_LORA, hw)), full((Q_LORA, hw)),
            full((KV_LORA, hw)), full((MLA_WIDTH, KV_LORA)), full((hw, LANES)),
        ],
        out_specs=[
            pl.BlockSpec((None, MLA_HEADS, tm, LANES), lambda b, i: (b, 0, i, 0)),
            pl.BlockSpec((None, MLA_HEADS, tm, LANES), lambda b, i: (b, 0, i, 0)),
            pl.BlockSpec((None, MLA_WIDTH, tm), lambda b, i: (b, 0, i)),
            pl.BlockSpec((None, None, SUBLANES, LANES), lambda b, i: (b, i, 0, 0)),
            pl.BlockSpec((None, None, SUBLANES, LANES), lambda b, i: (b, i, 0, 0)),
        ],
        out_shape=[
            jax.ShapeDtypeStruct((batch, MLA_HEADS, seq, LANES), BF16),
            jax.ShapeDtypeStruct((batch, MLA_HEADS, seq, LANES), BF16),
            jax.ShapeDtypeStruct((batch, MLA_WIDTH, seq), BF16),
            jax.ShapeDtypeStruct((batch, nt, SUBLANES, LANES), F32),
            jax.ShapeDtypeStruct((batch, nt, SUBLANES, LANES), F32),
        ],
        compiler_params=_cparams(("parallel", "parallel")),
        name="mla_prep",
    )(z2d, cos, sin, q_norm, kv_norm, q_a, q_b, kv_k, kv_v, head_sel)
    bound = jnp.sqrt(jnp.max(qmax, axis=(1, 2)) * jnp.max(kmax, axis=(1, 2)))[:, :MLA_HEADS] * NORM_MARGIN
    small = (bound <= FLASH_SAFE_LOG2).reshape(batch, MLA_HEADS // 2, 2).all(axis=-1)
    return q, k, v_t, small.astype(I32).reshape(-1)


def _flash_kernel(small_ref, q_ref, k_ref, vt_ref, o_ref, m_ref, l_ref, acc_ref):
    j = pl.program_id(3)
    small = small_ref[pl.program_id(0) * pl.num_programs(1) + pl.program_id(1)] != 0

    @pl.when(j == 0)
    def _():
        m_ref[...] = jnp.full_like(m_ref, -jnp.inf)
        l_ref[...] = jnp.zeros_like(l_ref)
        acc_ref[...] = jnp.zeros_like(acc_ref)

    tk = k_ref.shape[1]
    sub = min(FLASH_SUB, tk)
    inst = [(h, slice(b * sub, (b + 1) * sub)) for b in range(tk // sub) for h in range(2)]
    rows = [slice(h * V_HEAD, (h + 1) * V_HEAD) for h in range(2)]
    ones = jnp.ones((ONES_ROWS, sub), BF16)

    def scores():
        return [lax.dot_general(k_ref[h, kb, :], q_ref[h], (((1,), (1,)), ((), ())),
                                preferred_element_type=F32) for h, kb in inst]

    def weighted_values(p):
        pv = [jnp.dot(jnp.concatenate([vt_ref[rows[h], kb], ones], axis=0), x,
                      preferred_element_type=F32) for (h, kb), x in zip(inst, p)]
        return [x[:V_HEAD, :] for x in pv], [x[V_HEAD:V_HEAD + 1, :] for x in pv]

    @pl.when(small)
    def _():
        pv, l_loc = weighted_values([jnp.exp2(x).astype(BF16) for x in scores()])
        for h in range(2):
            mine = [i for i, (hh, _) in enumerate(inst) if hh == h]
            l_ref[h:h + 1, :] = l_ref[h:h + 1, :] + sum(l_loc[i] for i in mine)
            acc_ref[rows[h], :] = acc_ref[rows[h], :] + sum(pv[i] for i in mine)

    @pl.when(jnp.logical_not(small))
    def _():
        s = scores()
        m_loc = [jnp.max(x, axis=0, keepdims=True) for x in s]
        pv, l_loc = weighted_values([jnp.exp2(x - m).astype(BF16) for x, m in zip(s, m_loc)])
        for h in range(2):
            mine = [i for i, (hh, _) in enumerate(inst) if hh == h]
            m_prev = m_ref[h:h + 1, :]
            m_new = m_prev
            for i in mine:
                m_new = jnp.maximum(m_new, m_loc[i])
            alpha = jnp.exp2(m_prev - m_new)
            l_new = alpha * l_ref[h:h + 1, :]
            acc = alpha * acc_ref[rows[h], :]
            for i in mine:
                w = jnp.exp2(m_loc[i] - m_new)
                l_new = l_new + w * l_loc[i]
                acc = acc + w * pv[i]
            m_ref[h:h + 1, :] = m_new
            l_ref[h:h + 1, :] = l_new
            acc_ref[rows[h], :] = acc

    @pl.when(j == pl.num_programs(3) - 1)
    def _():
        inv = 1.0 / l_ref[...]
        o_t = jnp.concatenate([acc_ref[:V_HEAD, :] * inv[0:1, :], acc_ref[V_HEAD:, :] * inv[1:2, :]],
                              axis=0)
        o_ref[...] = o_t.T


def _flash(small, q, k, v_t):
    batch, heads, seq, _ = q.shape
    tq = min(FLASH_TQ, seq)
    tk = min(FLASH_TK, seq)
    return pl.pallas_call(
        _flash_kernel,
        grid_spec=pltpu.PrefetchScalarGridSpec(
            num_scalar_prefetch=1,
            grid=(batch, heads // 2, seq // tq, seq // tk),
            in_specs=[
                pl.BlockSpec((None, 2, tq, LANES), lambda b, p, i, j, sm: (b, p, i, 0)),
                pl.BlockSpec((None, 2, tk, LANES), lambda b, p, i, j, sm: (b, p, j, 0)),
                pl.BlockSpec((None, 2 * V_HEAD, tk), lambda b, p, i, j, sm: (b, p, j)),
            ],
            out_specs=pl.BlockSpec((None, tq, LANES), lambda b, p, i, j, sm: (b, i, p)),
            scratch_shapes=[pltpu.VMEM((2, tq), F32), pltpu.VMEM((2, tq), F32),
                            pltpu.VMEM((2 * V_HEAD, tq), F32)],
        ),
        out_shape=jax.ShapeDtypeStruct((batch, seq, MLA_WIDTH), F32),
        compiler_params=_cparams(("parallel", "parallel", "parallel", "arbitrary")),
        name="mla_flash",
    )(small, q, k, v_t)


def _out_kernel(of_ref, ob_ref, bonus_ref, gate_ref, ym_ref, x_ref, hsum_ref, lw_ref, lb_ref,
                on_ref, w_ref, o_ref, ot_ref):
    o = of_ref[...].astype(F32) + ob_ref[...].astype(F32)
    hsum = hsum_ref[...]
    inv_n = 1.0 / HEAD_DIM
    mean = _dot_lsplit(o, hsum, 2) * inv_n
    d = o - mean
    var = _dot_lsplit(d * d, hsum, 2) * inv_n
    y_rw = (d * lax.rsqrt(var + GN_EPS) * lw_ref[...] + lb_ref[...] + bonus_ref[...]) * gate_ref[...]
    y_mla = _rms(ym_ref[...], on_ref[...])
    w = RW_WIDTH
    out = x_ref[...] + _dot(y_rw, w_ref[:w, :]) + _dot(y_mla, w_ref[w:, :])
    o_ref[...] = out
    _rows_to_tiles(ot_ref, out)


def _out_proj(o_f, o_b, bonus, gate, y_mla, x2d, hsum, lnx_w, lnx_b, o_norm, w_out):
    m, d = x2d.shape
    w = RW_WIDTH
    tm = min(512, m)
    row = lambda n: pl.BlockSpec((tm, n), lambda i: (i, 0))
    full = lambda shape: pl.BlockSpec(shape, lambda i: (0,) * len(shape))
    return pl.pallas_call(
        _out_kernel,
        grid=(m // tm,),
        in_specs=[row(w), row(w), row(w), row(w), row(MLA_WIDTH), row(d), full((w, w)),
                  full((1, w)), full((1, w)), full((1, MLA_WIDTH)), full((w + MLA_WIDTH, d))],
        out_specs=[row(d), pl.BlockSpec((tm * SUBLANES, LANES), lambda i: (i, 0))],
        out_shape=[jax.ShapeDtypeStruct((m, d), F32),
                   jax.ShapeDtypeStruct((m * SUBLANES, LANES), F32)],
        compiler_params=_cparams(("parallel",)),
        name="out_proj",
    )(o_f, o_b, bonus, gate, y_mla, x2d, hsum, lnx_w, lnx_b, o_norm, w_out)


def _router_kernel(x_ref, g_ref, rth_ref, rtl_ref, a_ref):
    x_hi, x_lo = _split_bf16(_rms(x_ref[...], g_ref[...]), 2)
    nt = (((1,), (1,)), ((), ()))
    logits = (lax.dot_general(rth_ref[...], x_hi, nt, preferred_element_type=F32)
              + lax.dot_general(rth_ref[...], x_lo, nt, preferred_element_type=F32)
              + lax.dot_general(rtl_ref[...], x_hi, nt, preferred_element_type=F32))
    mx = jnp.max(logits, axis=0, keepdims=True)
    e = jnp.exp(logits - mx)
    a_ref[...] = e / jnp.sum(e, axis=0, keepdims=True)


def _router(x3d, g, router_t):
    batch, seq, d = x3d.shape
    e = router_t.shape[0]
    tm = min(512, seq)
    return pl.pallas_call(
        _router_kernel,
        grid=(batch, seq // tm),
        in_specs=[pl.BlockSpec((None, tm, d), lambda b, i: (b, i, 0)),
                  pl.BlockSpec((1, d), lambda b, i: (0, 0)),
                  pl.BlockSpec((e, d), lambda b, i: (0, 0)),
                  pl.BlockSpec((e, d), lambda b, i: (0, 0))],
        out_specs=pl.BlockSpec((None, e, tm), lambda b, i: (b, 0, i)),
        out_shape=jax.ShapeDtypeStruct((batch, e, seq), F32),
        compiler_params=_cparams(("parallel", "parallel")),
        name="moe_router",
    )(x3d, g, *_hi_lo(router_t))


def _threshold_kernel(a_ref, thr_ref, *, cap):
    bits = pltpu.bitcast(a_ref[...], I32)
    n_e = bits.shape[0]

    def search(i, cur):
        cand = cur | jnp.left_shift(jnp.int32(1), 30 - i)
        cnt = jnp.sum(jnp.where(bits >= cand, 1, 0), axis=1, keepdims=True)
        return jnp.where(cnt >= cap, cand, cur)

    thr_ref[...] = lax.fori_loop(0, 31, search, jnp.zeros((n_e, 1), I32))


def _compact_kernel(a_ref, thr_ref, idx_ref, gate_ref, *, cap):
    a = a_ref[...]
    nb = a.shape[0]
    bits = pltpu.bitcast(a, I32)
    thr = thr_ref[...]
    ri = lax.broadcasted_iota(I32, (LANES, LANES), 0)
    ci = lax.broadcasted_iota(I32, (LANES, LANES), 1)
    upper = jnp.where(ri <= ci, 1.0, 0.0).astype(BF16)
    bi = lax.broadcasted_iota(I32, (nb, nb), 0)
    bj = lax.broadcasted_iota(I32, (nb, nb), 1)
    before = jnp.where(bj < bi, 1.0, 0.0).astype(BF16)

    def total(x):
        return jnp.sum(jnp.sum(x, axis=1, keepdims=True), axis=0, keepdims=True)

    def running(mask):
        within = jnp.dot(mask, upper, preferred_element_type=F32)
        tot = jnp.broadcast_to(within[:, LANES - 1:LANES], (nb, LANES))
        return within, jnp.dot(before, tot.astype(BF16), preferred_element_type=F32)

    gt = bits > thr
    eq = bits == thr
    need = cap - total(jnp.where(gt, 1.0, 0.0))
    w_eq, b_eq = running(jnp.where(eq, 1.0, 0.0).astype(BF16))
    sel = gt | (eq & (w_eq + b_eq <= need))
    sel_b = jnp.where(sel, 1.0, 0.0).astype(BF16)
    within, base = running(sel_b)

    tot_row = lax.dot_general(jnp.ones((SUBLANES, LANES), BF16), sel_b, (((1,), (1,)), ((), ())),
                              preferred_element_type=F32)
    base_row = jnp.dot(tot_row.astype(BF16), jnp.where(bi < bj, 1.0, 0.0).astype(BF16),
                       preferred_element_type=F32)
    c_col = lax.broadcasted_iota(I32, (cap, 1), 0).astype(F32)
    in_block = (base_row[0:1, :] <= c_col) & (c_col < base_row[0:1, :] + tot_row[0:1, :])
    onehot = jnp.where(in_block, 1.0, 0.0).astype(BF16)

    a_parts = _split_bf16(a, 3)
    base_hi = jnp.floor(base * (1.0 / 32.0))
    lane = lax.broadcasted_iota(I32, (1, LANES), 1).astype(F32)
    block_id = lax.broadcasted_iota(I32, (nb, LANES), 0).astype(F32)
    table = jnp.concatenate(
        [within.astype(BF16), sel_b] + a_parts
        + [base_hi.astype(BF16), (base - 32.0 * base_hi).astype(BF16), block_id.astype(BF16)], axis=1)
    g = jnp.dot(onehot, table, preferred_element_type=F32)
    part = lambda k: g[:, k * LANES:(k + 1) * LANES]
    a_c = part(2) + part(3) + part(4)
    slot = lax.broadcasted_iota(I32, (cap, LANES), 0).astype(F32)
    target = slot - (32.0 * part(5) + part(6)) + 1.0
    match = (part(0) == target) & (part(1) > 0.5)
    pos = jnp.dot(jnp.where(match, lane, 0.0).astype(BF16), jnp.ones((LANES, LANES), BF16),
                  preferred_element_type=F32)
    gate_ref[...] = jnp.sum(jnp.where(match, a_c, 0.0), axis=1, keepdims=True)
    idx_ref[...] = (LANES * part(7) + pos)[:, 0:1].astype(I32)


def _select(aff_t, cap):
    batch, e, seq = aff_t.shape
    nb = seq // LANES
    thr = pl.pallas_call(
        functools.partial(_threshold_kernel, cap=cap),
        grid=(batch,),
        in_specs=[pl.BlockSpec((None, e, seq), lambda b: (b, 0, 0))],
        out_specs=pl.BlockSpec((None, e, 1), lambda b: (b, 0, 0)),
        out_shape=jax.ShapeDtypeStruct((batch, e, 1), I32),
        compiler_params=_cparams(("parallel",)),
        name="moe_threshold",
    )(aff_t)
    idx, gate = pl.pallas_call(
        functools.partial(_compact_kernel, cap=cap),
        grid=(batch, e),
        in_specs=[pl.BlockSpec((None, None, nb, LANES), lambda b, j: (b, j, 0, 0)),
                  pl.BlockSpec((None, None, 1, 1), lambda b, j: (b, j, 0, 0))],
        out_specs=[pl.BlockSpec((None, None, cap, 1), lambda b, j: (b, j, 0, 0))] * 2,
        out_shape=[jax.ShapeDtypeStruct((batch, e, cap, 1), I32),
                   jax.ShapeDtypeStruct((batch, e, cap, 1), F32)],
        compiler_params=_cparams(("parallel", "parallel")),
        name="moe_compact",
    )(aff_t.reshape(batch, e, nb, LANES), thr.reshape(batch, e, 1, 1))
    return idx.reshape(-1), gate


def _rows_from_tiles(ref):
    n = ref.shape[0] // SUBLANES
    return jnp.concatenate([ref[pl.ds(s, n, stride=SUBLANES), :] for s in range(SUBLANES)], axis=-1)


def _rows_to_tiles(ref, val):
    n = val.shape[0]
    for s in range(SUBLANES):
        ref[pl.ds(s, n, stride=SUBLANES), :] = val[:, s * LANES:(s + 1) * LANES]


def _tile(r):
    return pl.ds(pl.multiple_of(r * SUBLANES, SUBLANES), SUBLANES)


GATHER_UNROLL = 16


def _gather_kernel(idx_ref, x_ref, o_ref, *, cap, n_e):
    base = (pl.program_id(0) * n_e + pl.program_id(1)) * cap

    def body(c, carry):
        o_ref[_tile(c), :] = x_ref[_tile(idx_ref[base + c]), :]
        return carry

    lax.fori_loop(0, cap, body, 0, unroll=min(GATHER_UNROLL, cap))


def _gather(idx_flat, x_tiles, n_e, cap):
    batch, rows, _ = x_tiles.shape
    return pl.pallas_call(
        functools.partial(_gather_kernel, cap=cap, n_e=n_e),
        grid_spec=pltpu.PrefetchScalarGridSpec(
            num_scalar_prefetch=1,
            grid=(batch, n_e),
            in_specs=[pl.BlockSpec((None, rows, LANES), lambda b, e, idx: (b, 0, 0),
                                   pipeline_mode=pl.Buffered(1))],
            out_specs=pl.BlockSpec((None, None, cap * SUBLANES, LANES),
                                   lambda b, e, idx: (b, e, 0, 0)),
        ),
        out_shape=jax.ShapeDtypeStruct((batch, n_e, cap * SUBLANES, LANES), F32),
        compiler_params=_cparams(("parallel", "arbitrary")),
        name="moe_gather",
    )(idx_flat, x_tiles)


def _ffn_kernel(x_ref, gate_ref, g_ref, wg_ref, wu_ref, wd_ref, o_ref, wg_s, wu_s, wd_s):
    @pl.when((pl.program_id(1) == 0) & (pl.program_id(2) == 0))
    def _():
        wg_s[...] = wg_ref[...].astype(BF16)
        wu_s[...] = wu_ref[...].astype(BF16)
        wd_s[...] = wd_ref[...].astype(BF16)

    xn = _rms(_rows_from_tiles(x_ref), g_ref[...]).astype(BF16)
    h1 = jnp.dot(xn, wg_s[...], preferred_element_type=F32)
    h2 = jnp.dot(xn, wu_s[...], preferred_element_type=F32)
    hid = (h1 * _sigmoid(h1) * h2).astype(BF16)
    _rows_to_tiles(o_ref, jnp.dot(hid, wd_s[...], preferred_element_type=F32) * gate_ref[...])


def _expert_ffn(xe, gate_col, g, w_gate, w_up, w_down, layer):
    batch, n_e, rows, _ = xe.shape
    cap = rows // SUBLANES
    d, f = w_gate.shape[2:]
    tc = min(512, cap)
    n_c = cap // tc
    steps = batch * n_c

    def weight_spec(shape, k):
        ahead = min(k + 1, steps - 1)

        def index(e, b, c):
            nxt = (b * n_c + c >= ahead) if ahead > 0 else False
            return (layer, jnp.minimum(e + jnp.where(nxt, 1, 0), n_e - 1), 0, 0)

        return pl.BlockSpec((None, None) + shape, index)

    return pl.pallas_call(
        _ffn_kernel,
        grid=(n_e, batch, n_c),
        in_specs=[
            pl.BlockSpec((None, None, tc * SUBLANES, LANES), lambda e, b, c: (b, e, c, 0)),
            pl.BlockSpec((None, None, tc, 1), lambda e, b, c: (b, e, c, 0)),
            pl.BlockSpec((1, d), lambda e, b, c: (0, 0)),
            weight_spec((d, f), 0), weight_spec((d, f), 1), weight_spec((f, d), 2),
        ],
        out_specs=pl.BlockSpec((None, None, tc * SUBLANES, LANES), lambda e, b, c: (b, e, c, 0)),
        out_shape=jax.ShapeDtypeStruct(xe.shape, F32),
        scratch_shapes=[pltpu.VMEM((d, f), BF16), pltpu.VMEM((d, f), BF16), pltpu.VMEM((f, d), BF16)],
        compiler_params=_cparams(("parallel", "arbitrary", "arbitrary")),
        name="moe_ffn",
    )(xe, gate_col, g, w_gate, w_up, w_down)


SCATTER_GROUP = 16


def _scatter_kernel(idx_ref, y_ref, o_ref, *, cap, n_e):
    e = pl.program_id(1)
    base = (pl.program_id(0) * n_e + e) * cap

    @pl.when(e == 0)
    def _():
        o_ref[...] = jnp.zeros_like(o_ref)

    def body(g, carry):
        c0 = g * SCATTER_GROUP
        rows = [idx_ref[base + c0 + i] for i in range(SCATTER_GROUP)]
        new = [o_ref[_tile(r), :] + y_ref[_tile(c0 + i), :] for i, r in enumerate(rows)]
        for r, v in zip(rows, new):
            o_ref[_tile(r), :] = v
        return carry

    lax.fori_loop(0, cap // SCATTER_GROUP, body, 0)


def _scatter_add(idx_flat, ye, seq):
    batch, n_e, rows, _ = ye.shape
    cap = rows // SUBLANES
    return pl.pallas_call(
        functools.partial(_scatter_kernel, cap=cap, n_e=n_e),
        grid_spec=pltpu.PrefetchScalarGridSpec(
            num_scalar_prefetch=1,
            grid=(batch, n_e),
            in_specs=[pl.BlockSpec((None, None, rows, LANES), lambda b, e, idx: (b, e, 0, 0))],
            out_specs=pl.BlockSpec((None, seq * SUBLANES, LANES), lambda b, e, idx: (b, 0, 0),
                                   pipeline_mode=pl.Buffered(1)),
        ),
        out_shape=jax.ShapeDtypeStruct((batch, seq * SUBLANES, LANES), F32),
        compiler_params=_cparams(("parallel", "arbitrary")),
        name="moe_scatter_add",
    )(idx_flat, ye)


def _ple_kernel(x_ref, d_ref, p_ref, g_ref, wp_ref, wg_ref, fg_ref, o_ref, *, final):
    x = x_ref[...] + _rows_from_tiles(d_ref)
    gate = _sigmoid(_dot(_rms(x, g_ref[...]), wg_ref[...]))
    out = x + _dot(p_ref[...], wp_ref[...]) * gate
    if final:
        out = _rms(out, fg_ref[...])
    o_ref[...] = out


def _ple(x2d, delta_tiles, p_all, layer, g, w_proj, w_gate, final_g, final):
    m, d = x2d.shape
    dp = p_all.shape[1]
    tm = min(512, m)
    first = layer * (m // tm)
    full = lambda shape: pl.BlockSpec(shape, lambda i: (0,) * len(shape))
    return pl.pallas_call(
        functools.partial(_ple_kernel, final=final),
        grid=(m // tm,),
        in_specs=[pl.BlockSpec((tm, d), lambda i: (i, 0)),
                  pl.BlockSpec((tm * SUBLANES, LANES), lambda i: (i, 0)),
                  pl.BlockSpec((tm, dp), lambda i: (first + i, 0)),
                  full((1, d)), full((dp, d)), full((d, d)), full((1, d))],
        out_specs=pl.BlockSpec((tm, d), lambda i: (i, 0)),
        out_shape=jax.ShapeDtypeStruct((m, d), F32),
        compiler_params=_cparams(("parallel",)),
        name="ple_final" if final else "ple",
    )(x2d, delta_tiles, p_all, g, w_proj, w_gate, final_g)


def _hi_lo(w):
    hi = w.astype(BF16)
    return hi, (w - hi.astype(F32)).astype(BF16)


def _rot_cols(w):
    half = QK_ROPE // 2
    return jnp.concatenate([-w[..., half:], w[..., :half]], axis=-1)


def _pad_head(nope, rope):
    lead = (nope if nope is not None else rope).shape[:-1]
    n = nope if nope is not None else jnp.zeros(lead + (QK_NOPE,), F32)
    r = rope if rope is not None else jnp.zeros(lead + (QK_ROPE,), F32)
    return jnp.concatenate([n, r, jnp.zeros(lead + (LANES - QK_NOPE - QK_ROPE,), F32)], axis=-1)


def _block_rows(w_pair):
    z = jnp.zeros_like(w_pair[0])
    return jnp.concatenate([jnp.concatenate([w_pair[0], z], axis=1),
                            jnp.concatenate([z, w_pair[1]], axis=1)], axis=0)


def kernel(x, p, positions, attn_norm, w_in, rw_mu, rw_w0, rw_w_up, rw_a0, rw_a_up, rw_g_up, rw_k_k,
           rw_k_a, rw_r_k, rw_lnx_w, rw_lnx_b, mla_q_norm, mla_q_up, mla_kv_norm, mla_kv_up,
           mla_o_norm, w_out, ffn_norm, router, exp_w_gate, exp_w_up, exp_w_down, ple_norm,
           ple_proj, ple_gate, final_norm):
    batch, seq, d = x.shape
    depth = w_in.shape[0]
    m = batch * seq
    cap = EC_FACTOR * seq // N_EXPERTS
    w = RW_WIDTH

    cos, sin = _rope_tables(positions)
    hsum = (jnp.arange(w)[:, None] // HEAD_DIM == jnp.arange(w)[None, :] // HEAD_DIM).astype(BF16)
    x2d = x.reshape(m, d)

    for i in range(depth):
        w_mla = w_in[i][:, RW_COLS:]
        w_kr = w_mla[:, Q_LORA + KV_LORA:]
        w_ext = jnp.concatenate([w_in[i][:, :RW_COLS], w_mla[:, :Q_LORA + KV_LORA],
                                 _pad_head(None, w_kr), _pad_head(None, _rot_cols(w_kr))],
                                axis=1).astype(BF16)
        q_up = mla_q_up[i].reshape(Q_LORA, MLA_HEADS, QK_NOPE + QK_ROPE)
        q_a = _pad_head(q_up[..., :QK_NOPE], q_up[..., QK_NOPE:]).reshape(Q_LORA, -1).astype(BF16)
        q_b = _pad_head(None, _rot_cols(q_up[..., QK_NOPE:])).reshape(Q_LORA, -1).astype(BF16)
        kv_up = mla_kv_up[i].reshape(KV_LORA, MLA_HEADS, QK_NOPE + V_HEAD)
        kv_k = _pad_head(kv_up[..., :QK_NOPE], None).reshape(KV_LORA, -1).astype(BF16)
        kv_v = kv_up[..., QK_NOPE:].reshape(KV_LORA, MLA_WIDTH).T.astype(BF16)

        z2d = _in_proj(x2d, attn_norm[i][None, :], w_ext)
        qeff, oloc, g_all, h_all, bonus, gate = _rwkv_a(
            z2d, seq, rw_mu[i][None, :], rw_w0[i].reshape(1, 2 * w), _block_rows(rw_w_up[i]),
            rw_a0[i].reshape(1, 2 * w), _block_rows(rw_a_up[i]), rw_g_up[i], rw_k_k[i][None, :],
            rw_k_a[i][None, :], rw_r_k[i].reshape(1, w), hsum)
        o_f, o_b = _rwkv_b(qeff, oloc, g_all, h_all, batch, seq)
        q, k, v_t, small = _mla_prep(z2d, cos, sin, batch, seq, mla_q_norm[i][None, :],
                                     mla_kv_norm[i][None, :], q_a, q_b, kv_k, kv_v)
        y_mla = _flash(small, q, k, v_t).reshape(m, MLA_WIDTH)
        x2d, x_tiles = _out_proj(o_f, o_b, bonus, gate, y_mla, x2d, hsum, rw_lnx_w[i][None, :],
                                 rw_lnx_b[i][None, :], mla_o_norm[i][None, :],
                                 w_out[i].astype(BF16))

        aff_t = _router(x2d.reshape(batch, seq, d), ffn_norm[i][None, :], router[i].T)
        idx_flat, gates = _select(aff_t, cap)
        xe = _gather(idx_flat, x_tiles.reshape(batch, seq * SUBLANES, LANES), N_EXPERTS, cap)
        ye = _expert_ffn(xe, gates, ffn_norm[i][None, :], exp_w_gate, exp_w_up, exp_w_down, i)
        delta = _scatter_add(idx_flat, ye, seq).reshape(m * SUBLANES, LANES)

        x2d = _ple(x2d, delta, p.reshape(depth * m, -1), i, ple_norm[i][None, :], ple_proj[i].astype(BF16),
                   ple_gate[i].astype(BF16), final_norm[None, :], final=(i == depth - 1))

    return x2d.reshape(batch, seq, d)
```

```python
import functools
import math

import jax
import jax.numpy as jnp
from jax import lax
from jax.experimental import pallas as pl
from jax.experimental.pallas import tpu as pltpu

F32 = jnp.float32
BF16 = jnp.bfloat16
I32 = jnp.int32

RW_HEADS = 8
HEAD_DIM = 64
RW_WIDTH = RW_HEADS * HEAD_DIM
RW_COLS = 3 * RW_WIDTH + 2 * 64 + 2 * 64 + 128
MLA_HEADS = 8
QK_NOPE = 64
QK_ROPE = 32
V_HEAD = 64
Q_LORA = 256
KV_LORA = 128
MLA_WIDTH = MLA_HEADS * V_HEAD
MLA_IN = Q_LORA + KV_LORA + 2 * 128
ROPE_THETA = 10000.0
N_EXPERTS = 16
EC_FACTOR = 2
NORM_EPS = 1e-6
GN_EPS = 64e-5

LANES = 128
SUBLANES = 8
CHUNK = 64
CHUNKS_PER_ITER = 4
PAIR = 2 * HEAD_DIM
N_PAIRS = RW_WIDTH // PAIR
VMEM_LIMIT = 56 * 1024 * 1024
FLASH_TQ = 2048
FLASH_TK = 2048
FLASH_SUB = 256
ONES_ROWS = 16
FLASH_SAFE_LOG2 = 40.0
NORM_MARGIN = 1.05


def _cparams(sem):
    return pltpu.CompilerParams(dimension_semantics=sem, vmem_limit_bytes=VMEM_LIMIT)


def _rms(x, g):
    return x * lax.rsqrt(jnp.mean(x * x, axis=-1, keepdims=True) + NORM_EPS) * g


def _sigmoid(x):
    return 1.0 / (1.0 + jnp.exp(-x))


def _dot(a, b):
    return jnp.dot(a.astype(BF16), b.astype(BF16), preferred_element_type=F32)


def _split_bf16(x, parts):
    out = []
    rest = x
    for _ in range(parts):
        hi = rest.astype(BF16)
        out.append(hi)
        rest = rest - hi.astype(F32)
    return out


def _dot_lsplit(a, b_exact, parts):
    acc = None
    for term in _split_bf16(a, parts):
        d = jnp.dot(term, b_exact, preferred_element_type=F32)
        acc = d if acc is None else acc + d
    return acc


def _dot_rsplit(a_exact, b, parts):
    acc = None
    for term in _split_bf16(b, parts):
        d = jnp.dot(a_exact, term, preferred_element_type=F32)
        acc = d if acc is None else acc + d
    return acc


def _dot_tn(a, b):
    return lax.dot_general(a.astype(BF16), b.astype(BF16), (((0,), (0,)), ((), ())),
                           preferred_element_type=F32)


def _dot_nt(a, b):
    return lax.dot_general(a.astype(BF16), b.astype(BF16), (((1,), (1,)), ((), ())),
                           preferred_element_type=F32)


def _rope_kernel(pos_ref, inv_ref, c_ref, s_ref):
    ang = pos_ref[...] * inv_ref[...]
    c_ref[...] = jnp.cos(ang)
    s_ref[...] = jnp.sin(ang)


def _rope_tables(positions):
    b, t = positions.shape
    m = b * t
    inv = ROPE_THETA ** (-jnp.arange(0, QK_ROPE, 2, dtype=F32) / QK_ROPE)
    inv_row = jnp.concatenate([jnp.zeros((QK_NOPE,), F32), inv, inv,
                               jnp.zeros((LANES - QK_NOPE - QK_ROPE,), F32)])[None, :]
    posf = jnp.broadcast_to(positions.astype(F32).reshape(m, 1), (m, LANES))
    tm = min(1024, m)
    return pl.pallas_call(
        _rope_kernel,
        grid=(m // tm,),
        in_specs=[pl.BlockSpec((tm, LANES), lambda i: (i, 0)),
                  pl.BlockSpec((1, LANES), lambda i: (0, 0))],
        out_specs=[pl.BlockSpec((tm, LANES), lambda i: (i, 0))] * 2,
        out_shape=[jax.ShapeDtypeStruct((m, LANES), F32)] * 2,
        compiler_params=_cparams(("parallel",)),
        name="rope_tables",
    )(posf, inv_row)


def _in_kernel(x_ref, g_ref, w_ref, o_ref):
    h = _rms(x_ref[...], g_ref[...])
    o_ref[...] = jnp.dot(h.astype(BF16), w_ref[...], preferred_element_type=F32)


def _in_proj(x2d, g, w_ext):
    m, d = x2d.shape
    n = w_ext.shape[1]
    tm = min(512, m)
    return pl.pallas_call(
        _in_kernel,
        grid=(m // tm,),
        in_specs=[pl.BlockSpec((tm, d), lambda i: (i, 0)),
                  pl.BlockSpec((1, d), lambda i: (0, 0)),
                  pl.BlockSpec((d, n), lambda i: (0, 0))],
        out_specs=pl.BlockSpec((tm, n), lambda i: (i, 0)),
        out_shape=jax.ShapeDtypeStruct((m, n), F32),
        compiler_params=_cparams(("parallel",)),
        name="in_proj",
    )(x2d, g, w_ext)


def _pair_masks():
    i = lax.broadcasted_iota(I32, (PAIR, PAIR), 0)
    j = lax.broadcasted_iota(I32, (PAIR, PAIR), 1)
    same = (i // CHUNK) == (j // CHUNK)
    li = i % CHUNK
    lj = j % CHUNK
    return same, li, lj, i == j


def _chunk_pair(a_t, b_t, k_t, r_t, v, b_h, k_h, g_last, consts):
    n = len(a_t)
    idx = range(n)
    strict = [c[0] for c in consts]
    incl = [c[1] for c in consts]
    levels = [c[2] for c in consts]
    eye, m0, m1 = consts[0][3:6]
    reverse = [c[6] for c in consts]

    def stack(x):
        return jnp.concatenate([jnp.where(m0, x, jnp.zeros_like(x)),
                                jnp.where(m1, x, jnp.zeros_like(x))], axis=0)

    a2, b2, k2, v2, bh2, kh2, r2 = ([stack(x) for x in xs] for xs in (a_t, b_t, k_t, v, b_h, k_h, r_t))
    prod = [_dot_nt(jnp.concatenate([a2[i], r2[i].astype(BF16)], axis=0),
                    jnp.concatenate([b2[i], k2[i]], axis=0)) for i in idx]
    n_mat = [jnp.where(strict[i], prod[i][:PAIR, :PAIR], 0.0) for i in idx]
    m_ak = [jnp.where(strict[i], prod[i][:PAIR, PAIR:], 0.0).astype(BF16) for i in idx]
    m_rb = [jnp.where(incl[i], prod[i][PAIR:, :PAIR], 0.0).astype(BF16) for i in idx]
    m_rk = [jnp.where(incl[i], prod[i][PAIR:, PAIR:], 0.0).astype(BF16) for i in idx]
    mv = [_dot(jnp.concatenate([m_ak[i], m_rk[i]], axis=0), v2[i]) for i in idx]

    x = [jnp.where(eye, 1.0, 0.0) - jnp.where(levels[i][0], n_mat[i], 0.0) for i in idx]
    n_bf = [n_mat[i].astype(BF16) for i in idx]
    zero = jnp.zeros((PAIR, PAIR), BF16)
    for lv in range(1, len(levels[0])):
        s = 2 ** lv
        if s < SUBLANES:
            cx = [_dot(jnp.where(levels[i][lv], n_bf[i], zero), x[i]) for i in idx]
            x = [x[i] - _dot(x[i], cx[i]) for i in idx]
            continue
        blocks = [(r, r + s) for r in range(0, PAIR, s)]
        upd = [[((r % CHUNK) // s) % 2 == (0 if reverse[i] else 1) for r, _ in blocks] for i in idx]

        def take(mat, i):
            return jnp.concatenate([mat[r0:r1] for (r0, r1), u in zip(blocks, upd[i]) if u], axis=0)

        c_h = [take(jnp.where(levels[i][lv], n_mat[i], 0.0), i) for i in idx]
        cx_h = [_dot(c_h[i], x[i]) for i in idx]
        zrows = jnp.zeros((s, PAIR), F32)
        cx = []
        for i in idx:
            it = iter(range(CHUNK // s))
            cx.append(jnp.concatenate(
                [cx_h[i][k * s:(k + 1) * s] if u else zrows
                 for u in upd[i] for k in ([next(it)] if u else [0])], axis=0))
        du = [_dot(take(x[i], i), cx[i]) for i in idx]
        x_new = []
        for i in idx:
            it = iter(range(CHUNK // s))
            x_new.append(jnp.concatenate(
                [x[i][r0:r1] - du[i][k * s:(k + 1) * s] if u else x[i][r0:r1]
                 for (r0, r1), u in zip(blocks, upd[i]) for k in ([next(it)] if u else [0])], axis=0))
        x = x_new

    tw = [_dot(x[i], jnp.concatenate([a2[i], mv[i][:PAIR].astype(BF16)], axis=1))
          for i in idx]
    tw_bf = [t.astype(BF16) for t in tw]
    qo = [jnp.concatenate([r2[i], mv[i][PAIR:]], axis=1) - _dot(m_rb[i], tw_bf[i])
          for i in idx]
    bt = [_dot_tn(bh2[i], tw_bf[i]) for i in idx]
    kv = [_dot_tn(kh2[i], v2[i]) for i in idx]
    out = []
    for i in idx:
        g_mat = jnp.where(eye, g_last[i], 0.0) - bt[i][:, :PAIR]
        h_mat = kv[i] - bt[i][:, PAIR:]
        q = qo[i][:CHUNK] + qo[i][CHUNK:]
        out.append((q[:, :PAIR], q[:, PAIR:], g_mat, h_mat))
    return out


def _rwkv_a_kernel(z_ref, zp_ref, zn_ref, mu_ref, w0_ref, wup_ref, a0_ref, aup_ref,
                   gup_ref, kk_ref, ka_ref, rk_ref, hsum_ref, trif_ref, trib_ref,
                   q_out, ol_out, g_out, h_out, bonus_out, gate_out,
                   at_s, bt_s, kt_s, rt_s, bh_s, kh_s, v_s, gl_s, *, tm, seq):
    i = pl.program_id(0)
    z = z_ref[...]
    has_prev = (i * tm) % seq != 0
    has_next = ((i + 1) * tm) % seq != 0
    prev_row = jnp.where(has_prev, zp_ref[SUBLANES - 1:SUBLANES, :], 0.0)
    next_row = jnp.where(has_next, zn_ref[0:1, :], 0.0)
    sub = lax.broadcasted_iota(I32, (SUBLANES, 1), 0)
    z_dn = pltpu.roll(z, 1, axis=0)
    z_dn = jnp.concatenate([jnp.where(sub == 0, prev_row, z_dn[:SUBLANES]), z_dn[SUBLANES:]], axis=0)
    z_up = pltpu.roll(z, tm - 1, axis=0)
    z_up = jnp.concatenate([z_up[:tm - SUBLANES],
                            jnp.where(sub == SUBLANES - 1, next_row, z_up[tm - SUBLANES:])], axis=0)
    zs = z + mu_ref[...] * (0.5 * (z_dn + z_up) - z)

    w = RW_WIDTH
    r = zs[:, :w]
    k = zs[:, w:2 * w]
    v = zs[:, 2 * w:3 * w]
    wd = zs[:, 3 * w:3 * w + 128]
    ad = zs[:, 3 * w + 128:3 * w + 256]
    gd = zs[:, 3 * w + 256:3 * w + 384]

    hsum = hsum_ref[...]
    w_logit = w0_ref[...] + _dot(jnp.tanh(wd), wup_ref[...])
    lw = -_sigmoid(w_logit) * jnp.exp(jnp.float32(-0.5))
    a = _sigmoid(a0_ref[...] + _dot(ad, aup_ref[...]))
    gate_out[...] = _dot(_sigmoid(gd), gup_ref[...]).astype(BF16)
    kkr = k * kk_ref[...]
    kkn = kkr * jnp.minimum(lax.rsqrt(_dot(kkr * kkr, hsum)), 1e12)
    kd = [k * (1.0 + (a[:, d * w:(d + 1) * w] - 1.0) * ka_ref[...]) for d in range(2)]
    bonus_out[...] = (_dot(r * (0.5 * (kd[0] + kd[1])) * rk_ref[...], hsum) * v).astype(BF16)
    v_s[...] = v.astype(BF16)

    for d, tri_ref in enumerate((trif_ref, trib_ref)):
        cols = slice(d * w, (d + 1) * w)
        lw_d = lw[:, cols]
        cum = _dot_rsplit(tri_ref[...], lw_d, 2)
        ends = [c * CHUNK if d == 1 else (c + 1) * CHUNK - 1 for c in range(tm // CHUNK)]
        tot = jnp.concatenate([jnp.broadcast_to(cum[e:e + 1, :], (CHUNK, w)) for e in ends], axis=0)
        g_inv = jnp.exp(-cum)
        g_end = jnp.exp(tot - cum)
        b = kkn * a[:, cols]
        at_s[d] = (kkn * jnp.exp(cum - lw_d)).astype(BF16)
        bt_s[d] = (b * g_inv).astype(BF16)
        kt_s[d] = (kd[d] * g_inv).astype(BF16)
        rt_s[d] = r * jnp.exp(cum)
        bh_s[d] = (b * g_end).astype(BF16)
        kh_s[d] = (kd[d] * g_end).astype(BF16)
        gl_s[d] = jnp.exp(tot)

    same, li, lj, eye = _pair_masks()
    lane = lax.broadcasted_iota(I32, (1, PAIR), 1)
    m0 = lane < HEAD_DIM
    m1 = lane >= HEAD_DIM
    consts = []
    for reverse in (False, True):
        before = (lj > li) if reverse else (lj < li)
        strict = same & before
        incl = same & (before | (li == lj))
        levels = []
        s = 1
        while s < CHUNK:
            blk = same & ((li // (2 * s)) == (lj // (2 * s)))
            hi_row = (li // s) % 2 == 1
            hi_col = (lj // s) % 2 == 1
            levels.append(blk & ((~hi_row & hi_col) if reverse else (hi_row & ~hi_col)))
            s *= 2
        consts.append((strict, incl, levels, eye, m0, m1, reverse))

    def chunk_body(it, carry):
        inst = []
        for j in range(CHUNKS_PER_ITER):
            c = it * CHUNKS_PER_ITER + j
            r0 = pl.multiple_of(c * CHUNK, CHUNK)
            inst += [(c, r0, pl.ds(r0, CHUNK), d, slice(p * PAIR, (p + 1) * PAIR))
                     for d in range(2) for p in range(N_PAIRS)]
        outs = _chunk_pair(
            [at_s[d, rows, ln] for c, r0, rows, d, ln in inst],
            [bt_s[d, rows, ln] for c, r0, rows, d, ln in inst],
            [kt_s[d, rows, ln] for c, r0, rows, d, ln in inst],
            [rt_s[d, rows, ln] for c, r0, rows, d, ln in inst],
            [v_s[rows, ln] for c, r0, rows, d, ln in inst],
            [bh_s[d, rows, ln] for c, r0, rows, d, ln in inst],
            [kh_s[d, rows, ln] for c, r0, rows, d, ln in inst],
            [gl_s[d, pl.ds(r0, 1), ln] for c, r0, rows, d, ln in inst],
            [consts[d] for c, r0, rows, d, ln in inst])
        for (c, r0, rows, d, ln), (qe, ol, g_mat, h_mat) in zip(inst, outs):
            q_out[d, rows, ln] = qe.astype(BF16)
            ol_out[d, rows, ln] = ol.astype(BF16)
            g_out[d, c, :, ln] = (g_mat[:HEAD_DIM] + g_mat[HEAD_DIM:]).astype(BF16)
            h_out[d, c, :, ln] = (h_mat[:HEAD_DIM] + h_mat[HEAD_DIM:]).astype(BF16)
        return carry

    lax.fori_loop(0, tm // (CHUNK * CHUNKS_PER_ITER), chunk_body, 0)


def _rwkv_a(z2d, seq, mu, w0, wup, a0, aup, gup, k_k, k_a, r_k, hsum):
    m = z2d.shape[0]
    tm = min(256, seq)
    nc = tm // CHUNK
    w = RW_WIDTH
    full = lambda shape: pl.BlockSpec(shape, lambda i: (0,) * len(shape))
    last8 = m // 8 - 1
    ti = jnp.arange(tm)[:, None]
    tj = jnp.arange(tm)[None, :]
    same_chunk = (ti // CHUNK) == (tj // CHUNK)
    tri_f = (same_chunk & (tj <= ti)).astype(BF16)
    tri_b = (same_chunk & (tj >= ti)).astype(BF16)
    kern = functools.partial(_rwkv_a_kernel, tm=tm, seq=seq)
    return pl.pallas_call(
        kern,
        grid=(m // tm,),
        in_specs=[
            pl.BlockSpec((tm, RW_COLS), lambda i: (i, 0)),
            pl.BlockSpec((8, RW_COLS), lambda i: (jnp.maximum(i * (tm // 8) - 1, 0), 0)),
            pl.BlockSpec((8, RW_COLS), lambda i: (jnp.minimum((i + 1) * (tm // 8), last8), 0)),
            full((1, RW_COLS)), full((1, 2 * w)), full((128, 2 * w)),
            full((1, 2 * w)), full((128, 2 * w)), full((128, w)),
            full((1, w)), full((1, w)), full((1, w)), full((w, w)), full((tm, tm)), full((tm, tm)),
        ],
        out_specs=[
            pl.BlockSpec((2, tm, w), lambda i: (0, i, 0)),
            pl.BlockSpec((2, tm, w), lambda i: (0, i, 0)),
            pl.BlockSpec((2, nc, HEAD_DIM, w), lambda i: (0, i, 0, 0)),
            pl.BlockSpec((2, nc, HEAD_DIM, w), lambda i: (0, i, 0, 0)),
            pl.BlockSpec((tm, w), lambda i: (i, 0)),
            pl.BlockSpec((tm, w), lambda i: (i, 0)),
        ],
        out_shape=[
            jax.ShapeDtypeStruct((2, m, w), BF16),
            jax.ShapeDtypeStruct((2, m, w), BF16),
            jax.ShapeDtypeStruct((2, m // CHUNK, HEAD_DIM, w), BF16),
            jax.ShapeDtypeStruct((2, m // CHUNK, HEAD_DIM, w), BF16),
            jax.ShapeDtypeStruct((m, w), BF16),
            jax.ShapeDtypeStruct((m, w), BF16),
        ],
        scratch_shapes=[pltpu.VMEM((2, tm, w), BF16)] * 3 + [pltpu.VMEM((2, tm, w), F32)]
        + [pltpu.VMEM((2, tm, w), BF16)] * 2 + [pltpu.VMEM((tm, w), BF16), pltpu.VMEM((2, tm, w), F32)],
        compiler_params=_cparams(("parallel",)),
        name="rwkv_chunk_local",
    )(z2d, z2d, z2d, mu, w0, wup.astype(BF16), a0, aup.astype(BF16), gup.astype(BF16), k_k, k_a, r_k,
      hsum, tri_f, tri_b)


def _rwkv_b_kernel(qf_ref, olf_ref, gf_ref, hf_ref, qb_ref, olb_ref, gb_ref, hb_ref,
                   of_ref, ob_ref, s_ref, *, cb):
    @pl.when(pl.program_id(1) == 0)
    def _():
        s_ref[...] = jnp.zeros_like(s_ref)

    lane = lax.broadcasted_iota(I32, (1, PAIR), 1)

    def block_diag(packed):
        zero = jnp.zeros_like(packed)
        return jnp.concatenate([jnp.where(lane < HEAD_DIM, packed, zero),
                                jnp.where(lane >= HEAD_DIM, packed, zero)], axis=0)

    for step in range(cb):
        inst = []
        for d, refs in enumerate(((qf_ref, olf_ref, gf_ref, hf_ref, of_ref),
                                  (qb_ref, olb_ref, gb_ref, hb_ref, ob_ref))):
            c = cb - 1 - step if d == 1 else step
            for p in range(N_PAIRS):
                inst.append((d, c, slice(c * CHUNK, (c + 1) * CHUNK),
                             slice(p * PAIR, (p + 1) * PAIR)) + refs)
        s_bf = [s_ref[d, :, ln].astype(BF16) for d, c, rows, ln, *_ in inst]
        s_new = [jnp.dot(block_diag(g_ref[c, :, ln]), sb, preferred_element_type=F32)
                 + block_diag(h_ref[c, :, ln])
                 for (d, c, rows, ln, q_ref, ol_ref, g_ref, h_ref, o_ref), sb in zip(inst, s_bf)]
        o_val = [jnp.dot(q_ref[rows, ln], sb, preferred_element_type=F32) + ol_ref[rows, ln]
                 for (d, c, rows, ln, q_ref, ol_ref, g_ref, h_ref, o_ref), sb in zip(inst, s_bf)]
        for (d, c, rows, ln, q_ref, ol_ref, g_ref, h_ref, o_ref), sn, ov in zip(inst, s_new, o_val):
            s_ref[d, :, ln] = sn
            o_ref[rows, ln] = ov.astype(BF16)


def _rwkv_b(qeff, oloc, g_all, h_all, batch, seq):
    m = batch * seq
    w = RW_WIDTH
    cb = min(8, seq // CHUNK)
    tm = cb * CHUNK
    nb = seq // tm

    def fwd(b, j):
        return b * nb + j

    def bwd(b, j):
        return b * nb + nb - 1 - j

    def specs(d, blk):
        return [
            pl.BlockSpec((None, tm, w), lambda b, j: (d, blk(b, j), 0)),
            pl.BlockSpec((None, tm, w), lambda b, j: (d, blk(b, j), 0)),
            pl.BlockSpec((None, cb, HEAD_DIM, w), lambda b, j: (d, blk(b, j), 0, 0)),
            pl.BlockSpec((None, cb, HEAD_DIM, w), lambda b, j: (d, blk(b, j), 0, 0)),
        ]

    return pl.pallas_call(
        functools.partial(_rwkv_b_kernel, cb=cb),
        grid=(batch, nb),
        in_specs=specs(0, fwd) + specs(1, bwd),
        out_specs=[pl.BlockSpec((tm, w), lambda b, j: (fwd(b, j), 0)),
                   pl.BlockSpec((tm, w), lambda b, j: (bwd(b, j), 0))],
        out_shape=[jax.ShapeDtypeStruct((m, w), BF16)] * 2,
        scratch_shapes=[pltpu.VMEM((2, PAIR, w), F32)],
        compiler_params=_cparams(("parallel", "arbitrary")),
        name="rwkv_recurrence",
    )(qeff, oloc, g_all, h_all, qeff, oloc, g_all, h_all)


def _mla_prep_kernel(z_ref, c_ref, s_ref, qn_ref, kvn_ref, qa_ref, qb_ref, kk_ref, kvv_ref, hsel_ref,
                     q_out, k_out, v_out, qmax_out, kmax_out, *, scale):
    z = z_ref[...]
    cos = c_ref[...]
    sin = s_ref[...]
    qd = _rms(z[:, :Q_LORA], qn_ref[...]).astype(BF16)
    kvd = _rms(z[:, Q_LORA:Q_LORA + KV_LORA], kvn_ref[...]).astype(BF16)
    o = Q_LORA + KV_LORA
    kr = z[:, o:o + LANES] * cos + z[:, o + LANES:o + 2 * LANES] * sin
    qa = jnp.dot(qd, qa_ref[...], preferred_element_type=F32)
    qb = jnp.dot(qd, qb_ref[...], preferred_element_type=F32)
    kn = jnp.dot(kvd, kk_ref[...], preferred_element_type=F32)
    v_out[...] = lax.dot_general(kvv_ref[...], kvd, (((1,), (1,)), ((), ())),
                                 preferred_element_type=F32).astype(BF16)
    qs, ks = [], []
    for h in range(MLA_HEADS):
        lanes = slice(h * LANES, (h + 1) * LANES)
        qs.append(((qa[:, lanes] * cos + qb[:, lanes] * sin) * scale).astype(BF16))
        ks.append((kn[:, lanes] + kr).astype(BF16))
        q_out[h] = qs[h]
        k_out[h] = ks[h]
    for vals, out in ((qs, qmax_out), (ks, kmax_out)):
        full = jnp.concatenate([v.astype(F32) for v in vals], axis=1)
        n2 = jnp.dot((full * full).astype(BF16), hsel_ref[...], preferred_element_type=F32)
        out[...] = jnp.broadcast_to(jnp.max(n2, axis=0, keepdims=True), (SUBLANES, LANES))


def _mla_prep(z2d, cos, sin, batch, seq, q_norm, kv_norm, q_a, q_b, kv_k, kv_v):
    tm = min(512, seq)
    nt = seq // tm
    hw = MLA_HEADS * LANES
    scale = float((QK_NOPE + QK_ROPE) ** -0.5 * math.log2(math.e))
    full = lambda shape: pl.BlockSpec(shape, lambda b, i: (0,) * len(shape))
    col_blk = RW_COLS // MLA_IN
    assert col_blk * MLA_IN == RW_COLS
    head_sel = (jnp.arange(hw)[:, None] // LANES == jnp.arange(LANES)[None, :]).astype(BF16)
    q, k, v_t, qmax, kmax = pl.pallas_call(
        functools.partial(_mla_prep_kernel, scale=scale),
        grid=(batch, nt),
        in_specs=[
            pl.BlockSpec((tm, MLA_IN), lambda b, i: (b * nt + i, col_blk)),
            pl.BlockSpec((tm, LANES), lambda b, i: (b * nt + i, 0)),
            pl.BlockSpec((tm, LANES), lambda b, i: (b * nt + i, 0)),
            full((1, Q_LORA)), full((1, KV_LORA)), full((Q_LORA, hw)), full((Q_LORA, hw)),
            full((KV_LORA, hw)), full((MLA_WIDTH, KV_LORA)), full((hw, LANES)),
        ],
        out_specs=[
            pl.BlockSpec((None, MLA_HEADS, tm, LANES), lambda b, i: (b, 0, i, 0)),
            pl.BlockSpec((None, MLA_HEADS, tm, LANES), lambda b, i: (b, 0, i, 0)),
            pl.BlockSpec((None, MLA_WIDTH, tm), lambda b, i: (b, 0, i)),
            pl.BlockSpec((None, None, SUBLANES, LANES), lambda b, i: (b, i, 0, 0)),
            pl.BlockSpec((None, None, SUBLANES, LANES), lambda b, i: (b, i, 0, 0)),
        ],
        out_shape=[
            jax.ShapeDtypeStruct((batch, MLA_HEADS, seq, LANES), BF16),
            jax.ShapeDtypeStruct((batch, MLA_HEADS, seq, LANES), BF16),
            jax.ShapeDtypeStruct((batch, MLA_WIDTH, seq), BF16),
            jax.ShapeDtypeStruct((batch, nt, SUBLANES, LANES), F32),
            jax.ShapeDtypeStruct((batch, nt, SUBLANES, LANES), F32),
        ],
        compiler_params=_cparams(("parallel", "parallel")),
        name="mla_prep",
    )(z2d, cos, sin, q_norm, kv_norm, q_a, q_b, kv_k, kv_v, head_sel)
    bound = jnp.sqrt(jnp.max(qmax, axis=(1, 2)) * jnp.max(kmax, axis=(1, 2)))[:, :MLA_HEADS] * NORM_MARGIN
    small = (bound <= FLASH_SAFE_LOG2).reshape(batch, MLA_HEADS // 2, 2).all(axis=-1)
    return q, k, v_t, small.astype(I32).reshape(-1)


def _flash_kernel(small_ref, q_ref, k_ref, vt_ref, o_ref, m_ref, l_ref, acc_ref):
    j = pl.program_id(3)
    small = small_ref[pl.program_id(0) * pl.num_programs(1) + pl.program_id(1)] != 0

    @pl.when(j == 0)
    def _():
        m_ref[...] = jnp.full_like(m_ref, -jnp.inf)
        l_ref[...] = jnp.zeros_like(l_ref)
        acc_ref[...] = jnp.zeros_like(acc_ref)

    tk = k_ref.shape[1]
    sub = min(FLASH_SUB, tk)
    inst = [(h, slice(b * sub, (b + 1) * sub)) for b in range(tk // sub) for h in range(2)]
    rows = [slice(h * V_HEAD, (h + 1) * V_HEAD) for h in range(2)]
    ones = jnp.ones((ONES_ROWS, sub), BF16)

    def scores():
        return [lax.dot_general(k_ref[h, kb, :], q_ref[h], (((1,), (1,)), ((), ())),
                                preferred_element_type=F32) for h, kb in inst]

    def weighted_values(p):
        pv = [jnp.dot(jnp.concatenate([vt_ref[rows[h], kb], ones], axis=0), x,
                      preferred_element_type=F32) for (h, kb), x in zip(inst, p)]
        return [x[:V_HEAD, :] for x in pv], [x[V_HEAD:V_HEAD + 1, :] for x in pv]

    @pl.when(small)
    def _():
        pv, l_loc = weighted_values([jnp.exp2(x).astype(BF16) for x in scores()])
        for h in range(2):
            mine = [i for i, (hh, _) in enumerate(inst) if hh == h]
            l_ref[h:h + 1, :] = l_ref[h:h + 1, :] + sum(l_loc[i] for i in mine)
            acc_ref[rows[h], :] = acc_ref[rows[h], :] + sum(pv[i] for i in mine)

    @pl.when(jnp.logical_not(small))
    def _():
        s = scores()
        m_loc = [jnp.max(x, axis=0, keepdims=True) for x in s]
        pv, l_loc = weighted_values([jnp.exp2(x - m).astype(BF16) for x, m in zip(s, m_loc)])
        for h in range(2):
            mine = [i for i, (hh, _) in enumerate(inst) if hh == h]
            m_prev = m_ref[h:h + 1, :]
            m_new = m_prev
            for i in mine:
                m_new = jnp.maximum(m_new, m_loc[i])
            alpha = jnp.exp2(m_prev - m_new)
            l_new = alpha * l_ref[h:h + 1, :]
            acc = alpha * acc_ref[rows[h], :]
            for i in mine:
                w = jnp.exp2(m_loc[i] - m_new)
                l_new = l_new + w * l_loc[i]
                acc = acc + w * pv[i]
            m_ref[h:h + 1, :] = m_new
            l_ref[h:h + 1, :] = l_new
            acc_ref[rows[h], :] = acc

    @pl.when(j == pl.num_programs(3) - 1)
    def _():
        inv = 1.0 / l_ref[...]
        o_t = jnp.concatenate([acc_ref[:V_HEAD, :] * inv[0:1, :], acc_ref[V_HEAD:, :] * inv[1:2, :]],
                              axis=0)
        o_ref[...] = o_t.T


def _flash(small, q, k, v_t):
    batch, heads, seq, _ = q.shape
    tq = min(FLASH_TQ, seq)
    tk = min(FLASH_TK, seq)
    return pl.pallas_call(
        _flash_kernel,
        grid_spec=pltpu.PrefetchScalarGridSpec(
            num_scalar_prefetch=1,
            grid=(batch, heads // 2, seq // tq, seq // tk),
            in_specs=[
                pl.BlockSpec((None, 2, tq, LANES), lambda b, p, i, j, sm: (b, p, i, 0)),
                pl.BlockSpec((None, 2, tk, LANES), lambda b, p, i, j, sm: (b, p, j, 0)),
                pl.BlockSpec((None, 2 * V_HEAD, tk), lambda b, p, i, j, sm: (b, p, j)),
            ],
            out_specs=pl.BlockSpec((None, tq, LANES), lambda b, p, i, j, sm: (b, i, p)),
            scratch_shapes=[pltpu.VMEM((2, tq), F32), pltpu.VMEM((2, tq), F32),
                            pltpu.VMEM((2 * V_HEAD, tq), F32)],
        ),
        out_shape=jax.ShapeDtypeStruct((batch, seq, MLA_WIDTH), F32),
        compiler_params=_cparams(("parallel", "parallel", "parallel", "arbitrary")),
        name="mla_flash",
    )(small, q, k, v_t)


def _affinities(x, g, rt_hi, rt_lo):
    x_hi, x_lo = _split_bf16(_rms(x, g), 2)
    nt = (((1,), (1,)), ((), ()))
    logits = (lax.dot_general(rt_hi, x_hi, nt, preferred_element_type=F32)
              + lax.dot_general(rt_hi, x_lo, nt, preferred_element_type=F32)
              + lax.dot_general(rt_lo, x_hi, nt, preferred_element_type=F32))
    e = jnp.exp(logits - jnp.max(logits, axis=0, keepdims=True))
    return e / jnp.sum(e, axis=0, keepdims=True)


def _out_kernel(of_ref, ob_ref, bonus_ref, gate_ref, ym_ref, x_ref, hsum_ref, lw_ref, lb_ref,
                on_ref, w_ref, fg_ref, rth_ref, rtl_ref, o_ref, ot_ref, a_ref):
    o = of_ref[...].astype(F32) + ob_ref[...].astype(F32)
    hsum = hsum_ref[...]
    inv_n = 1.0 / HEAD_DIM
    mean = _dot_lsplit(o, hsum, 2) * inv_n
    d = o - mean
    var = _dot_lsplit(d * d, hsum, 2) * inv_n
    y_rw = (d * lax.rsqrt(var + GN_EPS) * lw_ref[...] + lb_ref[...] + bonus_ref[...]) * gate_ref[...]
    y_mla = _rms(ym_ref[...], on_ref[...])
    w = RW_WIDTH
    out = x_ref[...] + _dot(y_rw, w_ref[:w, :]) + _dot(y_mla, w_ref[w:, :])
    o_ref[...] = out
    _rows_to_tiles(ot_ref, out)
    a_ref[...] = _affinities(out, fg_ref[...], rth_ref[...], rtl_ref[...])


def _out_proj(o_f, o_b, bonus, gate, y_mla, x2d, hsum, lnx_w, lnx_b, o_norm, w_out, seq, ffn_g,
              router_t):
    m, d = x2d.shape
    w = RW_WIDTH
    n_e = router_t.shape[0]
    tm = min(512, seq)
    nt = seq // tm
    row = lambda n: pl.BlockSpec((tm, n), lambda i: (i, 0))
    full = lambda shape: pl.BlockSpec(shape, lambda i: (0,) * len(shape))
    return pl.pallas_call(
        _out_kernel,
        grid=(m // tm,),
        in_specs=[row(w), row(w), row(w), row(w), row(MLA_WIDTH), row(d), full((w, w)),
                  full((1, w)), full((1, w)), full((1, MLA_WIDTH)), full((w + MLA_WIDTH, d)),
                  full((1, d)), full((n_e, d)), full((n_e, d))],
        out_specs=[row(d), pl.BlockSpec((tm * SUBLANES, LANES), lambda i: (i, 0)),
                   pl.BlockSpec((None, n_e, tm), lambda i: (i // nt, 0, i % nt))],
        out_shape=[jax.ShapeDtypeStruct((m, d), F32),
                   jax.ShapeDtypeStruct((m * SUBLANES, LANES), F32),
                   jax.ShapeDtypeStruct((m // seq, n_e, seq), F32)],
        compiler_params=_cparams(("parallel",)),
        name="out_proj",
    )(o_f, o_b, bonus, gate, y_mla, x2d, hsum, lnx_w, lnx_b, o_norm, w_out, ffn_g, *_hi_lo(router_t))


def _threshold_kernel(a_ref, thr_ref, *, cap):
    bits = pltpu.bitcast(a_ref[...], I32)
    n_e = bits.shape[0]

    def search(i, cur):
        cand = cur | jnp.left_shift(jnp.int32(1), 30 - i)
        cnt = jnp.sum(jnp.where(bits >= cand, 1, 0), axis=1, keepdims=True)
        return jnp.where(cnt >= cap, cand, cur)

    thr_ref[...] = lax.fori_loop(0, 31, search, jnp.zeros((n_e, 1), I32))


def _compact_kernel(a_ref, thr_ref, idx_ref, gate_ref, *, cap):
    a = a_ref[...]
    nb = a.shape[0]
    bits = pltpu.bitcast(a, I32)
    thr = thr_ref[...]
    ri = lax.broadcasted_iota(I32, (LANES, LANES), 0)
    ci = lax.broadcasted_iota(I32, (LANES, LANES), 1)
    upper = jnp.where(ri <= ci, 1.0, 0.0).astype(BF16)
    bi = lax.broadcasted_iota(I32, (nb, nb), 0)
    bj = lax.broadcasted_iota(I32, (nb, nb), 1)
    before = jnp.where(bj < bi, 1.0, 0.0).astype(BF16)

    def total(x):
        return jnp.sum(jnp.sum(x, axis=1, keepdims=True), axis=0, keepdims=True)

    def running(mask):
        within = jnp.dot(mask, upper, preferred_element_type=F32)
        tot = jnp.broadcast_to(within[:, LANES - 1:LANES], (nb, LANES))
        return within, jnp.dot(before, tot.astype(BF16), preferred_element_type=F32)

    gt = bits > thr
    eq = bits == thr
    need = cap - total(jnp.where(gt, 1.0, 0.0))
    w_eq, b_eq = running(jnp.where(eq, 1.0, 0.0).astype(BF16))
    sel = gt | (eq & (w_eq + b_eq <= need))
    sel_b = jnp.where(sel, 1.0, 0.0).astype(BF16)
    within, base = running(sel_b)

    tot_row = lax.dot_general(jnp.ones((SUBLANES, LANES), BF16), sel_b, (((1,), (1,)), ((), ())),
                              preferred_element_type=F32)
    base_row = jnp.dot(tot_row.astype(BF16), jnp.where(bi < bj, 1.0, 0.0).astype(BF16),
                       preferred_element_type=F32)
    c_col = lax.broadcasted_iota(I32, (cap, 1), 0).astype(F32)
    in_block = (base_row[0:1, :] <= c_col) & (c_col < base_row[0:1, :] + tot_row[0:1, :])
    onehot = jnp.where(in_block, 1.0, 0.0).astype(BF16)

    a_parts = _split_bf16(a, 3)
    base_hi = jnp.floor(base * (1.0 / 32.0))
    lane = lax.broadcasted_iota(I32, (1, LANES), 1).astype(F32)
    block_id = lax.broadcasted_iota(I32, (nb, LANES), 0).astype(F32)
    table = jnp.concatenate(
        [within.astype(BF16), sel_b] + a_parts
        + [base_hi.astype(BF16), (base - 32.0 * base_hi).astype(BF16), block_id.astype(BF16)], axis=1)
    g = jnp.dot(onehot, table, preferred_element_type=F32)
    part = lambda k: g[:, k * LANES:(k + 1) * LANES]
    a_c = part(2) + part(3) + part(4)
    slot = lax.broadcasted_iota(I32, (cap, LANES), 0).astype(F32)
    target = slot - (32.0 * part(5) + part(6)) + 1.0
    match = (part(0) == target) & (part(1) > 0.5)
    pos = jnp.dot(jnp.where(match, lane, 0.0).astype(BF16), jnp.ones((LANES, LANES), BF16),
                  preferred_element_type=F32)
    gate_ref[...] = jnp.sum(jnp.where(match, a_c, 0.0), axis=1, keepdims=True)
    idx_ref[...] = (LANES * part(7) + pos)[:, 0:1].astype(I32)


def _select(aff_t, cap):
    batch, e, seq = aff_t.shape
    nb = seq // LANES
    thr = pl.pallas_call(
        functools.partial(_threshold_kernel, cap=cap),
        grid=(batch,),
        in_specs=[pl.BlockSpec((None, e, seq), lambda b: (b, 0, 0))],
        out_specs=pl.BlockSpec((None, e, 1), lambda b: (b, 0, 0)),
        out_shape=jax.ShapeDtypeStruct((batch, e, 1), I32),
        compiler_params=_cparams(("parallel",)),
        name="moe_threshold",
    )(aff_t)
    idx, gate = pl.pallas_call(
        functools.partial(_compact_kernel, cap=cap),
        grid=(batch, e),
        in_specs=[pl.BlockSpec((None, None, nb, LANES), lambda b, j: (b, j, 0, 0)),
                  pl.BlockSpec((None, None, 1, 1), lambda b, j: (b, j, 0, 0))],
        out_specs=[pl.BlockSpec((None, None, cap, 1), lambda b, j: (b, j, 0, 0))] * 2,
        out_shape=[jax.ShapeDtypeStruct((batch, e, cap, 1), I32),
                   jax.ShapeDtypeStruct((batch, e, cap, 1), F32)],
        compiler_params=_cparams(("parallel", "parallel")),
        name="moe_compact",
    )(aff_t.reshape(batch, e, nb, LANES), thr.reshape(batch, e, 1, 1))
    return idx.reshape(-1), gate


def _rows_from_tiles(ref):
    n = ref.shape[0] // SUBLANES
    return jnp.concatenate([ref[pl.ds(s, n, stride=SUBLANES), :] for s in range(SUBLANES)], axis=-1)


def _rows_to_tiles(ref, val):
    n = val.shape[0]
    for s in range(SUBLANES):
        ref[pl.ds(s, n, stride=SUBLANES), :] = val[:, s * LANES:(s + 1) * LANES]


def _tile(r):
    return pl.ds(pl.multiple_of(r * SUBLANES, SUBLANES), SUBLANES)


GATHER_UNROLL = 16


def _gather_kernel(idx_ref, x_ref, o_ref, *, cap, n_e):
    base = (pl.program_id(0) * n_e + pl.program_id(1)) * cap

    def body(c, carry):
        o_ref[_tile(c), :] = x_ref[_tile(idx_ref[base + c]), :]
        return carry

    lax.fori_loop(0, cap, body, 0, unroll=min(GATHER_UNROLL, cap))


def _gather(idx_flat, x_tiles, n_e, cap):
    batch, rows, _ = x_tiles.shape
    return pl.pallas_call(
        functools.partial(_gather_kernel, cap=cap, n_e=n_e),
        grid_spec=pltpu.PrefetchScalarGridSpec(
            num_scalar_prefetch=1,
            grid=(batch, n_e),
            in_specs=[pl.BlockSpec((None, rows, LANES), lambda b, e, idx: (b, 0, 0),
                                   pipeline_mode=pl.Buffered(1))],
            out_specs=pl.BlockSpec((None, None, cap * SUBLANES, LANES),
                                   lambda b, e, idx: (b, e, 0, 0)),
        ),
        out_shape=jax.ShapeDtypeStruct((batch, n_e, cap * SUBLANES, LANES), F32),
        compiler_params=_cparams(("parallel", "arbitrary")),
        name="moe_gather",
    )(idx_flat, x_tiles)


def _ffn_kernel(x_ref, gate_ref, g_ref, wg_ref, wu_ref, wd_ref, o_ref, wg_s, wu_s, wd_s):
    @pl.when((pl.program_id(1) == 0) & (pl.program_id(2) == 0))
    def _():
        wg_s[...] = wg_ref[...].astype(BF16)
        wu_s[...] = wu_ref[...].astype(BF16)
        wd_s[...] = wd_ref[...].astype(BF16)

    xn = _rms(_rows_from_tiles(x_ref), g_ref[...]).astype(BF16)
    h1 = jnp.dot(xn, wg_s[...], preferred_element_type=F32)
    h2 = jnp.dot(xn, wu_s[...], preferred_element_type=F32)
    hid = (h1 * _sigmoid(h1) * h2).astype(BF16)
    _rows_to_tiles(o_ref, jnp.dot(hid, wd_s[...], preferred_element_type=F32) * gate_ref[...])


def _expert_ffn(xe, gate_col, g, w_gate, w_up, w_down, layer):
    batch, n_e, rows, _ = xe.shape
    cap = rows // SUBLANES
    d, f = w_gate.shape[2:]
    tc = min(512, cap)
    n_c = cap // tc
    steps = batch * n_c

    def weight_spec(shape, k):
        ahead = min(k + 1, steps - 1)

        def index(e, b, c):
            nxt = (b * n_c + c >= ahead) if ahead > 0 else False
            return (layer, jnp.minimum(e + jnp.where(nxt, 1, 0), n_e - 1), 0, 0)

        return pl.BlockSpec((None, None) + shape, index)

    return pl.pallas_call(
        _ffn_kernel,
        grid=(n_e, batch, n_c),
        in_specs=[
            pl.BlockSpec((None, None, tc * SUBLANES, LANES), lambda e, b, c: (b, e, c, 0)),
            pl.BlockSpec((None, None, tc, 1), lambda e, b, c: (b, e, c, 0)),
            pl.BlockSpec((1, d), lambda e, b, c: (0, 0)),
            weight_spec((d, f), 0), weight_spec((d, f), 1), weight_spec((f, d), 2),
        ],
        out_specs=pl.BlockSpec((None, None, tc * SUBLANES, LANES), lambda e, b, c: (b, e, c, 0)),
        out_shape=jax.ShapeDtypeStruct(xe.shape, F32),
        scratch_shapes=[pltpu.VMEM((d, f), BF16), pltpu.VMEM((d, f), BF16), pltpu.VMEM((f, d), BF16)],
        compiler_params=_cparams(("parallel", "arbitrary", "arbitrary")),
        name="moe_ffn",
    )(xe, gate_col, g, w_gate, w_up, w_down)


SCATTER_GROUP = 16


def _scatter_kernel(idx_ref, y_ref, o_ref, *, cap, n_e):
    e = pl.program_id(1)
    base = (pl.program_id(0) * n_e + e) * cap

    @pl.when(e == 0)
    def _():
        o_ref[...] = jnp.zeros_like(o_ref)

    def body(g, carry):
        c0 = g * SCATTER_GROUP
        rows = [idx_ref[base + c0 + i] for i in range(SCATTER_GROUP)]
        new = [o_ref[_tile(r), :] + y_ref[_tile(c0 + i), :] for i, r in enumerate(rows)]
        for r, v in zip(rows, new):
            o_ref[_tile(r), :] = v
        return carry

    lax.fori_loop(0, cap // SCATTER_GROUP, body, 0)


def _scatter_add(idx_flat, ye, seq):
    batch, n_e, rows, _ = ye.shape
    cap = rows // SUBLANES
    return pl.pallas_call(
        functools.partial(_scatter_kernel, cap=cap, n_e=n_e),
        grid_spec=pltpu.PrefetchScalarGridSpec(
            num_scalar_prefetch=1,
            grid=(batch, n_e),
            in_specs=[pl.BlockSpec((None, None, rows, LANES), lambda b, e, idx: (b, e, 0, 0))],
            out_specs=pl.BlockSpec((None, seq * SUBLANES, LANES), lambda b, e, idx: (b, 0, 0),
                                   pipeline_mode=pl.Buffered(1)),
        ),
        out_shape=jax.ShapeDtypeStruct((batch, seq * SUBLANES, LANES), F32),
        compiler_params=_cparams(("parallel", "arbitrary")),
        name="moe_scatter_add",
    )(idx_flat, ye)


def _ple_kernel(x_ref, d_ref, p_ref, g_ref, wp_ref, wg_ref, fg_ref, o_ref, *, final):
    x = x_ref[...] + _rows_from_tiles(d_ref)
    gate = _sigmoid(_dot(_rms(x, g_ref[...]), wg_ref[...]))
    out = x + _dot(p_ref[...], wp_ref[...]) * gate
    if final:
        out = _rms(out, fg_ref[...])
    o_ref[...] = out


def _ple(x2d, delta_tiles, p_all, layer, g, w_proj, w_gate, final_g, final):
    m, d = x2d.shape
    dp = p_all.shape[1]
    tm = min(512, m)
    first = layer * (m // tm)
    full = lambda shape: pl.BlockSpec(shape, lambda i: (0,) * len(shape))
    return pl.pallas_call(
        functools.partial(_ple_kernel, final=final),
        grid=(m // tm,),
        in_specs=[pl.BlockSpec((tm, d), lambda i: (i, 0)),
                  pl.BlockSpec((tm * SUBLANES, LANES), lambda i: (i, 0)),
                  pl.BlockSpec((tm, dp), lambda i: (first + i, 0)),
                  full((1, d)), full((dp, d)), full((d, d)), full((1, d))],
        out_specs=pl.BlockSpec((tm, d), lambda i: (i, 0)),
        out_shape=jax.ShapeDtypeStruct((m, d), F32),
        compiler_params=_cparams(("parallel",)),
        name="ple_final" if final else "ple",
    )(x2d, delta_tiles, p_all, g, w_proj, w_gate, final_g)


def _hi_lo(w):
    hi = w.astype(BF16)
    return hi, (w - hi.astype(F32)).astype(BF16)


def _rot_cols(w):
    half = QK_ROPE // 2
    return jnp.concatenate([-w[..., half:], w[..., :half]], axis=-1)


def _pad_head(nope, rope):
    lead = (nope if nope is not None else rope).shape[:-1]
    n = nope if nope is not None else jnp.zeros(lead + (QK_NOPE,), F32)
    r = rope if rope is not None else jnp.zeros(lead + (QK_ROPE,), F32)
    return jnp.concatenate([n, r, jnp.zeros(lead + (LANES - QK_NOPE - QK_ROPE,), F32)], axis=-1)


def _block_rows(w_pair):
    z = jnp.zeros_like(w_pair[0])
    return jnp.concatenate([jnp.concatenate([w_pair[0], z], axis=1),
                            jnp.concatenate([z, w_pair[1]], axis=1)], axis=0)


def kernel(x, p, positions, attn_norm, w_in, rw_mu, rw_w0, rw_w_up, rw_a0, rw_a_up, rw_g_up, rw_k_k,
           rw_k_a, rw_r_k, rw_lnx_w, rw_lnx_b, mla_q_norm, mla_q_up, mla_kv_norm, mla_kv_up,
           mla_o_norm, w_out, ffn_norm, router, exp_w_gate, exp_w_up, exp_w_down, ple_norm,
           ple_proj, ple_gate, final_norm):
    batch, seq, d = x.shape
    depth = w_in.shape[0]
    m = batch * seq
    cap = EC_FACTOR * seq // N_EXPERTS
    w = RW_WIDTH

    cos, sin = _rope_tables(positions)
    hsum = (jnp.arange(w)[:, None] // HEAD_DIM == jnp.arange(w)[None, :] // HEAD_DIM).astype(BF16)
    x2d = x.reshape(m, d)

    for i in range(depth):
        w_mla = w_in[i][:, RW_COLS:]
        w_kr = w_mla[:, Q_LORA + KV_LORA:]
        w_ext = jnp.concatenate([w_in[i][:, :RW_COLS], w_mla[:, :Q_LORA + KV_LORA],
                                 _pad_head(None, w_kr), _pad_head(None, _rot_cols(w_kr))],
                                axis=1).astype(BF16)
        q_up = mla_q_up[i].reshape(Q_LORA, MLA_HEADS, QK_NOPE + QK_ROPE)
        q_a = _pad_head(q_up[..., :QK_NOPE], q_up[..., QK_NOPE:]).reshape(Q_LORA, -1).astype(BF16)
        q_b = _pad_head(None, _rot_cols(q_up[..., QK_NOPE:])).reshape(Q_LORA, -1).astype(BF16)
        kv_up = mla_kv_up[i].reshape(KV_LORA, MLA_HEADS, QK_NOPE + V_HEAD)
        kv_k = _pad_head(kv_up[..., :QK_NOPE], None).reshape(KV_LORA, -1).astype(BF16)
        kv_v = kv_up[..., QK_NOPE:].reshape(KV_LORA, MLA_WIDTH).T.astype(BF16)

        z2d = _in_proj(x2d, attn_norm[i][None, :], w_ext)
        qeff, oloc, g_all, h_all, bonus, gate = _rwkv_a(
            z2d, seq, rw_mu[i][None, :], rw_w0[i].reshape(1, 2 * w), _block_rows(rw_w_up[i]),
            rw_a0[i].reshape(1, 2 * w), _block_rows(rw_a_up[i]), rw_g_up[i], rw_k_k[i][None, :],
            rw_k_a[i][None, :], rw_r_k[i].reshape(1, w), hsum)
        o_f, o_b = _rwkv_b(qeff, oloc, g_all, h_all, batch, seq)
        q, k, v_t, small = _mla_prep(z2d, cos, sin, batch, seq, mla_q_norm[i][None, :],
                                     mla_kv_norm[i][None, :], q_a, q_b, kv_k, kv_v)
        y_mla = _flash(small, q, k, v_t).reshape(m, MLA_WIDTH)
        x2d, x_tiles, aff_t = _out_proj(o_f, o_b, bonus, gate, y_mla, x2d, hsum, rw_lnx_w[i][None, :],
                                        rw_lnx_b[i][None, :], mla_o_norm[i][None, :],
                                        w_out[i].astype(BF16), seq, ffn_norm[i][None, :],
                                        router[i].T)

        idx_flat, gates = _select(aff_t, cap)
        xe = _gather(idx_flat, x_tiles.reshape(batch, seq * SUBLANES, LANES), N_EXPERTS, cap)
        ye = _expert_ffn(xe, gates, ffn_norm[i][None, :], exp_w_gate, exp_w_up, exp_w_down, i)
        delta = _scatter_add(idx_flat, ye, seq).reshape(m * SUBLANES, LANES)

        x2d = _ple(x2d, delta, p.reshape(depth * m, -1), i, ple_norm[i][None, :], ple_proj[i].astype(BF16),
                   ple_gate[i].astype(BF16), final_norm[None, :], final=(i == depth - 1))

    return x2d.reshape(batch, seq, d)
```
